```python
import jax, jax.numpy as jnp
from jax import lax
import numpy as np


D_MODEL = 2048
BATCH = 1
SEQ = 8192
DEPTH = 1

HEAD_DIM = 128
DIL_PATTERNS = ((128, 1), (512, 4), (2048, 16))
HEADS_PER_DIL_GROUP = 4
N_HEADS_A = HEADS_PER_DIL_GROUP * len(DIL_PATTERNS)
N_HEADS_B = 8
N_HEADS = N_HEADS_A + N_HEADS_B
WIDTH_A = N_HEADS_A * HEAD_DIM
WIDTH_A_OUT = HEADS_PER_DIL_GROUP * HEAD_DIM
WIDTH_B = N_HEADS_B * HEAD_DIM
QKV_COLS = 3 * (WIDTH_A + WIDTH_B)
BAND = 128
MOBA_BLOCK = 256
MOBA_TOPK = 3
MOBA_Q_CHUNK = 32
D_FF = 5632
N_ADA = 9
EPS = 1e-6
NEG_INF = -1e30

kernel_name = "hybrid_dilated_moba_macaron_block"


def rmsnorm(x, g):
    xf = x.astype(jnp.float32)
    y = xf * lax.rsqrt(jnp.mean(xf * xf, axis=-1, keepdims=True) + EPS)
    return (y * g.astype(jnp.float32)).astype(x.dtype)


def head_rmsnorm(a, g):
    af = a.astype(jnp.float32)
    y = af * lax.rsqrt(jnp.mean(af * af, axis=-1, keepdims=True) + EPS)
    return (y * g.astype(jnp.float32)[None, :, None, :]).astype(a.dtype)


def modulate(x, shift, scale):
    return x * (1.0 + scale[:, None, :]) + shift[:, None, :]


def swiglu(x, w_gate, w_up, w_down):
    return (jax.nn.silu(x @ w_gate) * (x @ w_up)) @ w_down


def alibi_slopes(n):
    return jnp.exp2(-8.0 * jnp.arange(1, n + 1, dtype=jnp.float32) / n)


def head_split(a, n_heads):
    b, t, _ = a.shape
    return a.reshape(b, t, n_heads, HEAD_DIM).transpose(0, 2, 1, 3)


def head_merge(a):
    b, h, t, d = a.shape
    return a.transpose(0, 2, 1, 3).reshape(b, t, h * d)


def dilated_window_attn(q, k, v, slopes, dilation):
    b, h, t, dh = q.shape
    r = dilation
    L = t // r
    n_blk = -(-L // BAND)
    Lp = n_blk * BAND

    def to_sub(a):
        return a.reshape(b, h, L, r, dh).transpose(0, 1, 3, 2, 4)

    pad_q = ((0, 0), (0, 0), (0, 0), (0, Lp - L), (0, 0))
    pad_kv = ((0, 0), (0, 0), (0, 0), (BAND, Lp - L), (0, 0))
    qb = jnp.pad(to_sub(q), pad_q).reshape(b, h, r, n_blk, BAND, dh)
    kb = jnp.pad(to_sub(k), pad_kv).reshape(b, h, r, n_blk + 1, BAND, dh)
    vb = jnp.pad(to_sub(v), pad_kv).reshape(b, h, r, n_blk + 1, BAND, dh)
    k_band = jnp.concatenate([kb[:, :, :, :-1], kb[:, :, :, 1:]], axis=4)
    v_band = jnp.concatenate([vb[:, :, :, :-1], vb[:, :, :, 1:]], axis=4)

    s = jnp.einsum('bhrnqd,bhrnkd->bhrnqk', qb, k_band).astype(jnp.float32) * (HEAD_DIM ** -0.5)
    qi = jnp.arange(BAND)[:, None]
    ki = jnp.arange(2 * BAND)[None, :]
    dist = BAND + qi - ki
    key_idx = jnp.arange(n_blk)[:, None, None] * BAND - BAND + ki[None]
    valid = (dist >= 0)[None] & (dist <= BAND)[None] & (key_idx >= 0)
    s = s - slopes[None, :, None, None, None, None] * (dist * r).astype(jnp.float32)
    s = jnp.where(valid, s, NEG_INF)
    m = jnp.max(s, axis=-1, keepdims=True)
    p = jnp.exp(s - m)
    denom = jnp.sum(p, axis=-1, keepdims=True)
    o = jnp.einsum('bhrnqk,bhrnkd->bhrnqd', (p / denom).astype(v.dtype), v_band)
    lse = (m + jnp.log(denom))[..., 0]
    o = o.reshape(b, h, r, Lp, dh)[:, :, :, :L].transpose(0, 1, 3, 2, 4).reshape(b, h, t, dh)
    lse = lse.reshape(b, h, r, Lp)[:, :, :, :L].transpose(0, 1, 3, 2).reshape(b, h, t)
    return o, lse


def moba_attn(q, k, v, slopes):
    b, h, t, dh = q.shape
    nb = -(-t // MOBA_BLOCK)
    tp = nb * MOBA_BLOCK
    pad = ((0, 0), (0, 0), (0, tp - t), (0, 0))
    kblk = jnp.pad(k, pad).reshape(b, h, nb, MOBA_BLOCK, dh)
    vblk = jnp.pad(v, pad).reshape(b, h, nb, MOBA_BLOCK, dh)
    kmean = jnp.mean(kblk.astype(jnp.float32), axis=3)
    gate = jnp.einsum('bhtd,bhnd->bhtn', q.astype(jnp.float32), kmean)
    pos = jnp.arange(t)
    qblk = pos // MOBA_BLOCK
    past = jnp.arange(nb)[None, :] < qblk[:, None]
    gate = jnp.where(past, gate, NEG_INF)
    k_sel = min(MOBA_TOPK, nb)
    _, top_idx = lax.top_k(gate, k_sel)
    top_valid = top_idx < qblk[:, None]
    own = jnp.broadcast_to(qblk[:, None], (b, h, t, 1)).astype(top_idx.dtype)
    sel_idx = jnp.concatenate([top_idx, own], axis=-1)
    sel_valid = jnp.concatenate([top_valid, jnp.ones((b, h, t, 1), dtype=bool)], axis=-1)

    nc = t // MOBA_Q_CHUNK

    def to_chunks(a):
        return jnp.moveaxis(a.reshape(b, h, nc, MOBA_Q_CHUNK, a.shape[-1]), 2, 0)

    q_c = to_chunks(q)
    idx_c = to_chunks(sel_idx)
    ok_c = to_chunks(sel_valid)
    pos_c = pos.reshape(nc, MOBA_Q_CHUNK)
    gather_blocks = jax.vmap(jax.vmap(lambda blk, i: blk[i]))

    def one_chunk(args):
        qx, ix, okx, px = args
        kg = gather_blocks(kblk, ix)
        vg = gather_blocks(vblk, ix)
        s = jnp.einsum('bhcd,bhcskd->bhcsk', qx, kg).astype(jnp.float32) * (HEAD_DIM ** -0.5)
        key_pos = ix[..., None] * MOBA_BLOCK + jnp.arange(MOBA_BLOCK)
        dist = px[None, None, :, None, None] - key_pos
        mask = okx[..., None] & (dist >= 0)
        s = s - slopes[None, :, None, None, None] * dist.astype(jnp.float32)
        s = jnp.where(mask, s, NEG_INF)
        p = jax.nn.softmax(s.reshape(b, h, MOBA_Q_CHUNK, -1), axis=-1).reshape(s.shape)
        return jnp.einsum('bhcsk,bhcskd->bhcd', p.astype(vg.dtype), vg)

    out = lax.map(one_chunk, (q_c, idx_c, ok_c, pos_c))
    return jnp.moveaxis(out, 0, 2).reshape(b, h, t, dh)


def hybrid_layer(x, c, w_ada, b_ada, g_ffn1, ffn1_w_gate, ffn1_w_up, ffn1_w_down,
                 g_mix, w_in, q_norm, k_norm, w_gate, w_branch_a, w_branch_b, w_out,
                 g_ffn2, ffn2_w_gate, ffn2_w_up, ffn2_w_down):
    ada = jax.nn.silu(c) @ w_ada + b_ada
    sh1, sc1, gt1, sh2, sc2, gt2, sh3, sc3, gt3 = jnp.split(ada, N_ADA, axis=-1)

    u = modulate(rmsnorm(x, g_ffn1), sh1, sc1)
    x = x + 0.5 * gt1[:, None, :] * swiglu(u, ffn1_w_gate, ffn1_w_up, ffn1_w_down)

    u = modulate(rmsnorm(x, g_mix), sh2, sc2)
    qkv = u @ w_in
    qa, ka, va, qb, kb, vb = jnp.split(
        qkv, np.cumsum([WIDTH_A, WIDTH_A, WIDTH_A, WIDTH_B, WIDTH_B])[:5].tolist(), axis=-1)
    qa = head_rmsnorm(head_split(qa, N_HEADS_A), q_norm[:N_HEADS_A])
    ka = head_rmsnorm(head_split(ka, N_HEADS_A), k_norm[:N_HEADS_A])
    va = head_split(va, N_HEADS_A)
    qb = head_rmsnorm(head_split(qb, N_HEADS_B), q_norm[N_HEADS_A:])
    kb = head_rmsnorm(head_split(kb, N_HEADS_B), k_norm[N_HEADS_A:])
    vb = head_split(vb, N_HEADS_B)
    slopes = alibi_slopes(N_HEADS)

    outs, lses = [], []
    for g, (_, dil) in enumerate(DIL_PATTERNS):
        hs = slice(g * HEADS_PER_DIL_GROUP, (g + 1) * HEADS_PER_DIL_GROUP)
        o, lse = dilated_window_attn(qa[:, hs], ka[:, hs], va[:, hs], slopes[hs], dil)
        outs.append(o)
        lses.append(lse)
    outs = jnp.stack(outs, axis=0)
    w_den = jax.nn.softmax(jnp.stack(lses, axis=0), axis=0)
    y_a = jnp.sum(w_den[..., None] * outs.astype(jnp.float32), axis=0).astype(x.dtype)
    y_a = head_merge(y_a)

    y_b = head_merge(moba_attn(qb, kb, vb, slopes[N_HEADS_A:]))

    g_a, g_b = jnp.split(jax.nn.sigmoid(u @ w_gate), 2, axis=-1)
    merged = g_a * (y_a @ w_branch_a) + g_b * (y_b @ w_branch_b)
    x = x + gt2[:, None, :] * (merged @ w_out)

    u = modulate(rmsnorm(x, g_ffn2), sh3, sc3)
    x = x + 0.5 * gt3[:, None, :] * swiglu(u, ffn2_w_gate, ffn2_w_up, ffn2_w_down)
    return x


def setup_inputs(seed: int = 0) -> dict:
    key = jax.random.key(seed)
    ks = jax.random.split(key, 24)

    def nrm(k, shape, scale):
        return jax.random.normal(k, shape, jnp.float32) * scale

    def gain(k, shape):
        return 1.0 + 0.01 * jax.random.normal(k, shape, jnp.float32)

    d = D_MODEL
    return {
        'x': nrm(ks[0], (BATCH, SEQ, d), 1.0),
        'c': nrm(ks[1], (BATCH, d), 1.0),
        'w_ada': nrm(ks[2], (DEPTH, d, N_ADA * d), d ** -0.5),
        'b_ada': nrm(ks[3], (DEPTH, N_ADA * d), 0.01),
        'g_ffn1': gain(ks[4], (DEPTH, d)),
        'ffn1_w_gate': nrm(ks[5], (DEPTH, d, D_FF), d ** -0.5),
        'ffn1_w_up': nrm(ks[6], (DEPTH, d, D_FF), d ** -0.5),
        'ffn1_w_down': nrm(ks[7], (DEPTH, D_FF, d), D_FF ** -0.5),
        'g_mix': gain(ks[8], (DEPTH, d)),
        'w_in': nrm(ks[9], (DEPTH, d, QKV_COLS), d ** -0.5),
        'q_norm': gain(ks[10], (DEPTH, N_HEADS, HEAD_DIM)),
        'k_norm': gain(ks[11], (DEPTH, N_HEADS, HEAD_DIM)),
        'w_gate': nrm(ks[12], (DEPTH, d, 2 * d), d ** -0.5),
        'w_branch_a': nrm(ks[13], (DEPTH, WIDTH_A_OUT, d), WIDTH_A_OUT ** -0.5),
        'w_branch_b': nrm(ks[14], (DEPTH, WIDTH_B, d), WIDTH_B ** -0.5),
        'w_out': nrm(ks[15], (DEPTH, d, d), d ** -0.5),
        'g_ffn2': gain(ks[16], (DEPTH, d)),
        'ffn2_w_gate': nrm(ks[17], (DEPTH, d, D_FF), d ** -0.5),
        'ffn2_w_up': nrm(ks[18], (DEPTH, d, D_FF), d ** -0.5),
        'ffn2_w_down': nrm(ks[19], (DEPTH, D_FF, d), D_FF ** -0.5),
    }


def reference(x, c, w_ada, b_ada, g_ffn1, ffn1_w_gate, ffn1_w_up, ffn1_w_down,
              g_mix, w_in, q_norm, k_norm, w_gate, w_branch_a, w_branch_b, w_out,
              g_ffn2, ffn2_w_gate, ffn2_w_up, ffn2_w_down):
    for l in range(DEPTH):
        x = hybrid_layer(x, c, w_ada[l], b_ada[l], g_ffn1[l], ffn1_w_gate[l], ffn1_w_up[l],
                         ffn1_w_down[l], g_mix[l], w_in[l], q_norm[l], k_norm[l], w_gate[l],
                         w_branch_a[l], w_branch_b[l], w_out[l], g_ffn2[l], ffn2_w_gate[l],
                         ffn2_w_up[l], ffn2_w_down[l])
    return x
```

```python
import functools

import jax
import jax.numpy as jnp
from jax import lax
from jax.experimental import pallas as pl
from jax.experimental.pallas import tpu as pltpu

HEAD_DIM = 128
DIL_PATTERNS = ((128, 1), (512, 4), (2048, 16))
HEADS_PER_DIL_GROUP = 4
N_HEADS_A = HEADS_PER_DIL_GROUP * len(DIL_PATTERNS)
N_HEADS_B = 8
N_HEADS = N_HEADS_A + N_HEADS_B
BAND = 128
MOBA_BLOCK = 256
MOBA_TOPK = 3
N_ADA = 9
EPS = 1e-6
NEG_INF = -1e30

LANES = 128
VMEM_LIMIT = 56 * 1024 * 1024

BF16 = jnp.bfloat16
F32 = jnp.float32

_NT = (((1,), (1,)), ((), ()))


def _params(*sem):
    return pltpu.CompilerParams(dimension_semantics=sem, vmem_limit_bytes=VMEM_LIMIT)


def _sigmoid(x):
    return 1.0 / (1.0 + jnp.exp(-x))


def _norm_modulate(x, g, shift, scale):
    ms = jnp.mean(x * x, axis=-1, keepdims=True)
    y = (x * lax.rsqrt(ms + EPS)) * g
    return y * (1.0 + scale) + shift


def _ada_kernel(c_ref, w_ref, b_ref, o_ref):
    c = c_ref[...]
    s = c * _sigmoid(c)
    o_ref[...] = jnp.sum(s * w_ref[...], axis=0, keepdims=True) + b_ref[...]


def _ada(c, w_ada, b_ada, *, tn=1024):
    d, n = w_ada.shape
    return pl.pallas_call(
        _ada_kernel,
        grid=(n // tn,),
        in_specs=[pl.BlockSpec((d, 1), lambda j: (0, 0)),
                  pl.BlockSpec((d, tn), lambda j: (0, j)),
                  pl.BlockSpec((1, tn), lambda j: (0, j))],
        out_specs=pl.BlockSpec((1, tn), lambda j: (0, j)),
        out_shape=jax.ShapeDtypeStruct((1, n), F32),
        compiler_params=_params("arbitrary"),
        name="ada",
    )(c.reshape(d, 1), w_ada, b_ada.reshape(1, n))


def _ffn_kernel(x_ref, ada_ref, g_ref, wg_ref, wu_ref, wd_ref, o_ref, u_ref, acc_ref, *, sub):
    f = pl.program_id(1)

    @pl.when(f == 0)
    def _():
        u = _norm_modulate(x_ref[...], g_ref[...],
                           ada_ref[3 * sub:3 * sub + 1, :], ada_ref[3 * sub + 1:3 * sub + 2, :])
        u_ref[...] = u.astype(BF16)
        acc_ref[...] = jnp.zeros_like(acc_ref)

    u = u_ref[...]
    hg = jnp.dot(u, wg_ref[...], preferred_element_type=F32)
    hu = jnp.dot(u, wu_ref[...], preferred_element_type=F32)
    h = (hg * _sigmoid(hg)) * hu
    acc_ref[...] += jnp.dot(h.astype(BF16), wd_ref[...], preferred_element_type=F32)

    @pl.when(f == pl.num_programs(1) - 1)
    def _():
        gate = ada_ref[3 * sub + 2:3 * sub + 3, :]
        o_ref[...] = x_ref[...] + (0.5 * gate) * acc_ref[...]


def _ffn(x, ada, g, w_gate, w_up, w_down, *, sub, tm=512, tf=512):
    t, d = x.shape
    dff = w_gate.shape[1]
    return pl.pallas_call(
        functools.partial(_ffn_kernel, sub=sub),
        grid=(t // tm, dff // tf),
        in_specs=[pl.BlockSpec((tm, d), lambda i, f: (i, 0)),
                  pl.BlockSpec((N_ADA, d), lambda i, f: (0, 0)),
                  pl.BlockSpec((1, d), lambda i, f: (0, 0)),
                  pl.BlockSpec((d, tf), lambda i, f: (0, f)),
                  pl.BlockSpec((d, tf), lambda i, f: (0, f)),
                  pl.BlockSpec((tf, d), lambda i, f: (f, 0))],
        out_specs=pl.BlockSpec((tm, d), lambda i, f: (i, 0)),
        out_shape=jax.ShapeDtypeStruct((t, d), F32),
        scratch_shapes=[pltpu.VMEM((tm, d), BF16), pltpu.VMEM((tm, d), F32)],
        compiler_params=_params("parallel", "arbitrary"),
        name=f"ffn{sub}",
    )(x, ada, g.reshape(1, d), w_gate, w_up, w_down)


def _proj_heads_kernel(x_ref, ada_ref, g_ref, w_ref, gain_ref, o_ref, u_ref, *, heads_per_blk, norm_blocks):
    j = pl.program_id(1)

    @pl.when(j == 0)
    def _():
        u = _norm_modulate(x_ref[...], g_ref[...], ada_ref[3:4, :], ada_ref[4:5, :])
        u_ref[...] = u.astype(BF16)

    res = jnp.dot(u_ref[...], w_ref[...], preferred_element_type=F32)

    is_norm = functools.reduce(jnp.logical_or, [(j >= lo) & (j < hi) for lo, hi in norm_blocks])

    @pl.when(is_norm)
    def _():
        for h in range(heads_per_blk):
            a = res[:, h * HEAD_DIM:(h + 1) * HEAD_DIM]
            ms = jnp.mean(a * a, axis=-1, keepdims=True)
            y = (a * lax.rsqrt(ms + EPS)) * gain_ref[0, h:h + 1, :]
            o_ref[h] = y.astype(BF16)

    @pl.when(jnp.logical_not(is_norm))
    def _():
        for h in range(heads_per_blk):
            o_ref[h] = res[:, h * HEAD_DIM:(h + 1) * HEAD_DIM].astype(BF16)


def _proj_heads(x, ada, g, w, gains, norm_blocks, *, tm=512, heads_per_blk=4):
    t, d = x.shape
    n = w.shape[1]
    tn = heads_per_blk * HEAD_DIM
    n_heads = n // HEAD_DIM
    return pl.pallas_call(
        functools.partial(_proj_heads_kernel, heads_per_blk=heads_per_blk, norm_blocks=norm_blocks),
        grid=(t // tm, n // tn),
        in_specs=[pl.BlockSpec((tm, d), lambda i, j: (i, 0)),
                  pl.BlockSpec((N_ADA, d), lambda i, j: (0, 0)),
                  pl.BlockSpec((1, d), lambda i, j: (0, 0)),
                  pl.BlockSpec((d, tn), lambda i, j: (0, j)),
                  pl.BlockSpec((1, heads_per_blk, HEAD_DIM), lambda i, j: (j, 0, 0))],
        out_specs=pl.BlockSpec((heads_per_blk, tm, HEAD_DIM), lambda i, j: (j, i, 0)),
        out_shape=jax.ShapeDtypeStruct((n_heads, t, HEAD_DIM), BF16),
        scratch_shapes=[pltpu.VMEM((tm, d), BF16)],
        compiler_params=_params("parallel", "arbitrary"),
        name="proj_qkv",
    )(x, ada, g.reshape(1, d), w, gains.reshape(n // tn, heads_per_blk, HEAD_DIM))


def _proj_gate_kernel(x_ref, ada_ref, g_ref, w_ref, o_ref, u_ref):
    @pl.when(pl.program_id(1) == 0)
    def _():
        u = _norm_modulate(x_ref[...], g_ref[...], ada_ref[3:4, :], ada_ref[4:5, :])
        u_ref[...] = u.astype(BF16)

    res = jnp.dot(u_ref[...], w_ref[...], preferred_element_type=F32)
    o_ref[...] = _sigmoid(res).astype(BF16)


def _proj_gate(x, ada, g, w, *, tm=512, tn=512):
    t, d = x.shape
    n = w.shape[1]
    return pl.pallas_call(
        _proj_gate_kernel,
        grid=(t // tm, n // tn),
        in_specs=[pl.BlockSpec((tm, d), lambda i, j: (i, 0)),
                  pl.BlockSpec((N_ADA, d), lambda i, j: (0, 0)),
                  pl.BlockSpec((1, d), lambda i, j: (0, 0)),
                  pl.BlockSpec((d, tn), lambda i, j: (0, j))],
        out_specs=pl.BlockSpec((tm, tn), lambda i, j: (i, j)),
        out_shape=jax.ShapeDtypeStruct((t, n), BF16),
        scratch_shapes=[pltpu.VMEM((tm, d), BF16)],
        compiler_params=_params("parallel", "arbitrary"),
        name="proj_gate",
    )(x, ada, g.reshape(1, d), w)


DIL_TOKENS = 2048


def _band_block(q, k2, v2, bias):
    s = lax.dot_general(q, k2, _NT, preferred_element_type=F32) + bias
    m = jnp.max(s, axis=-1, keepdims=True)
    p = jnp.exp(s - m)
    denom = jnp.sum(p, axis=-1, keepdims=True)
    o = jnp.dot(p.astype(BF16), v2, preferred_element_type=F32) / denom
    lse = jnp.broadcast_to(m + jnp.log(denom), (BAND, LANES))
    return o, lse


def _dilated_kernel(slopes_ref,
                    q0, k0, v0, kp0, vp0,
                    q1, k1, v1, kp1, vp1,
                    q2, k2, v2, kp2, vp2,
                    y_ref, o_scr, l_scr):
    j = pl.program_id(0)
    b = pl.program_id(1)

    qi = lax.broadcasted_iota(jnp.int32, (BAND, 2 * BAND), 0)
    ki = lax.broadcasted_iota(jnp.int32, (BAND, 2 * BAND), 1)
    dist = BAND + qi - ki
    in_band = (dist >= 0) & (dist <= BAND)
    dist_f = dist.astype(F32)
    first_ok = (ki + jnp.minimum(b, 1) * BAND) >= BAND

    groups = ((q0, k0, v0, kp0, vp0), (q1, k1, v1, kp1, vp1), (q2, k2, v2, kp2, vp2))
    for g, (_, r) in enumerate(DIL_PATTERNS):
        q_ref, k_ref, v_ref, kp_ref, vp_ref = groups[g]
        slope = slopes_ref[g * HEADS_PER_DIL_GROUP + j]
        bias = jnp.where(in_band, dist_f * (-slope * r), NEG_INF)
        bias_first = jnp.where(first_ok, bias, NEG_INF)
        n_blk = DIL_TOKENS // (r * BAND)
        for rho in range(r):
            cols = slice(rho * HEAD_DIM, (rho + 1) * HEAD_DIM)

            def store(i, o, lse, g=g, r=r, rho=rho):
                rows = pl.ds(i * (BAND * r) + rho, BAND, stride=r) if r > 1 else pl.ds(i * BAND, BAND)
                o_scr.at[g][rows, :] = o
                l_scr.at[g][rows, :] = lse

            kk = jnp.concatenate([kp_ref[:, cols], k_ref[0:BAND, cols]], axis=0)
            vv = jnp.concatenate([vp_ref[:, cols], v_ref[0:BAND, cols]], axis=0)
            o, lse = _band_block(q_ref[0:BAND, cols], kk, vv, bias_first)
            store(0, o, lse)

            if n_blk > 1:
                def body(i, carry, q_ref=q_ref, k_ref=k_ref, v_ref=v_ref, cols=cols, bias=bias, store=store):
                    lo = pl.multiple_of(i * BAND - BAND, BAND)
                    qs = pl.multiple_of(i * BAND, BAND)
                    o, lse = _band_block(q_ref[pl.ds(qs, BAND), cols],
                                         k_ref[pl.ds(lo, 2 * BAND), cols],
                                         v_ref[pl.ds(lo, 2 * BAND), cols], bias)
                    store(i, o, lse)
                    return carry
                lax.fori_loop(1, n_blk, body, 0)

    l0, l1, l2 = l_scr[0], l_scr[1], l_scr[2]
    mx = jnp.maximum(jnp.maximum(l0, l1), l2)
    e0, e1, e2 = jnp.exp(l0 - mx), jnp.exp(l1 - mx), jnp.exp(l2 - mx)
    tot = e0 + e1 + e2
    y = (e0 / tot) * o_scr[0] + (e1 / tot) * o_scr[1] + (e2 / tot) * o_scr[2]
    y_ref[...] = y.astype(BF16)


def _dilated(qkvh, slopes):
    n_slots, t, dh = qkvh.shape
    nb = t // DIL_TOKENS
    hq, hk, hv = 0, N_HEADS_A, 2 * N_HEADS_A
    args, specs = [], []
    for g, (_, r) in enumerate(DIL_PATTERNS):
        view = qkvh.reshape(n_slots, t // r, r * dh)
        rows = DIL_TOKENS // r
        prev_per_blk = rows // BAND
        width = r * dh
        off = g * HEADS_PER_DIL_GROUP

        def cur(base, rows=rows, width=width, off=off):
            return pl.BlockSpec((None, rows, width), lambda j, b: (base + off + j, b, 0))

        def prev(base, width=width, off=off, ppb=prev_per_blk):
            return pl.BlockSpec((None, BAND, width),
                                lambda j, b: (base + off + j, jnp.maximum(b * ppb - 1, 0), 0))

        args += [view] * 5
        specs += [cur(hq), cur(hk), cur(hv), prev(hk), prev(hv)]

    return pl.pallas_call(
        _dilated_kernel,
        grid=(HEADS_PER_DIL_GROUP, nb),
        in_specs=[pl.BlockSpec(memory_space=pltpu.SMEM)] + specs,
        out_specs=pl.BlockSpec((DIL_TOKENS, dh), lambda j, b: (b, j)),
        out_shape=jax.ShapeDtypeStruct((t, HEADS_PER_DIL_GROUP * dh), BF16),
        scratch_shapes=[pltpu.VMEM((len(DIL_PATTERNS), DIL_TOKENS, dh), F32),
                        pltpu.VMEM((len(DIL_PATTERNS), DIL_TOKENS, LANES), F32)],
        compiler_params=_params("parallel", "arbitrary"),
        name="dilated",
    )(slopes, *args)


def _moba_kernel(slopes_ref, q_ref, k_ref, v_ref, y_ref, kmean_scr, m_scr, l_scr, acc_scr, *, n_blocks):
    h = pl.program_id(0)
    qb = pl.program_id(1)
    blk = MOBA_BLOCK
    slope = slopes_ref[N_HEADS_A + h]

    @pl.when(qb == 0)
    def _():
        kmean_scr[...] = jnp.zeros_like(kmean_scr)

        def body(n, carry):
            kn = k_ref[pl.ds(pl.multiple_of(n * blk, blk), blk), :].astype(F32)
            kmean_scr[pl.ds(n, 1), :] = jnp.mean(kn, axis=0, keepdims=True)
            return carry
        lax.fori_loop(0, n_blocks, body, 0)

    q = q_ref[...]

    gate = lax.dot_general(q.astype(F32), kmean_scr[...], _NT,
                           precision=lax.Precision.HIGHEST, preferred_element_type=F32)
    blk_id = lax.broadcasted_iota(jnp.int32, (blk, LANES), 1)
    blk_id_f = blk_id.astype(F32)
    past = blk_id < qb
    work = jnp.where(past, gate, NEG_INF)
    sel = jnp.zeros((blk, LANES), F32)
    for _ in range(MOBA_TOPK):
        mx = jnp.max(work, axis=-1, keepdims=True)
        first = jnp.min(jnp.where(work == mx, blk_id_f, float(LANES)), axis=-1, keepdims=True)
        hit = blk_id_f == first
        sel = jnp.where(hit, 1.0, sel)
        work = jnp.where(hit, -jnp.inf, work)
    sel = jnp.where(past, sel, 0.0)

    q_pos = lax.broadcasted_iota(jnp.int32, (blk, blk), 0)
    k_pos = lax.broadcasted_iota(jnp.int32, (blk, blk), 1)
    k_iota = lax.broadcasted_iota(jnp.int32, (1, blk), 1)

    own = pl.multiple_of(qb * blk, blk)
    s = lax.dot_general(q, k_ref[pl.ds(own, blk), :], _NT, preferred_element_type=F32)
    s = jnp.where(k_pos <= q_pos, s + slope * k_iota.astype(F32), NEG_INF)
    m0 = jnp.max(s, axis=-1, keepdims=True)
    p = jnp.exp(s - m0)
    m_scr[...] = m0
    l_scr[...] = jnp.sum(p, axis=-1, keepdims=True)
    acc_scr[...] = jnp.dot(p.astype(BF16), v_ref[pl.ds(own, blk), :], preferred_element_type=F32)

    def past_block(n, carry):
        start = pl.multiple_of(n * blk, blk)
        s = lax.dot_general(q, k_ref[pl.ds(start, blk), :], _NT, preferred_element_type=F32)
        s = s + slope * (k_iota + (n - qb) * blk).astype(F32)
        chosen = jnp.sum(jnp.where(blk_id == n, sel, 0.0), axis=-1, keepdims=True) > 0.5
        m_old = m_scr[...]
        m_new = jnp.where(chosen, jnp.maximum(m_old, jnp.max(s, axis=-1, keepdims=True)), m_old)
        p = jnp.exp(s - m_new)
        alpha = jnp.exp(m_old - m_new)
        pv = jnp.dot(p.astype(BF16), v_ref[pl.ds(start, blk), :], preferred_element_type=F32)
        l_scr[...] = alpha * l_scr[...] + jnp.where(chosen, jnp.sum(p, axis=-1, keepdims=True), 0.0)
        acc_scr[...] = alpha * acc_scr[...] + jnp.where(chosen, pv, 0.0)
        m_scr[...] = m_new
        return carry

    lax.fori_loop(0, qb, past_block, 0)
    y_ref[...] = (acc_scr[...] / l_scr[...]).astype(BF16)


def _moba(qkvh, slopes):
    n_slots, t, dh = qkvh.shape
    hq = 3 * N_HEADS_A
    hk, hv = hq + N_HEADS_B, hq + 2 * N_HEADS_B
    n_blocks = t // MOBA_BLOCK
    assert n_blocks <= LANES, "block gate keeps one block per lane"
    return pl.pallas_call(
        functools.partial(_moba_kernel, n_blocks=n_blocks),
        grid=(N_HEADS_B, n_blocks),
        in_specs=[pl.BlockSpec(memory_space=pltpu.SMEM),
                  pl.BlockSpec((None, MOBA_BLOCK, dh), lambda h, i: (hq + h, i, 0)),
                  pl.BlockSpec((None, t, dh), lambda h, i: (hk + h, 0, 0)),
                  pl.BlockSpec((None, t, dh), lambda h, i: (hv + h, 0, 0))],
        out_specs=pl.BlockSpec((MOBA_BLOCK, dh), lambda h, i: (i, h)),
        out_shape=jax.ShapeDtypeStruct((t, N_HEADS_B * dh), BF16),
        scratch_shapes=[pltpu.VMEM((LANES, dh), F32),
                        pltpu.VMEM((MOBA_BLOCK, 1), F32),
                        pltpu.VMEM((MOBA_BLOCK, 1), F32),
                        pltpu.VMEM((MOBA_BLOCK, dh), F32)],
        compiler_params=_params("arbitrary", "arbitrary"),
        name="moba",
    )(slopes, qkvh, qkvh, qkvh)


def _merge_kernel(x_ref, ada_ref, ya_ref, yb_ref, gates_a_ref, gates_b_ref, wa_ref, wb_ref, wo_ref, o_ref):
    pa = jnp.dot(ya_ref[...], wa_ref[...], preferred_element_type=F32)
    pb = jnp.dot(yb_ref[...], wb_ref[...], preferred_element_type=F32)
    merged = gates_a_ref[...].astype(F32) * pa + gates_b_ref[...].astype(F32) * pb
    out = jnp.dot(merged.astype(BF16), wo_ref[...], preferred_element_type=F32)
    o_ref[...] = x_ref[...] + ada_ref[5:6, :] * out


def _merge(x, ada, y_a, y_b, gates, w_a, w_b, w_o, *, tm=256):
    t, d = x.shape
    const = lambda i: (0, 0)
    return pl.pallas_call(
        _merge_kernel,
        grid=(t // tm,),
        in_specs=[pl.BlockSpec((tm, d), lambda i: (i, 0)),
                  pl.BlockSpec((N_ADA, d), const),
                  pl.BlockSpec((tm, y_a.shape[1]), lambda i: (i, 0)),
                  pl.BlockSpec((tm, y_b.shape[1]), lambda i: (i, 0)),
                  pl.BlockSpec((tm, d), lambda i: (i, 0)),
                  pl.BlockSpec((tm, d), lambda i: (i, 1)),
                  pl.BlockSpec(w_a.shape, const),
                  pl.BlockSpec(w_b.shape, const),
                  pl.BlockSpec(w_o.shape, const)],
        out_specs=pl.BlockSpec((tm, d), lambda i: (i, 0)),
        out_shape=jax.ShapeDtypeStruct((t, d), F32),
        compiler_params=_params("parallel"),
        name="merge",
    )(x, ada, y_a, y_b, gates, gates, w_a, w_b, w_o)


def _layer(x, c, w_ada, b_ada, g_ffn1, ffn1_w_gate, ffn1_w_up, ffn1_w_down,
           g_mix, w_in, q_norm, k_norm, w_gate, w_branch_a, w_branch_b, w_out,
           g_ffn2, ffn2_w_gate, ffn2_w_up, ffn2_w_down):
    t, d = x.shape
    ada = _ada(c, w_ada, b_ada).reshape(N_ADA, d)
    slopes = jnp.exp2(-8.0 * jnp.arange(1, N_HEADS + 1, dtype=F32) / N_HEADS)

    x = _ffn(x, ada, g_ffn1, ffn1_w_gate.astype(BF16), ffn1_w_up.astype(BF16), ffn1_w_down.astype(BF16), sub=0)

    scale = HEAD_DIM ** -0.5
    ones_a = jnp.ones((N_HEADS_A, HEAD_DIM), F32)
    ones_b = jnp.ones((N_HEADS_B, HEAD_DIM), F32)
    gains = jnp.concatenate([q_norm[:N_HEADS_A] * scale, k_norm[:N_HEADS_A], ones_a,
                             q_norm[N_HEADS_A:] * scale, k_norm[N_HEADS_A:], ones_b], axis=0)
    hpb = 4
    na, nbk = N_HEADS_A // hpb, N_HEADS_B // hpb
    norm_blocks = ((0, 2 * na), (3 * na, 3 * na + 2 * nbk))
    qkvh = _proj_heads(x, ada, g_mix, w_in.astype(BF16), gains, norm_blocks, heads_per_blk=hpb)
    gates = _proj_gate(x, ada, g_mix, w_gate.astype(BF16))

    y_a = _dilated(qkvh, slopes)
    y_b = _moba(qkvh, slopes)

    x = _merge(x, ada, y_a, y_b, gates, w_branch_a.astype(BF16), w_branch_b.astype(BF16), w_out.astype(BF16))
    x = _ffn(x, ada, g_ffn2, ffn2_w_gate.astype(BF16), ffn2_w_up.astype(BF16), ffn2_w_down.astype(BF16), sub=2)
    return x


def kernel(x, c, w_ada, b_ada, g_ffn1, ffn1_w_gate, ffn1_w_up, ffn1_w_down, g_mix, w_in, q_norm, k_norm,
           w_gate, w_branch_a, w_branch_b, w_out, g_ffn2, ffn2_w_gate, ffn2_w_up, ffn2_w_down):
    batch, depth = x.shape[0], w_ada.shape[0]
    outs = []
    for bi in range(batch):
        xb = x[bi]
        for l in range(depth):
            xb = _layer(xb, c[bi], w_ada[l], b_ada[l], g_ffn1[l], ffn1_w_gate[l], ffn1_w_up[l],
                        ffn1_w_down[l], g_mix[l], w_in[l], q_norm[l], k_norm[l], w_gate[l],
                        w_branch_a[l], w_branch_b[l], w_out[l], g_ffn2[l], ffn2_w_gate[l],
                        ffn2_w_up[l], ffn2_w_down[l])
        outs.append(xb)
    return jnp.stack(outs, axis=0)
```

```python
import functools

import jax
import jax.numpy as jnp
from jax import lax
from jax.experimental import pallas as pl
from jax.experimental.pallas import tpu as pltpu

HEAD_DIM = 128
DIL_PATTERNS = ((128, 1), (512, 4), (2048, 16))
N_DIL = len(DIL_PATTERNS)
HEADS_PER_DIL_GROUP = 4
N_HEADS_A = HEADS_PER_DIL_GROUP * N_DIL
N_HEADS_B = 8
N_HEADS = N_HEADS_A + N_HEADS_B
BAND = 128
MOBA_BLOCK = 256
MOBA_TOPK = 3
N_ADA = 9
EPS = 1e-6
NEG_INF = -1e30
LOG2_E = 1.4426950408889634

LANES = 128
VMEM_LIMIT = 56 * 1024 * 1024

BF16 = jnp.bfloat16
F32 = jnp.float32

_NT = (((1,), (1,)), ((), ()))


def _params(*sem):
    return pltpu.CompilerParams(dimension_semantics=sem, vmem_limit_bytes=VMEM_LIMIT)


def _sigmoid(x):
    return 1.0 / (1.0 + jnp.exp(-x))


def _norm_modulate(x, g, shift, scale):
    ms = jnp.mean(x * x, axis=-1, keepdims=True)
    y = (x * lax.rsqrt(ms + EPS)) * g
    return y * (1.0 + scale) + shift


def _head_rmsnorm(a, gain):
    ms = jnp.mean(a * a, axis=-1, keepdims=True)
    return (a * lax.rsqrt(ms + EPS)) * gain


def _ada_kernel(c_ref, w_ref, b_ref, o_ref):
    c = c_ref[...]
    s = c * _sigmoid(c)
    o_ref[...] = jnp.sum(s * w_ref[...], axis=0, keepdims=True) + b_ref[...]


def _ada(c, w_ada, b_ada, *, tn=1024):
    d, n = w_ada.shape
    return pl.pallas_call(
        _ada_kernel,
        grid=(n // tn,),
        in_specs=[pl.BlockSpec((d, 1), lambda j: (0, 0)),
                  pl.BlockSpec((d, tn), lambda j: (0, j)),
                  pl.BlockSpec((1, tn), lambda j: (0, j))],
        out_specs=pl.BlockSpec((1, tn), lambda j: (0, j)),
        out_shape=jax.ShapeDtypeStruct((1, n), F32),
        compiler_params=_params("arbitrary"),
        name="ada",
    )(c.reshape(d, 1), w_ada, b_ada.reshape(1, n))


def _ffn_kernel(x_ref, ada_ref, g_ref, wg_ref, wu_ref, wd_ref, o_ref, u_ref, acc_ref, *, sub):
    f = pl.program_id(1)

    @pl.when(f == 0)
    def _():
        u = _norm_modulate(x_ref[...], g_ref[...],
                           ada_ref[3 * sub:3 * sub + 1, :], ada_ref[3 * sub + 1:3 * sub + 2, :])
        u_ref[...] = u.astype(BF16)
        acc_ref[...] = jnp.zeros_like(acc_ref)

    u = u_ref[...]
    hg = jnp.dot(u, wg_ref[...], preferred_element_type=F32)
    hu = jnp.dot(u, wu_ref[...], preferred_element_type=F32)
    h = (hg * _sigmoid(hg)) * hu
    acc_ref[...] += jnp.dot(h.astype(BF16), wd_ref[...], preferred_element_type=F32)

    @pl.when(f == pl.num_programs(1) - 1)
    def _():
        gate = ada_ref[3 * sub + 2:3 * sub + 3, :]
        o_ref[...] = x_ref[...] + (0.5 * gate) * acc_ref[...]


def _ffn(x, ada, g, w_gate, w_up, w_down, *, sub, tm=512, tf=512):
    t, d = x.shape
    dff = w_gate.shape[1]
    return pl.pallas_call(
        functools.partial(_ffn_kernel, sub=sub),
        grid=(t // tm, dff // tf),
        in_specs=[pl.BlockSpec((tm, d), lambda i, f: (i, 0)),
                  pl.BlockSpec((N_ADA, d), lambda i, f: (0, 0)),
                  pl.BlockSpec((1, d), lambda i, f: (0, 0)),
                  pl.BlockSpec((d, tf), lambda i, f: (0, f)),
                  pl.BlockSpec((d, tf), lambda i, f: (0, f)),
                  pl.BlockSpec((tf, d), lambda i, f: (f, 0))],
        out_specs=pl.BlockSpec((tm, d), lambda i, f: (i, 0)),
        out_shape=jax.ShapeDtypeStruct((t, d), F32),
        scratch_shapes=[pltpu.VMEM((tm, d), BF16), pltpu.VMEM((tm, d), F32)],
        compiler_params=_params("parallel", "arbitrary"),
        name=f"ffn{sub}",
    )(x, ada, g.reshape(1, d), w_gate, w_up, w_down)


HPB = HEADS_PER_DIL_GROUP
PROJ_TN = HPB * HEAD_DIM
BLK_A_END = 3 * N_DIL
BLK_BQK0 = BLK_A_END
BLK_BV0 = BLK_BQK0 + 2 * (N_HEADS_B // HPB)
BLK_GATE0 = BLK_BV0 + N_HEADS_B // HPB


def _proj_kernel(x_ref, ada_ref, g_ref, w_ref, gain_ref,
                 a0_ref, a1_ref, a2_ref, bqk_ref, bvt_ref, gate_ref, u_ref, stg_ref):
    j = pl.program_id(1)
    tm = x_ref.shape[0]

    @pl.when(j == 0)
    def _():
        u = _norm_modulate(x_ref[...], g_ref[...], ada_ref[3:4, :], ada_ref[4:5, :])
        u_ref[...] = u.astype(BF16)

    res = jnp.dot(u_ref[...], w_ref[...], preferred_element_type=F32)

    def head(h, normed):
        a = res[:, h * HEAD_DIM:(h + 1) * HEAD_DIM]
        return _head_rmsnorm(a, gain_ref[0, h:h + 1, :]) if normed else a

    def store_dilated(o_ref, r, normed):
        for h in range(HPB):
            y = head(h, normed)
            if r == 1:
                o_ref[h] = y.astype(BF16)
            else:
                stg_ref[h] = y
                for rho in range(r):
                    part = stg_ref.at[h][pl.ds(rho, tm // r, stride=r), :]
                    o_ref[h, :, rho * HEAD_DIM:(rho + 1) * HEAD_DIM] = part.astype(BF16)

    a_refs = (a0_ref, a1_ref, a2_ref)
    for g, (_, r) in enumerate(DIL_PATTERNS):
        in_group = (j < BLK_A_END) & (lax.rem(j, N_DIL) == g)
        is_v = j >= 2 * N_DIL
        pl.when(in_group & jnp.logical_not(is_v))(functools.partial(store_dilated, a_refs[g], r, True))
        pl.when(in_group & is_v)(functools.partial(store_dilated, a_refs[g], r, False))

    @pl.when((j >= BLK_BQK0) & (j < BLK_BV0))
    def _():
        for h in range(HPB):
            bqk_ref[h] = head(h, True).astype(BF16)

    @pl.when((j >= BLK_BV0) & (j < BLK_GATE0))
    def _():
        for h in range(HPB):
            y = head(h, False)
            for bb in range(tm // MOBA_BLOCK):
                bvt_ref[h, bb] = y[bb * MOBA_BLOCK:(bb + 1) * MOBA_BLOCK, :].T.astype(BF16)

    @pl.when(j >= BLK_GATE0)
    def _():
        gate_ref[...] = _sigmoid(res).astype(BF16)


def _proj(x, ada, g, w_all, gains, *, tm=1024):
    t, d = x.shape
    n_blk = w_all.shape[1] // PROJ_TN
    n_gate = n_blk - BLK_GATE0

    def a_spec(gi, r):
        return pl.BlockSpec((HPB, tm // r, r * HEAD_DIM),
                            lambda i, j: (jnp.clip((j - gi + N_DIL - 1) // N_DIL, 0, 2), i, 0))

    a_shapes = [jax.ShapeDtypeStruct((3 * HPB, t // r, r * HEAD_DIM), BF16) for _, r in DIL_PATTERNS]
    out_shape = a_shapes + [
        jax.ShapeDtypeStruct((2 * N_HEADS_B, t, HEAD_DIM), BF16),
        jax.ShapeDtypeStruct((N_HEADS_B, t // MOBA_BLOCK, HEAD_DIM, MOBA_BLOCK), BF16),
        jax.ShapeDtypeStruct((t, n_gate * PROJ_TN), BF16)]
    out_specs = [a_spec(gi, r) for gi, (_, r) in enumerate(DIL_PATTERNS)] + [
        pl.BlockSpec((HPB, tm, HEAD_DIM),
                     lambda i, j: (jnp.clip(j - BLK_BQK0, 0, BLK_BV0 - BLK_BQK0 - 1), i, 0)),
        pl.BlockSpec((HPB, tm // MOBA_BLOCK, HEAD_DIM, MOBA_BLOCK),
                     lambda i, j: (jnp.clip(j - BLK_BV0, 0, BLK_GATE0 - BLK_BV0 - 1), i, 0, 0)),
        pl.BlockSpec((tm, PROJ_TN), lambda i, j: (i, jnp.clip(j - BLK_GATE0, 0, n_gate - 1)))]
    return pl.pallas_call(
        _proj_kernel,
        grid=(t // tm, n_blk),
        in_specs=[pl.BlockSpec((tm, d), lambda i, j: (i, 0)),
                  pl.BlockSpec((N_ADA, d), lambda i, j: (0, 0)),
                  pl.BlockSpec((1, d), lambda i, j: (0, 0)),
                  pl.BlockSpec((d, PROJ_TN), lambda i, j: (0, j)),
                  pl.BlockSpec((1, HPB, HEAD_DIM), lambda i, j: (j, 0, 0))],
        out_specs=out_specs,
        out_shape=out_shape,
        scratch_shapes=[pltpu.VMEM((tm, d), BF16), pltpu.VMEM((HPB, tm, HEAD_DIM), F32)],
        compiler_params=_params("arbitrary", "arbitrary"),
        name="proj",
    )(x, ada, g.reshape(1, d), w_all, gains.reshape(n_blk, HPB, HEAD_DIM))


DIL_TOKENS = 2048


def _band_block(q, k2, v2, bias):
    s = lax.dot_general(q, k2, _NT, preferred_element_type=F32) + bias
    m = jnp.max(s, axis=-1, keepdims=True)
    p = jnp.exp2(s - m)
    denom = jnp.sum(p, axis=-1, keepdims=True)
    o = jnp.dot(p.astype(BF16), v2, preferred_element_type=F32) / denom
    lse = jnp.broadcast_to(m + jnp.log2(denom), (BAND, LANES))
    return o, lse


def _dilated_kernel(slopes_ref,
                    q0, k0, v0, kp0, vp0,
                    q1, k1, v1, kp1, vp1,
                    q2, k2, v2, kp2, vp2,
                    y_ref, o_scr, l_scr):
    j = pl.program_id(0)
    b = pl.program_id(1)

    qi = lax.broadcasted_iota(jnp.int32, (BAND, 2 * BAND), 0)
    ki = lax.broadcasted_iota(jnp.int32, (BAND, 2 * BAND), 1)
    dist = BAND + qi - ki
    in_band = (dist >= 0) & (dist <= BAND)
    dist_f = dist.astype(F32)
    first_ok = (ki + jnp.minimum(b, 1) * BAND) >= BAND

    groups = ((q0, k0, v0, kp0, vp0), (q1, k1, v1, kp1, vp1), (q2, k2, v2, kp2, vp2))
    for g, (_, r) in enumerate(DIL_PATTERNS):
        q_ref, k_ref, v_ref, kp_ref, vp_ref = groups[g]
        slope = slopes_ref[g * HEADS_PER_DIL_GROUP + j]
        bias = jnp.where(in_band, dist_f * (-slope * r), NEG_INF)
        bias_first = jnp.where(first_ok, bias, NEG_INF)
        n_blk = DIL_TOKENS // (r * BAND)
        for rho in range(r):
            cols = slice(rho * HEAD_DIM, (rho + 1) * HEAD_DIM)

            def store(i, o, lse, g=g, r=r, rho=rho):
                rows = pl.ds(i * (BAND * r) + rho, BAND, stride=r) if r > 1 else pl.ds(i * BAND, BAND)
                o_scr.at[g][rows, :] = o
                l_scr.at[g][rows, :] = lse

            kk = jnp.concatenate([kp_ref[:, cols], k_ref[0:BAND, cols]], axis=0)
            vv = jnp.concatenate([vp_ref[:, cols], v_ref[0:BAND, cols]], axis=0)
            o, lse = _band_block(q_ref[0:BAND, cols], kk, vv, bias_first)
            store(0, o, lse)

            if n_blk > 1:
                def body(i, carry, q_ref=q_ref, k_ref=k_ref, v_ref=v_ref, cols=cols, bias=bias, store=store):
                    lo = pl.multiple_of(i * BAND - BAND, BAND)
                    qs = pl.multiple_of(i * BAND, BAND)
                    o, lse = _band_block(q_ref[pl.ds(qs, BAND), cols],
                                         k_ref[pl.ds(lo, 2 * BAND), cols],
                                         v_ref[pl.ds(lo, 2 * BAND), cols], bias)
                    store(i, o, lse)
                    return carry
                lax.fori_loop(1, n_blk, body, 0)

    l0, l1, l2 = l_scr[0], l_scr[1], l_scr[2]
    mx = jnp.maximum(jnp.maximum(l0, l1), l2)
    e0, e1, e2 = jnp.exp2(l0 - mx), jnp.exp2(l1 - mx), jnp.exp2(l2 - mx)
    tot = e0 + e1 + e2
    y = (e0 / tot) * o_scr[0] + (e1 / tot) * o_scr[1] + (e2 / tot) * o_scr[2]
    y_ref[...] = y.astype(BF16)


def _dilated(a_views, slopes):
    t = a_views[0].shape[1]
    dh = HEAD_DIM
    nb = t // DIL_TOKENS
    hq, hk, hv = 0, HPB, 2 * HPB
    args, specs = [], []
    for g, (_, r) in enumerate(DIL_PATTERNS):
        rows = DIL_TOKENS // r
        prev_per_blk = rows // BAND
        width = r * dh

        def cur(base, rows=rows, width=width):
            return pl.BlockSpec((None, rows, width), lambda j, b: (base + j, b, 0))

        def prev(base, width=width, ppb=prev_per_blk):
            return pl.BlockSpec((None, BAND, width),
                                lambda j, b: (base + j, jnp.maximum(b * ppb - 1, 0), 0))

        args += [a_views[g]] * 5
        specs += [cur(hq), cur(hk), cur(hv), prev(hk), prev(hv)]

    return pl.pallas_call(
        _dilated_kernel,
        grid=(HEADS_PER_DIL_GROUP, nb),
        in_specs=[pl.BlockSpec(memory_space=pltpu.SMEM)] + specs,
        out_specs=pl.BlockSpec((DIL_TOKENS, dh), lambda j, b: (b, j)),
        out_shape=jax.ShapeDtypeStruct((t, HEADS_PER_DIL_GROUP * dh), BF16),
        scratch_shapes=[pltpu.VMEM((N_DIL, DIL_TOKENS, dh), F32),
                        pltpu.VMEM((N_DIL, DIL_TOKENS, LANES), F32)],
        compiler_params=_params("arbitrary", "arbitrary"),
        name="dilated",
    )(slopes, *args)


MOBA_UNROLL = 4


def _moba_kernel(slopes_ref, q_ref, k_ref, vt_ref, y_ref, kmean_scr, sel_scr, acc_scr, *, n_blocks):
    h = pl.program_id(0)
    qb = pl.program_id(1)
    blk = MOBA_BLOCK
    slope = slopes_ref[N_HEADS_A + h]

    @pl.when(qb == 0)
    def _():
        def body(n, carry):
            kn = k_ref[pl.ds(pl.multiple_of(n * blk, blk), blk), :].astype(F32)
            kmean_scr[pl.ds(n, 1), :] = jnp.mean(kn, axis=0, keepdims=True)
            return carry
        lax.fori_loop(0, n_blocks, body, 0)

    q = q_ref[...]

    gate = lax.dot_general(kmean_scr[...], q.astype(F32), _NT,
                           precision=lax.Precision.HIGHEST, preferred_element_type=F32)
    blk_id = lax.broadcasted_iota(jnp.int32, (n_blocks, blk), 0)
    blk_id_f = blk_id.astype(F32)
    past = blk_id < qb
    work = jnp.where(past, gate, NEG_INF)
    sel = jnp.zeros((n_blocks, blk), F32)
    for _ in range(MOBA_TOPK):
        mx = jnp.max(work, axis=0, keepdims=True)
        first = jnp.min(jnp.where(work == mx, blk_id_f, float(n_blocks)), axis=0, keepdims=True)
        hit = blk_id_f == first
        sel = jnp.where(hit, 1.0, sel)
        work = jnp.where(hit, -jnp.inf, work)
    sel_scr[...] = jnp.where(past, sel, 0.0)

    key_i = lax.broadcasted_iota(jnp.int32, (blk, blk), 0)
    qry_i = lax.broadcasted_iota(jnp.int32, (blk, blk), 1)
    bias = slope * key_i.astype(F32)

    own = pl.multiple_of(qb * blk, blk)
    s = lax.dot_general(k_ref[pl.ds(own, blk), :], q, _NT, preferred_element_type=F32)
    s = jnp.where(key_i <= qry_i, s + bias, NEG_INF)
    m0 = jnp.max(s, axis=0, keepdims=True)
    p = jnp.exp2(s - m0)
    l0 = jnp.sum(p, axis=0, keepdims=True)
    acc_scr[...] = jnp.dot(vt_ref[qb], p.astype(BF16), preferred_element_type=F32)

    width = MOBA_UNROLL
    n_chunks = (qb + width - 1) // width

    def chunk_scores(ci):
        start = pl.multiple_of(ci * (width * blk), width * blk)
        return lax.dot_general(k_ref[pl.ds(start, width * blk), :], q, _NT, preferred_element_type=F32)

    def past_chunk(ci, carry):
        m_old, l_old, s_all = carry
        s_next = chunk_scores(jnp.minimum(ci + 1, n_chunks - 1))
        n0 = ci * width
        s, c, chosen = [], [], []
        m_chunk = jnp.full((1, blk), NEG_INF, F32)
        for a in range(width):
            s.append(s_all[a * blk:(a + 1) * blk] + bias)
            c.append(slope * jnp.full((1, blk), (n0 + a - qb) * blk, jnp.int32).astype(F32))
            chosen.append(sel_scr[pl.ds(n0 + a, 1), :] > 0.5)
            m_blk = jnp.max(s[a], axis=0, keepdims=True) + c[a]
            m_chunk = jnp.maximum(m_chunk, jnp.where(chosen[a], m_blk, NEG_INF))
        l_chunk = jnp.zeros((1, blk), F32)
        pv = jnp.zeros((vt_ref.shape[1], blk), F32)
        for a in range(width):
            p = jnp.exp2(s[a] - jnp.where(chosen[a], m_chunk - c[a], -NEG_INF))
            l_chunk = l_chunk + jnp.sum(p, axis=0, keepdims=True)
            pv = pv + jnp.dot(vt_ref[n0 + a], p.astype(BF16), preferred_element_type=F32)
        m_new = jnp.maximum(m_old, m_chunk)
        alpha = jnp.exp2(m_old - m_new)
        beta = jnp.exp2(m_chunk - m_new)
        acc_scr[...] = alpha * acc_scr[...] + beta * pv
        return m_new, alpha * l_old + beta * l_chunk, s_next

    _, l_fin, _ = lax.fori_loop(0, n_chunks, past_chunk, (m0, l0, chunk_scores(0)))
    y_ref[...] = (acc_scr[...] / l_fin).T.astype(BF16)


def _moba(bqk, bvt, slopes):
    _, t, dh = bqk.shape
    n_blocks = t // MOBA_BLOCK
    return pl.pallas_call(
        functools.partial(_moba_kernel, n_blocks=n_blocks),
        grid=(N_HEADS_B, n_blocks),
        in_specs=[pl.BlockSpec(memory_space=pltpu.SMEM),
                  pl.BlockSpec((None, MOBA_BLOCK, dh), lambda h, i: (h, i, 0)),
                  pl.BlockSpec((None, t, dh), lambda h, i: (N_HEADS_B + h, 0, 0)),
                  pl.BlockSpec((None, n_blocks, dh, MOBA_BLOCK), lambda h, i: (h, 0, 0, 0))],
        out_specs=pl.BlockSpec((MOBA_BLOCK, dh), lambda h, i: (i, h)),
        out_shape=jax.ShapeDtypeStruct((t, N_HEADS_B * dh), BF16),
        scratch_shapes=[pltpu.VMEM((n_blocks, dh), F32),
                        pltpu.VMEM((n_blocks, MOBA_BLOCK), F32),
                        pltpu.VMEM((dh, MOBA_BLOCK), F32)],
        compiler_params=_params("arbitrary", "arbitrary"),
        name="moba",
    )(slopes, bqk, bqk, bvt)


def _merge_kernel(x_ref, ada_ref, ya_ref, yb_ref, gates_a_ref, gates_b_ref, wa_ref, wb_ref, wo_ref, o_ref):
    pa = jnp.dot(ya_ref[...], wa_ref[...], preferred_element_type=F32)
    pb = jnp.dot(yb_ref[...], wb_ref[...], preferred_element_type=F32)
    merged = gates_a_ref[...].astype(F32) * pa + gates_b_ref[...].astype(F32) * pb
    out = jnp.dot(merged.astype(BF16), wo_ref[...], preferred_element_type=F32)
    o_ref[...] = x_ref[...] + ada_ref[5:6, :] * out


def _merge(x, ada, y_a, y_b, gates, w_a, w_b, w_o, *, tm=256):
    t, d = x.shape
    const = lambda i: (0, 0)
    return pl.pallas_call(
        _merge_kernel,
        grid=(t // tm,),
        in_specs=[pl.BlockSpec((tm, d), lambda i: (i, 0)),
                  pl.BlockSpec((N_ADA, d), const),
                  pl.BlockSpec((tm, y_a.shape[1]), lambda i: (i, 0)),
                  pl.BlockSpec((tm, y_b.shape[1]), lambda i: (i, 0)),
                  pl.BlockSpec((tm, d), lambda i: (i, 0)),
                  pl.BlockSpec((tm, d), lambda i: (i, 1)),
                  pl.BlockSpec(w_a.shape, const),
                  pl.BlockSpec(w_b.shape, const),
                  pl.BlockSpec(w_o.shape, const)],
        out_specs=pl.BlockSpec((tm, d), lambda i: (i, 0)),
        out_shape=jax.ShapeDtypeStruct((t, d), F32),
        compiler_params=_params("parallel"),
        name="merge",
    )(x, ada, y_a, y_b, gates, gates, w_a, w_b, w_o)


def _layer(x, c, w_ada, b_ada, g_ffn1, ffn1_w_gate, ffn1_w_up, ffn1_w_down,
           g_mix, w_in, q_norm, k_norm, w_gate, w_branch_a, w_branch_b, w_out,
           g_ffn2, ffn2_w_gate, ffn2_w_up, ffn2_w_down):
    t, d = x.shape
    ada = _ada(c, w_ada, b_ada).reshape(N_ADA, d)
    slopes = jnp.exp2(-8.0 * jnp.arange(1, N_HEADS + 1, dtype=F32) / N_HEADS) * LOG2_E

    x = _ffn(x, ada, g_ffn1, ffn1_w_gate.astype(BF16), ffn1_w_up.astype(BF16), ffn1_w_down.astype(BF16), sub=0)

    scale = HEAD_DIM ** -0.5 * LOG2_E
    n_gate_heads = w_gate.shape[1] // HEAD_DIM
    gains = jnp.concatenate([q_norm[:N_HEADS_A] * scale, k_norm[:N_HEADS_A], jnp.ones((N_HEADS_A, HEAD_DIM), F32),
                             q_norm[N_HEADS_A:] * scale, k_norm[N_HEADS_A:],
                             jnp.ones((N_HEADS_B + n_gate_heads, HEAD_DIM), F32)], axis=0)
    w_all = jnp.concatenate([w_in.astype(BF16), w_gate.astype(BF16)], axis=1)
    a0, a1, a2, bqk, bvt, gates = _proj(x, ada, g_mix, w_all, gains)

    y_a = _dilated((a0, a1, a2), slopes)
    y_b = _moba(bqk, bvt, slopes)

    x = _merge(x, ada, y_a, y_b, gates, w_branch_a.astype(BF16), w_branch_b.astype(BF16), w_out.astype(BF16))
    x = _ffn(x, ada, g_ffn2, ffn2_w_gate.astype(BF16), ffn2_w_up.astype(BF16), ffn2_w_down.astype(BF16), sub=2)
    return x


def kernel(x, c, w_ada, b_ada, g_ffn1, ffn1_w_gate, ffn1_w_up, ffn1_w_down, g_mix, w_in, q_norm, k_norm,
           w_gate, w_branch_a, w_branch_b, w_out, g_ffn2, ffn2_w_gate, ffn2_w_up, ffn2_w_down):
    batch, depth = x.shape[0], w_ada.shape[0]
    outs = []
    for bi in range(batch):
        xb = x[bi]
        for l in range(depth):
            xb = _layer(xb, c[bi], w_ada[l], b_ada[l], g_ffn1[l], ffn1_w_gate[l], ffn1_w_up[l],
                        ffn1_w_down[l], g_mix[l], w_in[l], q_norm[l], k_norm[l], w_gate[l],
                        w_branch_a[l], w_branch_b[l], w_out[l], g_ffn2[l], ffn2_w_gate[l],
                        ffn2_w_up[l], ffn2_w_down[l])
        outs.append(xb)
    return jnp.stack(outs, axis=0)
```

```python
import functools

import jax
import jax.numpy as jnp
from jax import lax
from jax.experimental import pallas as pl
from jax.experimental.pallas import tpu as pltpu

HEAD_DIM = 128
DIL_PATTERNS = ((128, 1), (512, 4), (2048, 16))
N_DIL = len(DIL_PATTERNS)
HEADS_PER_DIL_GROUP = 4
N_HEADS_A = HEADS_PER_DIL_GROUP * N_DIL
N_HEADS_B = 8
N_HEADS = N_HEADS_A + N_HEADS_B
BAND = 128
MOBA_BLOCK = 256
MOBA_TOPK = 3
N_ADA = 9
EPS = 1e-6
NEG_INF = -1e30
LOG2_E = 1.4426950408889634

LANES = 128
VMEM_LIMIT = 56 * 1024 * 1024

BF16 = jnp.bfloat16
F32 = jnp.float32

_NT = (((1,), (1,)), ((), ()))


def _params(*sem):
    return pltpu.CompilerParams(dimension_semantics=sem, vmem_limit_bytes=VMEM_LIMIT)


def _sigmoid(x):
    return 1.0 / (1.0 + jnp.exp(-x))


def _norm_modulate(x, g, shift, scale):
    ms = jnp.mean(x * x, axis=-1, keepdims=True)
    y = (x * lax.rsqrt(ms + EPS)) * g
    return y * (1.0 + scale) + shift


def _head_rmsnorm(a, gain):
    ms = jnp.mean(a * a, axis=-1, keepdims=True)
    return (a * lax.rsqrt(ms + EPS)) * gain


def _ada_kernel(c_ref, w_ref, b_ref, o_ref):
    c = c_ref[...]
    s = c * _sigmoid(c)
    o_ref[...] = jnp.sum(s * w_ref[...], axis=0, keepdims=True) + b_ref[...]


def _ada(c, w_ada, b_ada, *, tn=1024):
    d, n = w_ada.shape
    return pl.pallas_call(
        _ada_kernel,
        grid=(n // tn,),
        in_specs=[pl.BlockSpec((d, 1), lambda j: (0, 0)),
                  pl.BlockSpec((d, tn), lambda j: (0, j)),
                  pl.BlockSpec((1, tn), lambda j: (0, j))],
        out_specs=pl.BlockSpec((1, tn), lambda j: (0, j)),
        out_shape=jax.ShapeDtypeStruct((1, n), F32),
        compiler_params=_params("arbitrary"),
        name="ada",
    )(c.reshape(d, 1), w_ada, b_ada.reshape(1, n))


def _ffn_kernel(x_ref, ada_ref, g_ref, wg_ref, wu_ref, wd_ref, o_ref, u_ref, acc_ref, *, sub):
    f = pl.program_id(1)

    @pl.when(f == 0)
    def _():
        u = _norm_modulate(x_ref[...], g_ref[...],
                           ada_ref[3 * sub:3 * sub + 1, :], ada_ref[3 * sub + 1:3 * sub + 2, :])
        u_ref[...] = u.astype(BF16)
        acc_ref[...] = jnp.zeros_like(acc_ref)

    u = u_ref[...]
    hg = jnp.dot(u, wg_ref[...], preferred_element_type=F32)
    hu = jnp.dot(u, wu_ref[...], preferred_element_type=F32)
    h = (hg * _sigmoid(hg)) * hu
    acc_ref[...] += jnp.dot(h.astype(BF16), wd_ref[...], preferred_element_type=F32)

    @pl.when(f == pl.num_programs(1) - 1)
    def _():
        gate = ada_ref[3 * sub + 2:3 * sub + 3, :]
        o_ref[...] = x_ref[...] + (0.5 * gate) * acc_ref[...]


def _ffn(x, ada, g, w_gate, w_up, w_down, *, sub, tm=512, tf=512):
    t, d = x.shape
    dff = w_gate.shape[1]
    return pl.pallas_call(
        functools.partial(_ffn_kernel, sub=sub),
        grid=(t // tm, dff // tf),
        in_specs=[pl.BlockSpec((tm, d), lambda i, f: (i, 0)),
                  pl.BlockSpec((N_ADA, d), lambda i, f: (0, 0)),
                  pl.BlockSpec((1, d), lambda i, f: (0, 0)),
                  pl.BlockSpec((d, tf), lambda i, f: (0, f)),
                  pl.BlockSpec((d, tf), lambda i, f: (0, f)),
                  pl.BlockSpec((tf, d), lambda i, f: (f, 0))],
        out_specs=pl.BlockSpec((tm, d), lambda i, f: (i, 0)),
        out_shape=jax.ShapeDtypeStruct((t, d), F32),
        scratch_shapes=[pltpu.VMEM((tm, d), BF16), pltpu.VMEM((tm, d), F32)],
        compiler_params=_params("parallel", "arbitrary"),
        name=f"ffn{sub}",
    )(x, ada, g.reshape(1, d), w_gate, w_up, w_down)


HPB = HEADS_PER_DIL_GROUP
PROJ_TN = HPB * HEAD_DIM
BLK_A_END = 3 * N_DIL
BLK_BQK0 = BLK_A_END
BLK_BV0 = BLK_BQK0 + 2 * (N_HEADS_B // HPB)
BLK_GATE0 = BLK_BV0 + N_HEADS_B // HPB


MOBA_VT_PAD = 16
MOBA_VT_ROWS = HEAD_DIM + MOBA_VT_PAD


def _proj_kernel(x_ref, ada_ref, g_ref, w_ref, gain_ref, kw_ref,
                 a0_ref, a1_ref, a2_ref, bqk_ref, bvt_ref, gate_ref, u_ref, stg_ref):
    j = pl.program_id(1)
    tm = x_ref.shape[0]

    @pl.when(j == 0)
    def _():
        u = _norm_modulate(x_ref[...], g_ref[...], ada_ref[3:4, :], ada_ref[4:5, :])
        u_ref[...] = u.astype(BF16)

    res = jnp.dot(u_ref[...], w_ref[...], preferred_element_type=F32)

    def head(h, normed):
        a = res[:, h * HEAD_DIM:(h + 1) * HEAD_DIM]
        return _head_rmsnorm(a, gain_ref[0, h:h + 1, :]) if normed else a

    def store_dilated(o_ref, r, normed):
        for h in range(HPB):
            y = head(h, normed)
            if r == 1:
                o_ref[h] = y.astype(BF16)
            else:
                stg_ref[h] = y
                for rho in range(r):
                    part = stg_ref.at[h][pl.ds(rho, tm // r, stride=r), :]
                    o_ref[h, :, rho * HEAD_DIM:(rho + 1) * HEAD_DIM] = part.astype(BF16)

    a_refs = (a0_ref, a1_ref, a2_ref)
    for g, (_, r) in enumerate(DIL_PATTERNS):
        in_group = (j < BLK_A_END) & (lax.rem(j, N_DIL) == g)
        is_v = j >= 2 * N_DIL
        pl.when(in_group & jnp.logical_not(is_v))(functools.partial(store_dilated, a_refs[g], r, True))
        pl.when(in_group & is_v)(functools.partial(store_dilated, a_refs[g], r, False))

    @pl.when((j >= BLK_BQK0) & (j < BLK_BV0))
    def _():
        for h in range(HPB):
            bqk_ref[h] = head(h, True).astype(BF16)

    @pl.when((j >= BLK_BV0) & (j < BLK_GATE0))
    def _():
        for h in range(HPB):
            y = head(h, False)
            kw = kw_ref[0, h]
            for bb in range(tm // MOBA_BLOCK):
                yt = y[bb * MOBA_BLOCK:(bb + 1) * MOBA_BLOCK, :].T
                bvt_ref[h, bb, 0:HEAD_DIM, :] = (yt * kw[0:1, :]).astype(BF16)
                bvt_ref[h, bb, HEAD_DIM:, :] = kw.astype(BF16)

    @pl.when(j >= BLK_GATE0)
    def _():
        gate_ref[...] = _sigmoid(res).astype(BF16)


def _proj(x, ada, g, w_all, gains, key_weights, *, tm=1024):
    t, d = x.shape
    n_blk = w_all.shape[1] // PROJ_TN
    n_gate = n_blk - BLK_GATE0
    n_bv = BLK_GATE0 - BLK_BV0

    def a_spec(gi, r):
        return pl.BlockSpec((HPB, tm // r, r * HEAD_DIM),
                            lambda i, j: (jnp.clip((j - gi + N_DIL - 1) // N_DIL, 0, 2), i, 0))

    a_shapes = [jax.ShapeDtypeStruct((3 * HPB, t // r, r * HEAD_DIM), BF16) for _, r in DIL_PATTERNS]
    out_shape = a_shapes + [
        jax.ShapeDtypeStruct((2 * N_HEADS_B, t, HEAD_DIM), BF16),
        jax.ShapeDtypeStruct((N_HEADS_B, t // MOBA_BLOCK, MOBA_VT_ROWS, MOBA_BLOCK), BF16),
        jax.ShapeDtypeStruct((t, n_gate * PROJ_TN), BF16)]
    out_specs = [a_spec(gi, r) for gi, (_, r) in enumerate(DIL_PATTERNS)] + [
        pl.BlockSpec((HPB, tm, HEAD_DIM),
                     lambda i, j: (jnp.clip(j - BLK_BQK0, 0, BLK_BV0 - BLK_BQK0 - 1), i, 0)),
        pl.BlockSpec((HPB, tm // MOBA_BLOCK, MOBA_VT_ROWS, MOBA_BLOCK),
                     lambda i, j: (jnp.clip(j - BLK_BV0, 0, n_bv - 1), i, 0, 0)),
        pl.BlockSpec((tm, PROJ_TN), lambda i, j: (i, jnp.clip(j - BLK_GATE0, 0, n_gate - 1)))]
    return pl.pallas_call(
        _proj_kernel,
        grid=(t // tm, n_blk),
        in_specs=[pl.BlockSpec((tm, d), lambda i, j: (i, 0)),
                  pl.BlockSpec((N_ADA, d), lambda i, j: (0, 0)),
                  pl.BlockSpec((1, d), lambda i, j: (0, 0)),
                  pl.BlockSpec((d, PROJ_TN), lambda i, j: (0, j)),
                  pl.BlockSpec((1, HPB, HEAD_DIM), lambda i, j: (j, 0, 0)),
                  pl.BlockSpec((1, HPB, MOBA_VT_PAD, MOBA_BLOCK),
                               lambda i, j: (jnp.clip(j - BLK_BV0, 0, n_bv - 1), 0, 0, 0))],
        out_specs=out_specs,
        out_shape=out_shape,
        scratch_shapes=[pltpu.VMEM((tm, d), BF16), pltpu.VMEM((HPB, tm, HEAD_DIM), F32)],
        compiler_params=_params("arbitrary", "arbitrary"),
        name="proj",
    )(x, ada, g.reshape(1, d), w_all, gains.reshape(n_blk, HPB, HEAD_DIM),
      key_weights.reshape(n_bv, HPB, MOBA_VT_PAD, MOBA_BLOCK))


DIL_TOKENS = 2048


def _band_block(q, k2, v2, bias):
    s = lax.dot_general(q, k2, _NT, preferred_element_type=F32) + bias
    m = jnp.max(s, axis=-1, keepdims=True)
    p = jnp.exp2(s - m)
    denom = jnp.sum(p, axis=-1, keepdims=True)
    o = jnp.dot(p.astype(BF16), v2, preferred_element_type=F32) / denom
    lse = jnp.broadcast_to(m + jnp.log2(denom), (BAND, LANES))
    return o, lse


def _dilated_kernel(slopes_ref,
                    q0, k0, v0, kp0, vp0,
                    q1, k1, v1, kp1, vp1,
                    q2, k2, v2, kp2, vp2,
                    y_ref, o_scr, l_scr):
    j = pl.program_id(0)
    b = pl.program_id(1)

    qi = lax.broadcasted_iota(jnp.int32, (BAND, 2 * BAND), 0)
    ki = lax.broadcasted_iota(jnp.int32, (BAND, 2 * BAND), 1)
    dist = BAND + qi - ki
    in_band = (dist >= 0) & (dist <= BAND)
    dist_f = dist.astype(F32)
    first_ok = (ki + jnp.minimum(b, 1) * BAND) >= BAND

    groups = ((q0, k0, v0, kp0, vp0), (q1, k1, v1, kp1, vp1), (q2, k2, v2, kp2, vp2))
    for g, (_, r) in enumerate(DIL_PATTERNS):
        q_ref, k_ref, v_ref, kp_ref, vp_ref = groups[g]
        slope = slopes_ref[g * HEADS_PER_DIL_GROUP + j]
        bias = jnp.where(in_band, dist_f * (-slope * r), NEG_INF)
        bias_first = jnp.where(first_ok, bias, NEG_INF)
        n_blk = DIL_TOKENS // (r * BAND)
        for rho in range(r):
            cols = slice(rho * HEAD_DIM, (rho + 1) * HEAD_DIM)

            def store(i, o, lse, g=g, r=r, rho=rho):
                rows = pl.ds(i * (BAND * r) + rho, BAND, stride=r) if r > 1 else pl.ds(i * BAND, BAND)
                o_scr.at[g][rows, :] = o
                l_scr.at[g][rows, :] = lse

            kk = jnp.concatenate([kp_ref[:, cols], k_ref[0:BAND, cols]], axis=0)
            vv = jnp.concatenate([vp_ref[:, cols], v_ref[0:BAND, cols]], axis=0)
            o, lse = _band_block(q_ref[0:BAND, cols], kk, vv, bias_first)
            store(0, o, lse)

            if n_blk > 1:
                def body(i, carry, q_ref=q_ref, k_ref=k_ref, v_ref=v_ref, cols=cols, bias=bias, store=store):
                    lo = pl.multiple_of(i * BAND - BAND, BAND)
                    qs = pl.multiple_of(i * BAND, BAND)
                    o, lse = _band_block(q_ref[pl.ds(qs, BAND), cols],
                                         k_ref[pl.ds(lo, 2 * BAND), cols],
                                         v_ref[pl.ds(lo, 2 * BAND), cols], bias)
                    store(i, o, lse)
                    return carry
                lax.fori_loop(1, n_blk, body, 0)

    l0, l1, l2 = l_scr[0], l_scr[1], l_scr[2]
    mx = jnp.maximum(jnp.maximum(l0, l1), l2)
    e0, e1, e2 = jnp.exp2(l0 - mx), jnp.exp2(l1 - mx), jnp.exp2(l2 - mx)
    tot = e0 + e1 + e2
    y = (e0 / tot) * o_scr[0] + (e1 / tot) * o_scr[1] + (e2 / tot) * o_scr[2]
    y_ref[...] = y.astype(BF16)


def _dilated(a_views, slopes):
    t = a_views[0].shape[1]
    dh = HEAD_DIM
    nb = t // DIL_TOKENS
    hq, hk, hv = 0, HPB, 2 * HPB
    args, specs = [], []
    for g, (_, r) in enumerate(DIL_PATTERNS):
        rows = DIL_TOKENS // r
        prev_per_blk = rows // BAND
        width = r * dh

        def cur(base, rows=rows, width=width):
            return pl.BlockSpec((None, rows, width), lambda j, b: (base + j, b, 0))

        def prev(base, width=width, ppb=prev_per_blk):
            return pl.BlockSpec((None, BAND, width),
                                lambda j, b: (base + j, jnp.maximum(b * ppb - 1, 0), 0))

        args += [a_views[g]] * 5
        specs += [cur(hq), cur(hk), cur(hv), prev(hk), prev(hv)]

    return pl.pallas_call(
        _dilated_kernel,
        grid=(HEADS_PER_DIL_GROUP, nb),
        in_specs=[pl.BlockSpec(memory_space=pltpu.SMEM)] + specs,
        out_specs=pl.BlockSpec((DIL_TOKENS, dh), lambda j, b: (b, j)),
        out_shape=jax.ShapeDtypeStruct((t, HEADS_PER_DIL_GROUP * dh), BF16),
        scratch_shapes=[pltpu.VMEM((N_DIL, DIL_TOKENS, dh), F32),
                        pltpu.VMEM((N_DIL, DIL_TOKENS, LANES), F32)],
        compiler_params=_params("arbitrary", "arbitrary"),
        name="dilated",
    )(slopes, *args)


MOBA_UNROLL = 4


MOBA_HEADS_PER_STEP = 4


def _moba_kernel(slopes_ref, q_ref, k_ref, vt_ref, y_ref, kmean_scr, sel_scr, acc_scr, s_scr, *, n_blocks):
    hg = pl.program_id(0)
    qb = pl.program_id(1)
    blk = MOBA_BLOCK
    n_h, _, dh = q_ref.shape
    heads = range(n_h)
    slope = [slopes_ref[N_HEADS_A + hg * n_h + hh] for hh in heads]

    @pl.when(qb == 0)
    def _():
        def body(n, carry):
            for hh in heads:
                kn = k_ref[hh, pl.ds(pl.multiple_of(n * blk, blk), blk), :].astype(F32)
                kmean_scr[hh, pl.ds(n, 1), :] = jnp.mean(kn, axis=0, keepdims=True)
            return carry
        lax.fori_loop(0, n_blocks, body, 0)

    q = [q_ref[hh] for hh in heads]

    blk_id = lax.broadcasted_iota(jnp.int32, (n_blocks, blk), 0)
    blk_id_f = blk_id.astype(F32)
    past = blk_id < qb
    key_i = lax.broadcasted_iota(jnp.int32, (blk, blk), 0)
    qry_i = lax.broadcasted_iota(jnp.int32, (blk, blk), 1)
    own = pl.multiple_of(qb * blk, blk)

    in_block_max = [slope[hh] * (blk - 1) for hh in heads]

    m0 = []
    for hh in heads:
        gate = lax.dot_general(kmean_scr[hh], q[hh].astype(F32), _NT,
                               precision=lax.Precision.HIGHEST, preferred_element_type=F32)
        work = jnp.where(past, gate, NEG_INF)
        sel = jnp.zeros((n_blocks, blk), F32)
        for _ in range(MOBA_TOPK):
            mx = jnp.max(work, axis=0, keepdims=True)
            first = jnp.min(jnp.where(work == mx, blk_id_f, float(n_blocks)), axis=0, keepdims=True)
            hit = blk_id_f == first
            sel = jnp.where(hit, 1.0, sel)
            work = jnp.where(hit, -jnp.inf, work)
        sel_scr[hh] = jnp.where(past, sel, 0.0)

        s = lax.dot_general(k_ref[hh, pl.ds(own, blk), :], q[hh], _NT, preferred_element_type=F32)
        s = jnp.where(key_i <= qry_i, s, NEG_INF)
        m0.append(jnp.max(s, axis=0, keepdims=True) + in_block_max[hh])
        p = jnp.exp2(s - m0[hh])
        acc_scr[hh] = jnp.dot(vt_ref[hh, qb], p.astype(BF16), preferred_element_type=F32)

    width = MOBA_UNROLL
    n_chunks = (qb + width - 1) // width

    def chunk_scores(hh, ci, slot):
        start = pl.multiple_of(ci * (width * blk), width * blk)
        s_scr[slot, hh] = lax.dot_general(k_ref[hh, pl.ds(start, width * blk), :], q[hh], _NT,
                                          preferred_element_type=F32)

    def past_chunk(ci, m_run, src, dst):
        n0 = ci * width
        nxt = jnp.minimum(ci + 1, n_chunks - 1)
        out = []
        for hh in heads:
            chunk_scores(hh, nxt, dst)
            c, chosen = [], []
            m_chunk = jnp.full((1, blk), NEG_INF, F32)
            for a in range(width):
                s = s_scr[src, hh, a * blk:(a + 1) * blk, :]
                c.append(slope[hh] * jnp.full((1, blk), (n0 + a - qb) * blk, jnp.int32).astype(F32))
                chosen.append(sel_scr[hh, pl.ds(n0 + a, 1), :] > 0.5)
                m_blk = jnp.max(s, axis=0, keepdims=True) + (c[a] + in_block_max[hh])
                m_chunk = jnp.maximum(m_chunk, jnp.where(chosen[a], m_blk, NEG_INF))
            pv = jnp.zeros(acc_scr.shape[1:], F32)
            for a in range(width):
                s = s_scr[src, hh, a * blk:(a + 1) * blk, :]
                p = jnp.exp2(s - jnp.where(chosen[a], m_chunk - c[a], -NEG_INF))
                pv = pv + jnp.dot(vt_ref[hh, n0 + a], p.astype(BF16), preferred_element_type=F32)
            m_new = jnp.maximum(m_run[hh], m_chunk)
            alpha = jnp.exp2(m_run[hh] - m_new)
            beta = jnp.exp2(m_chunk - m_new)
            acc_scr[hh] = alpha * acc_scr[hh] + beta * pv
            out.append(m_new)
        return tuple(out)

    for hh in heads:
        chunk_scores(hh, 0, 0)

    def body(ci, m_run):
        return lax.cond(lax.rem(ci, 2) == 0,
                        lambda m: past_chunk(ci, m, 0, 1), lambda m: past_chunk(ci, m, 1, 0), m_run)

    lax.fori_loop(0, n_chunks, body, tuple(m0))
    for hh in heads:
        y = acc_scr[hh, 0:dh, :] / acc_scr[hh, dh:dh + 1, :]
        y_ref[:, hh * dh:(hh + 1) * dh] = y.T.astype(BF16)


def _moba(bqk, bvt, slopes):
    _, t, dh = bqk.shape
    n_blocks = t // MOBA_BLOCK
    assert n_blocks % MOBA_UNROLL == 0, "the padded last chunk must stay inside the key array"
    n_h = MOBA_HEADS_PER_STEP
    n_groups = N_HEADS_B // n_h
    return pl.pallas_call(
        functools.partial(_moba_kernel, n_blocks=n_blocks),
        grid=(n_groups, n_blocks),
        in_specs=[pl.BlockSpec(memory_space=pltpu.SMEM),
                  pl.BlockSpec((n_h, MOBA_BLOCK, dh), lambda h, i: (h, i, 0)),
                  pl.BlockSpec((n_h, t, dh), lambda h, i: (n_groups + h, 0, 0)),
                  pl.BlockSpec((n_h, n_blocks, MOBA_VT_ROWS, MOBA_BLOCK), lambda h, i: (h, 0, 0, 0))],
        out_specs=pl.BlockSpec((MOBA_BLOCK, n_h * dh), lambda h, i: (i, h)),
        out_shape=jax.ShapeDtypeStruct((t, N_HEADS_B * dh), BF16),
        scratch_shapes=[pltpu.VMEM((n_h, n_blocks, dh), F32),
                        pltpu.VMEM((n_h, n_blocks, MOBA_BLOCK), F32),
                        pltpu.VMEM((n_h, MOBA_VT_ROWS, MOBA_BLOCK), F32),
                        pltpu.VMEM((2, n_h, MOBA_UNROLL * MOBA_BLOCK, MOBA_BLOCK), F32)],
        compiler_params=_params("arbitrary", "arbitrary"),
        name="moba",
    )(slopes, bqk, bqk, bvt)


def _merge_kernel(x_ref, ada_ref, ya_ref, yb_ref, gates_a_ref, gates_b_ref, wa_ref, wb_ref, wo_ref, o_ref):
    pa = jnp.dot(ya_ref[...], wa_ref[...], preferred_element_type=F32)
    pb = jnp.dot(yb_ref[...], wb_ref[...], preferred_element_type=F32)
    merged = gates_a_ref[...].astype(F32) * pa + gates_b_ref[...].astype(F32) * pb
    out = jnp.dot(merged.astype(BF16), wo_ref[...], preferred_element_type=F32)
    o_ref[...] = x_ref[...] + ada_ref[5:6, :] * out


def _merge(x, ada, y_a, y_b, gates, w_a, w_b, w_o, *, tm=256):
    t, d = x.shape
    const = lambda i: (0, 0)
    return pl.pallas_call(
        _merge_kernel,
        grid=(t // tm,),
        in_specs=[pl.BlockSpec((tm, d), lambda i: (i, 0)),
                  pl.BlockSpec((N_ADA, d), const),
                  pl.BlockSpec((tm, y_a.shape[1]), lambda i: (i, 0)),
                  pl.BlockSpec((tm, y_b.shape[1]), lambda i: (i, 0)),
                  pl.BlockSpec((tm, d), lambda i: (i, 0)),
                  pl.BlockSpec((tm, d), lambda i: (i, 1)),
                  pl.BlockSpec(w_a.shape, const),
                  pl.BlockSpec(w_b.shape, const),
                  pl.BlockSpec(w_o.shape, const)],
        out_specs=pl.BlockSpec((tm, d), lambda i: (i, 0)),
        out_shape=jax.ShapeDtypeStruct((t, d), F32),
        compiler_params=_params("parallel"),
        name="merge",
    )(x, ada, y_a, y_b, gates, gates, w_a, w_b, w_o)


def _layer(x, c, w_ada, b_ada, g_ffn1, ffn1_w_gate, ffn1_w_up, ffn1_w_down,
           g_mix, w_in, q_norm, k_norm, w_gate, w_branch_a, w_branch_b, w_out,
           g_ffn2, ffn2_w_gate, ffn2_w_up, ffn2_w_down):
    t, d = x.shape
    ada = _ada(c, w_ada, b_ada).reshape(N_ADA, d)
    slopes = jnp.exp2(-8.0 * jnp.arange(1, N_HEADS + 1, dtype=F32) / N_HEADS) * LOG2_E

    x = _ffn(x, ada, g_ffn1, ffn1_w_gate.astype(BF16), ffn1_w_up.astype(BF16), ffn1_w_down.astype(BF16), sub=0)

    scale = HEAD_DIM ** -0.5 * LOG2_E
    n_gate_heads = w_gate.shape[1] // HEAD_DIM
    gains = jnp.concatenate([q_norm[:N_HEADS_A] * scale, k_norm[:N_HEADS_A], jnp.ones((N_HEADS_A, HEAD_DIM), F32),
                             q_norm[N_HEADS_A:] * scale, k_norm[N_HEADS_A:],
                             jnp.ones((N_HEADS_B + n_gate_heads, HEAD_DIM), F32)], axis=0)
    w_all = jnp.concatenate([w_in.astype(BF16), w_gate.astype(BF16)], axis=1)
    offsets = jnp.arange(MOBA_BLOCK, dtype=F32)
    key_weights = jnp.exp2(slopes[N_HEADS_A:, None, None] * offsets[None, None, :])
    key_weights = jnp.pad(key_weights, ((0, 0), (0, MOBA_VT_PAD - 1), (0, 0)))
    a0, a1, a2, bqk, bvt, gates = _proj(x, ada, g_mix, w_all, gains, key_weights)

    y_a = _dilated((a0, a1, a2), slopes)
    y_b = _moba(bqk, bvt, slopes)

    x = _merge(x, ada, y_a, y_b, gates, w_branch_a.astype(BF16), w_branch_b.astype(BF16), w_out.astype(BF16))
    x = _ffn(x, ada, g_ffn2, ffn2_w_gate.astype(BF16), ffn2_w_up.astype(BF16), ffn2_w_down.astype(BF16), sub=2)
    return x


def kernel(x, c, w_ada, b_ada, g_ffn1, ffn1_w_gate, ffn1_w_up, ffn1_w_down, g_mix, w_in, q_norm, k_norm,
           w_gate, w_branch_a, w_branch_b, w_out, g_ffn2, ffn2_w_gate, ffn2_w_up, ffn2_w_down):
    batch, depth = x.shape[0], w_ada.shape[0]
    outs = []
    for bi in range(batch):
        xb = x[bi]
        for l in range(depth):
            xb = _layer(xb, c[bi], w_ada[l], b_ada[l], g_ffn1[l], ffn1_w_gate[l], ffn1_w_up[l],
                        ffn1_w_down[l], g_mix[l], w_in[l], q_norm[l], k_norm[l], w_gate[l],
                        w_branch_a[l], w_branch_b[l], w_out[l], g_ffn2[l], ffn2_w_gate[l],
                        ffn2_w_up[l], ffn2_w_down[l])
        outs.append(xb)
    return jnp.stack(outs, axis=0)
```

```python
import functools

import jax
import jax.numpy as jnp
from jax import lax
from jax.experimental import pallas as pl
from jax.experimental.pallas import tpu as pltpu

HEAD_DIM = 128
DIL_PATTERNS = ((128, 1), (512, 4), (2048, 16))
N_DIL = len(DIL_PATTERNS)
HEADS_PER_DIL_GROUP = 4
N_HEADS_A = HEADS_PER_DIL_GROUP * N_DIL
N_HEADS_B = 8
N_HEADS = N_HEADS_A + N_HEADS_B
BAND = 128
MOBA_BLOCK = 256
MOBA_TOPK = 3
N_ADA = 9
EPS = 1e-6
NEG_INF = -1e30
LOG2_E = 1.4426950408889634

LANES = 128
VMEM_LIMIT = 56 * 1024 * 1024

BF16 = jnp.bfloat16
F32 = jnp.float32

_NT = (((1,), (1,)), ((), ()))


def _params(*sem):
    return pltpu.CompilerParams(dimension_semantics=sem, vmem_limit_bytes=VMEM_LIMIT)


def _sigmoid(x):
    return 1.0 / (1.0 + jnp.exp(-x))


def _norm_modulate(x, g, shift, scale):
    ms = jnp.mean(x * x, axis=-1, keepdims=True)
    y = (x * lax.rsqrt(ms + EPS)) * g
    return y * (1.0 + scale) + shift


def _head_rmsnorm(a, gain):
    ms = jnp.mean(a * a, axis=-1, keepdims=True)
    return (a * lax.rsqrt(ms + EPS)) * gain


def _ada_kernel(c_ref, w_ref, b_ref, o_ref):
    c = c_ref[...]
    s = c * _sigmoid(c)
    o_ref[...] = jnp.sum(s * w_ref[...], axis=0, keepdims=True) + b_ref[...]


def _ada(c, w_ada, b_ada, *, tn=1024):
    d, n = w_ada.shape
    return pl.pallas_call(
        _ada_kernel,
        grid=(n // tn,),
        in_specs=[pl.BlockSpec((d, 1), lambda j: (0, 0)),
                  pl.BlockSpec((d, tn), lambda j: (0, j)),
                  pl.BlockSpec((1, tn), lambda j: (0, j))],
        out_specs=pl.BlockSpec((1, tn), lambda j: (0, j)),
        out_shape=jax.ShapeDtypeStruct((1, n), F32),
        compiler_params=_params("arbitrary"),
        name="ada",
    )(c.reshape(d, 1), w_ada, b_ada.reshape(1, n))


def _ffn_kernel(x_ref, ada_ref, g_ref, wg_ref, wu_ref, wd_ref, o_ref, u_ref, *, sub):
    f = pl.program_id(1)

    @pl.when(f == 0)
    def _():
        u = _norm_modulate(x_ref[...], g_ref[...],
                           ada_ref[3 * sub:3 * sub + 1, :], ada_ref[3 * sub + 1:3 * sub + 2, :])
        u_ref[...] = u.astype(BF16)
        o_ref[...] = jnp.zeros_like(o_ref)

    u = u_ref[...]
    hg = jnp.dot(u, wg_ref[...].astype(BF16), preferred_element_type=F32)
    hu = jnp.dot(u, wu_ref[...].astype(BF16), preferred_element_type=F32)
    h = (hg * _sigmoid(hg)) * hu
    o_ref[...] += jnp.dot(h.astype(BF16), wd_ref[...].astype(BF16), preferred_element_type=F32)

    @pl.when(f == pl.num_programs(1) - 1)
    def _():
        gate = ada_ref[3 * sub + 2:3 * sub + 3, :]
        o_ref[...] = x_ref[...] + (0.5 * gate) * o_ref[...]


def _ffn(x, ada, g, w_gate, w_up, w_down, *, sub, tm=1024, tf=256):
    t, d = x.shape
    dff = w_gate.shape[1]
    return pl.pallas_call(
        functools.partial(_ffn_kernel, sub=sub),
        grid=(t // tm, dff // tf),
        in_specs=[pl.BlockSpec((tm, d), lambda i, f: (i, 0), pipeline_mode=pl.Buffered(1)),
                  pl.BlockSpec((N_ADA, d), lambda i, f: (0, 0)),
                  pl.BlockSpec((1, d), lambda i, f: (0, 0)),
                  pl.BlockSpec((d, tf), lambda i, f: (0, f)),
                  pl.BlockSpec((d, tf), lambda i, f: (0, f)),
                  pl.BlockSpec((tf, d), lambda i, f: (f, 0))],
        out_specs=pl.BlockSpec((tm, d), lambda i, f: (i, 0)),
        out_shape=jax.ShapeDtypeStruct((t, d), F32),
        scratch_shapes=[pltpu.VMEM((tm, d), BF16)],
        compiler_params=_params("parallel", "arbitrary"),
        name=f"ffn{sub}",
    )(x, ada, g.reshape(1, d), w_gate, w_up, w_down)


HPB = HEADS_PER_DIL_GROUP
PROJ_TN = HPB * HEAD_DIM
BLK_A_END = 3 * N_DIL
BLK_BQK0 = BLK_A_END
BLK_BV0 = BLK_BQK0 + 2 * (N_HEADS_B // HPB)
BLK_GATE0 = BLK_BV0 + N_HEADS_B // HPB


MOBA_VT_PAD = 16
MOBA_VT_ROWS = HEAD_DIM + MOBA_VT_PAD


PROJ_ROWS = MOBA_BLOCK


def _proj_kernel(nflag_ref, x_ref, ada_ref, g_ref, win_ref, wgate_ref, gain_ref, kw_ref,
                 a0_ref, a1_ref, a2_ref, bqk_ref, bvt_ref, gate_ref, u_ref, stg_ref):
    j = pl.program_id(1)
    tm = x_ref.shape[0]
    n_chunks = tm // PROJ_ROWS

    @pl.when(j == 0)
    def _():
        u = _norm_modulate(x_ref[...], g_ref[...], ada_ref[3:4, :], ada_ref[4:5, :])
        u_ref[...] = u.astype(BF16)

    def chunks(w_ref):
        w = w_ref[...].astype(BF16)
        for c in range(n_chunks):
            rows = slice(c * PROJ_ROWS, (c + 1) * PROJ_ROWS)
            yield c, jnp.dot(u_ref[rows, :], w, preferred_element_type=F32)

    def head(res, h):
        a = res[:, h * HEAD_DIM:(h + 1) * HEAD_DIM]
        flag = nflag_ref[j]
        ms = jnp.mean(a * a, axis=-1, keepdims=True)
        return (a * (lax.rsqrt(ms + EPS) * flag + (1.0 - flag))) * gain_ref[0, h:h + 1, :]

    def dilated_block(o_ref, r):
        n = PROJ_ROWS // r
        for c, res in chunks(win_ref):
            for h in range(HPB):
                y = head(res, h)
                if r == 1:
                    o_ref[h, c * n:(c + 1) * n, :] = y.astype(BF16)
                else:
                    stg_ref[c % 2, h] = y
                    for rho in range(r):
                        part = stg_ref.at[c % 2, h][pl.ds(rho, n, stride=r), :]
                        o_ref[h, c * n:(c + 1) * n, rho * HEAD_DIM:(rho + 1) * HEAD_DIM] = part.astype(BF16)

    a_refs = (a0_ref, a1_ref, a2_ref)
    for g, (_, r) in enumerate(DIL_PATTERNS):
        pl.when((j < BLK_A_END) & (lax.rem(j, N_DIL) == g))(functools.partial(dilated_block, a_refs[g], r))

    @pl.when((j >= BLK_BQK0) & (j < BLK_BV0))
    def _():
        for c, res in chunks(win_ref):
            for h in range(HPB):
                bqk_ref[h, c * PROJ_ROWS:(c + 1) * PROJ_ROWS, :] = head(res, h).astype(BF16)

    @pl.when((j >= BLK_BV0) & (j < BLK_GATE0))
    def _():
        for c, res in chunks(win_ref):
            for h in range(HPB):
                kw = kw_ref[0, h]
                yt = res[:, h * HEAD_DIM:(h + 1) * HEAD_DIM].T
                bvt_ref[h, c, 0:HEAD_DIM, :] = (yt * kw[0:1, :]).astype(BF16)
                bvt_ref[h, c, HEAD_DIM:, :] = kw.astype(BF16)

    @pl.when(j >= BLK_GATE0)
    def _():
        for c, res in chunks(wgate_ref):
            gate_ref[c * PROJ_ROWS:(c + 1) * PROJ_ROWS, :] = _sigmoid(res).astype(BF16)


def _proj(x, ada, g, w_in, w_gate, gains, norm_flags, key_weights, *, tm=1024):
    t, d = x.shape
    n_gate = w_gate.shape[1] // PROJ_TN
    n_blk = BLK_GATE0 + n_gate
    assert w_in.shape[1] == BLK_GATE0 * PROJ_TN
    n_bv = BLK_GATE0 - BLK_BV0

    def a_spec(gi, r):
        return pl.BlockSpec((HPB, tm // r, r * HEAD_DIM),
                            lambda i, j: (jnp.clip((j - gi + N_DIL - 1) // N_DIL, 0, 2), i, 0))

    a_shapes = [jax.ShapeDtypeStruct((3 * HPB, t // r, r * HEAD_DIM), BF16) for _, r in DIL_PATTERNS]
    out_shape = a_shapes + [
        jax.ShapeDtypeStruct((2 * N_HEADS_B, t, HEAD_DIM), BF16),
        jax.ShapeDtypeStruct((N_HEADS_B, t // MOBA_BLOCK, MOBA_VT_ROWS, MOBA_BLOCK), BF16),
        jax.ShapeDtypeStruct((t, n_gate * PROJ_TN), BF16)]
    out_specs = [a_spec(gi, r) for gi, (_, r) in enumerate(DIL_PATTERNS)] + [
        pl.BlockSpec((HPB, tm, HEAD_DIM),
                     lambda i, j: (jnp.clip(j - BLK_BQK0, 0, BLK_BV0 - BLK_BQK0 - 1), i, 0)),
        pl.BlockSpec((HPB, tm // MOBA_BLOCK, MOBA_VT_ROWS, MOBA_BLOCK),
                     lambda i, j: (jnp.clip(j - BLK_BV0, 0, n_bv - 1), i, 0, 0)),
        pl.BlockSpec((tm, PROJ_TN), lambda i, j: (i, jnp.clip(j - BLK_GATE0, 0, n_gate - 1)))]
    return pl.pallas_call(
        _proj_kernel,
        grid=(t // tm, n_blk),
        in_specs=[pl.BlockSpec(memory_space=pltpu.SMEM),
                  pl.BlockSpec((tm, d), lambda i, j: (i, 0), pipeline_mode=pl.Buffered(1)),
                  pl.BlockSpec((N_ADA, d), lambda i, j: (0, 0)),
                  pl.BlockSpec((1, d), lambda i, j: (0, 0)),
                  pl.BlockSpec((d, PROJ_TN), lambda i, j: (0, jnp.minimum(j, BLK_GATE0 - 1))),
                  pl.BlockSpec((d, PROJ_TN), lambda i, j: (0, jnp.clip(j - BLK_GATE0, 0, n_gate - 1))),
                  pl.BlockSpec((1, HPB, HEAD_DIM), lambda i, j: (j, 0, 0)),
                  pl.BlockSpec((1, HPB, MOBA_VT_PAD, MOBA_BLOCK),
                               lambda i, j: (jnp.clip(j - BLK_BV0, 0, n_bv - 1), 0, 0, 0))],
        out_specs=out_specs,
        out_shape=out_shape,
        scratch_shapes=[pltpu.VMEM((tm, d), BF16), pltpu.VMEM((2, HPB, PROJ_ROWS, HEAD_DIM), F32)],
        compiler_params=_params("arbitrary", "arbitrary"),
        name="proj",
    )(norm_flags, x, ada, g.reshape(1, d), w_in, w_gate, gains.reshape(n_blk, HPB, HEAD_DIM),
      key_weights.reshape(n_bv, HPB, MOBA_VT_PAD, MOBA_BLOCK))


DIL_TOKENS = 2048


def _band_block(q, k2, v2, bias):
    s = lax.dot_general(q, k2, _NT, preferred_element_type=F32) + bias
    m = jnp.max(s, axis=-1, keepdims=True)
    p = jnp.exp2(s - m)
    denom = jnp.sum(p, axis=-1, keepdims=True)
    o = jnp.dot(p.astype(BF16), v2, preferred_element_type=F32) / denom
    lse = jnp.broadcast_to(m + jnp.log2(denom), (BAND, LANES))
    return o, lse


def _dilated_kernel(slopes_ref,
                    q0, k0, v0, kp0, vp0,
                    q1, k1, v1, kp1, vp1,
                    q2, k2, v2, kp2, vp2,
                    y_ref, o_scr, l_scr):
    j = pl.program_id(0)
    b = pl.program_id(1)

    qi = lax.broadcasted_iota(jnp.int32, (BAND, 2 * BAND), 0)
    ki = lax.broadcasted_iota(jnp.int32, (BAND, 2 * BAND), 1)
    dist = BAND + qi - ki
    in_band = (dist >= 0) & (dist <= BAND)
    dist_f = dist.astype(F32)
    first_ok = (ki + jnp.minimum(b, 1) * BAND) >= BAND

    groups = ((q0, k0, v0, kp0, vp0), (q1, k1, v1, kp1, vp1), (q2, k2, v2, kp2, vp2))
    for g, (_, r) in enumerate(DIL_PATTERNS):
        q_ref, k_ref, v_ref, kp_ref, vp_ref = groups[g]
        slope = slopes_ref[g * HEADS_PER_DIL_GROUP + j]
        bias = jnp.where(in_band, dist_f * (-slope * r), NEG_INF)
        bias_first = jnp.where(first_ok, bias, NEG_INF)
        n_blk = DIL_TOKENS // (r * BAND)
        for rho in range(r):
            cols = slice(rho * HEAD_DIM, (rho + 1) * HEAD_DIM)

            def store(i, o, lse, g=g, r=r, rho=rho):
                rows = pl.ds(i * (BAND * r) + rho, BAND, stride=r) if r > 1 else pl.ds(i * BAND, BAND)
                o_scr.at[g][rows, :] = o
                l_scr.at[g][rows, :] = lse

            kk = jnp.concatenate([kp_ref[:, cols], k_ref[0:BAND, cols]], axis=0)
            vv = jnp.concatenate([vp_ref[:, cols], v_ref[0:BAND, cols]], axis=0)
            o, lse = _band_block(q_ref[0:BAND, cols], kk, vv, bias_first)
            store(0, o, lse)

            if n_blk > 1:
                def body(i, carry, q_ref=q_ref, k_ref=k_ref, v_ref=v_ref, cols=cols, bias=bias, store=store):
                    lo = pl.multiple_of(i * BAND - BAND, BAND)
                    qs = pl.multiple_of(i * BAND, BAND)
                    o, lse = _band_block(q_ref[pl.ds(qs, BAND), cols],
                                         k_ref[pl.ds(lo, 2 * BAND), cols],
                                         v_ref[pl.ds(lo, 2 * BAND), cols], bias)
                    store(i, o, lse)
                    return carry
                lax.fori_loop(1, n_blk, body, 0)

    l0, l1, l2 = l_scr[0], l_scr[1], l_scr[2]
    mx = jnp.maximum(jnp.maximum(l0, l1), l2)
    e0, e1, e2 = jnp.exp2(l0 - mx), jnp.exp2(l1 - mx), jnp.exp2(l2 - mx)
    tot = e0 + e1 + e2
    y = (e0 / tot) * o_scr[0] + (e1 / tot) * o_scr[1] + (e2 / tot) * o_scr[2]
    y_ref[...] = y.astype(BF16)


def _dilated(a_views, slopes):
    t = a_views[0].shape[1]
    dh = HEAD_DIM
    nb = t // DIL_TOKENS
    hq, hk, hv = 0, HPB, 2 * HPB
    args, specs = [], []
    for g, (_, r) in enumerate(DIL_PATTERNS):
        rows = DIL_TOKENS // r
        prev_per_blk = rows // BAND
        width = r * dh

        def cur(base, rows=rows, width=width):
            return pl.BlockSpec((None, rows, width), lambda j, b: (base + j, b, 0))

        def prev(base, width=width, ppb=prev_per_blk):
            return pl.BlockSpec((None, BAND, width),
                                lambda j, b: (base + j, jnp.maximum(b * ppb - 1, 0), 0))

        args += [a_views[g]] * 5
        specs += [cur(hq), cur(hk), cur(hv), prev(hk), prev(hv)]

    return pl.pallas_call(
        _dilated_kernel,
        grid=(HEADS_PER_DIL_GROUP, nb),
        in_specs=[pl.BlockSpec(memory_space=pltpu.SMEM)] + specs,
        out_specs=pl.BlockSpec((DIL_TOKENS, dh), lambda j, b: (b, j)),
        out_shape=jax.ShapeDtypeStruct((t, HEADS_PER_DIL_GROUP * dh), BF16),
        scratch_shapes=[pltpu.VMEM((N_DIL, DIL_TOKENS, dh), F32),
                        pltpu.VMEM((N_DIL, DIL_TOKENS, LANES), F32)],
        compiler_params=_params("arbitrary", "arbitrary"),
        name="dilated",
    )(slopes, *args)


MOBA_UNROLL = 4


MOBA_HEADS_PER_STEP = 4


def _moba_kernel(slopes_ref, q_ref, k_ref, vt_ref, y_ref, kmean_scr, sel_scr, acc_scr, s_scr, *, n_blocks):
    hg = pl.program_id(0)
    qb = pl.program_id(1)
    blk = MOBA_BLOCK
    n_h, _, dh = q_ref.shape
    heads = range(n_h)
    slope = [slopes_ref[N_HEADS_A + hg * n_h + hh] for hh in heads]

    @pl.when(qb == 0)
    def _():
        def body(n, carry):
            for hh in heads:
                kn = k_ref[hh, pl.ds(pl.multiple_of(n * blk, blk), blk), :].astype(F32)
                kmean_scr[hh, pl.ds(n, 1), :] = jnp.mean(kn, axis=0, keepdims=True)
            return carry
        lax.fori_loop(0, n_blocks, body, 0)

    q = [q_ref[hh] for hh in heads]

    blk_id = lax.broadcasted_iota(jnp.int32, (n_blocks, blk), 0)
    blk_id_f = blk_id.astype(F32)
    past = blk_id < qb
    key_i = lax.broadcasted_iota(jnp.int32, (blk, blk), 0)
    qry_i = lax.broadcasted_iota(jnp.int32, (blk, blk), 1)
    own = pl.multiple_of(qb * blk, blk)

    in_block_max = [slope[hh] * (blk - 1) for hh in heads]

    m0 = []
    for hh in heads:
        gate = lax.dot_general(kmean_scr[hh], q[hh].astype(F32), _NT,
                               precision=lax.Precision.HIGHEST, preferred_element_type=F32)
        work = jnp.where(past, gate, NEG_INF)
        sel = jnp.zeros((n_blocks, blk), F32)
        for _ in range(MOBA_TOPK):
            mx = jnp.max(work, axis=0, keepdims=True)
            first = jnp.min(jnp.where(work == mx, blk_id_f, float(n_blocks)), axis=0, keepdims=True)
            hit = blk_id_f == first
            sel = jnp.where(hit, 1.0, sel)
            work = jnp.where(hit, -jnp.inf, work)
        sel_scr[hh] = jnp.where(past, sel, 0.0)

        s = lax.dot_general(k_ref[hh, pl.ds(own, blk), :], q[hh], _NT, preferred_element_type=F32)
        s = jnp.where(key_i <= qry_i, s, NEG_INF)
        m0.append(jnp.max(s, axis=0, keepdims=True) + in_block_max[hh])
        p = jnp.exp2(s - m0[hh])
        acc_scr[hh] = jnp.dot(vt_ref[hh, qb], p.astype(BF16), preferred_element_type=F32)

    width = MOBA_UNROLL
    n_chunks = (qb + width - 1) // width

    def chunk_scores(hh, ci, slot):
        start = pl.multiple_of(ci * (width * blk), width * blk)
        s_scr[slot, hh] = lax.dot_general(k_ref[hh, pl.ds(start, width * blk), :], q[hh], _NT,
                                          preferred_element_type=F32)

    def past_chunk(ci, m_run, src, dst):
        n0 = ci * width
        nxt = jnp.minimum(ci + 1, n_chunks - 1)
        out = []
        for hh in heads:
            chunk_scores(hh, nxt, dst)
            c, chosen = [], []
            m_chunk = jnp.full((1, blk), NEG_INF, F32)
            for a in range(width):
                s = s_scr[src, hh, a * blk:(a + 1) * blk, :]
                c.append(slope[hh] * jnp.full((1, blk), (n0 + a - qb) * blk, jnp.int32).astype(F32))
                chosen.append(sel_scr[hh, pl.ds(n0 + a, 1), :] > 0.5)
                m_blk = jnp.max(s, axis=0, keepdims=True) + (c[a] + in_block_max[hh])
                m_chunk = jnp.maximum(m_chunk, jnp.where(chosen[a], m_blk, NEG_INF))
            pv = jnp.zeros(acc_scr.shape[1:], F32)
            for a in range(width):
                s = s_scr[src, hh, a * blk:(a + 1) * blk, :]
                p = jnp.exp2(s - jnp.where(chosen[a], m_chunk - c[a], -NEG_INF))
                pv = pv + jnp.dot(vt_ref[hh, n0 + a], p.astype(BF16), preferred_element_type=F32)
            m_new = jnp.maximum(m_run[hh], m_chunk)
            alpha = jnp.exp2(m_run[hh] - m_new)
            beta = jnp.exp2(m_chunk - m_new)
            acc_scr[hh] = alpha * acc_scr[hh] + beta * pv
            out.append(m_new)
        return tuple(out)

    for hh in heads:
        chunk_scores(hh, 0, 0)

    def body(ci, m_run):
        return lax.cond(lax.rem(ci, 2) == 0,
                        lambda m: past_chunk(ci, m, 0, 1), lambda m: past_chunk(ci, m, 1, 0), m_run)

    lax.fori_loop(0, n_chunks, body, tuple(m0))
    for hh in heads:
        y = acc_scr[hh, 0:dh, :] / acc_scr[hh, dh:dh + 1, :]
        y_ref[:, hh * dh:(hh + 1) * dh] = y.T.astype(BF16)


def _moba(bqk, bvt, slopes):
    _, t, dh = bqk.shape
    n_blocks = t // MOBA_BLOCK
    assert n_blocks % MOBA_UNROLL == 0, "the padded last chunk must stay inside the key array"
    n_h = MOBA_HEADS_PER_STEP
    n_groups = N_HEADS_B // n_h
    return pl.pallas_call(
        functools.partial(_moba_kernel, n_blocks=n_blocks),
        grid=(n_groups, n_blocks),
        in_specs=[pl.BlockSpec(memory_space=pltpu.SMEM),
                  pl.BlockSpec((n_h, MOBA_BLOCK, dh), lambda h, i: (h, i, 0)),
                  pl.BlockSpec((n_h, t, dh), lambda h, i: (n_groups + h, 0, 0)),
                  pl.BlockSpec((n_h, n_blocks, MOBA_VT_ROWS, MOBA_BLOCK), lambda h, i: (h, 0, 0, 0))],
        out_specs=pl.BlockSpec((MOBA_BLOCK, n_h * dh), lambda h, i: (i, h)),
        out_shape=jax.ShapeDtypeStruct((t, N_HEADS_B * dh), BF16),
        scratch_shapes=[pltpu.VMEM((n_h, n_blocks, dh), F32),
                        pltpu.VMEM((n_h, n_blocks, MOBA_BLOCK), F32),
                        pltpu.VMEM((n_h, MOBA_VT_ROWS, MOBA_BLOCK), F32),
                        pltpu.VMEM((2, n_h, MOBA_UNROLL * MOBA_BLOCK, MOBA_BLOCK), F32)],
        compiler_params=_params("arbitrary", "arbitrary"),
        name="moba",
    )(slopes, bqk, bqk, bvt)


def _merge_kernel(x_ref, ada_ref, ya_ref, yb_ref, gates_a_ref, gates_b_ref, wa_ref, wb_ref, wo_ref, o_ref):
    pa = jnp.dot(ya_ref[...], wa_ref[...], preferred_element_type=F32)
    pb = jnp.dot(yb_ref[...], wb_ref[...], preferred_element_type=F32)
    merged = gates_a_ref[...].astype(F32) * pa + gates_b_ref[...].astype(F32) * pb
    out = jnp.dot(merged.astype(BF16), wo_ref[...], preferred_element_type=F32)
    o_ref[...] = x_ref[...] + ada_ref[5:6, :] * out


def _merge(x, ada, y_a, y_b, gates, w_a, w_b, w_o, *, tm=256):
    t, d = x.shape
    const = lambda i: (0, 0)
    return pl.pallas_call(
        _merge_kernel,
        grid=(t // tm,),
        in_specs=[pl.BlockSpec((tm, d), lambda i: (i, 0)),
                  pl.BlockSpec((N_ADA, d), const),
                  pl.BlockSpec((tm, y_a.shape[1]), lambda i: (i, 0)),
                  pl.BlockSpec((tm, y_b.shape[1]), lambda i: (i, 0)),
                  pl.BlockSpec((tm, d), lambda i: (i, 0)),
                  pl.BlockSpec((tm, d), lambda i: (i, 1)),
                  pl.BlockSpec(w_a.shape, const),
                  pl.BlockSpec(w_b.shape, const),
                  pl.BlockSpec(w_o.shape, const)],
        out_specs=pl.BlockSpec((tm, d), lambda i: (i, 0)),
        out_shape=jax.ShapeDtypeStruct((t, d), F32),
        compiler_params=_params("parallel"),
        name="merge",
    )(x, ada, y_a, y_b, gates, gates, w_a, w_b, w_o)


def _layer(x, c, w_ada, b_ada, g_ffn1, ffn1_w_gate, ffn1_w_up, ffn1_w_down,
           g_mix, w_in, q_norm, k_norm, w_gate, w_branch_a, w_branch_b, w_out,
           g_ffn2, ffn2_w_gate, ffn2_w_up, ffn2_w_down):
    t, d = x.shape
    ada = _ada(c, w_ada, b_ada).reshape(N_ADA, d)
    slopes = jnp.exp2(-8.0 * jnp.arange(1, N_HEADS + 1, dtype=F32) / N_HEADS) * LOG2_E

    x = _ffn(x, ada, g_ffn1, ffn1_w_gate, ffn1_w_up, ffn1_w_down, sub=0)

    scale = HEAD_DIM ** -0.5 * LOG2_E
    n_gate_heads = w_gate.shape[1] // HEAD_DIM
    gains = jnp.concatenate([q_norm[:N_HEADS_A] * scale, k_norm[:N_HEADS_A], jnp.ones((N_HEADS_A, HEAD_DIM), F32),
                             q_norm[N_HEADS_A:] * scale, k_norm[N_HEADS_A:],
                             jnp.ones((N_HEADS_B + n_gate_heads, HEAD_DIM), F32)], axis=0)
    norm_flags = jnp.concatenate([jnp.ones((2 * N_DIL,), F32), jnp.zeros((N_DIL,), F32),
                                  jnp.ones((BLK_BV0 - BLK_BQK0,), F32),
                                  jnp.zeros((BLK_GATE0 - BLK_BV0 + n_gate_heads // HPB,), F32)])
    offsets = jnp.arange(MOBA_BLOCK, dtype=F32)
    key_weights = jnp.exp2(slopes[N_HEADS_A:, None, None] * offsets[None, None, :])
    key_weights = jnp.pad(key_weights, ((0, 0), (0, MOBA_VT_PAD - 1), (0, 0)))
    a0, a1, a2, bqk, bvt, gates = _proj(x, ada, g_mix, w_in, w_gate, gains, norm_flags, key_weights)

    y_a = _dilated((a0, a1, a2), slopes)
    y_b = _moba(bqk, bvt, slopes)

    x = _merge(x, ada, y_a, y_b, gates, w_branch_a.astype(BF16), w_branch_b.astype(BF16), w_out.astype(BF16))
    x = _ffn(x, ada, g_ffn2, ffn2_w_gate, ffn2_w_up, ffn2_w_down, sub=2)
    return x


def kernel(x, c, w_ada, b_ada, g_ffn1, ffn1_w_gate, ffn1_w_up, ffn1_w_down, g_mix, w_in, q_norm, k_norm,
           w_gate, w_branch_a, w_branch_b, w_out, g_ffn2, ffn2_w_gate, ffn2_w_up, ffn2_w_down):
    batch, depth = x.shape[0], w_ada.shape[0]
    outs = []
    for bi in range(batch):
        xb = x[bi]
        for l in range(depth):
            xb = _layer(xb, c[bi], w_ada[l], b_ada[l], g_ffn1[l], ffn1_w_gate[l], ffn1_w_up[l],
                        ffn1_w_down[l], g_mix[l], w_in[l], q_norm[l], k_norm[l], w_gate[l],
                        w_branch_a[l], w_branch_b[l], w_out[l], g_ffn2[l], ffn2_w_gate[l],
                        ffn2_w_up[l], ffn2_w_down[l])
        outs.append(xb)
    return jnp.stack(outs, axis=0)
```

```python
import functools

import jax
import jax.numpy as jnp
from jax import lax
from jax.experimental import pallas as pl
from jax.experimental.pallas import tpu as pltpu

HEAD_DIM = 128
DIL_PATTERNS = ((128, 1), (512, 4), (2048, 16))
N_DIL = len(DIL_PATTERNS)
HEADS_PER_DIL_GROUP = 4
N_HEADS_A = HEADS_PER_DIL_GROUP * N_DIL
N_HEADS_B = 8
N_HEADS = N_HEADS_A + N_HEADS_B
BAND = 128
MOBA_BLOCK = 256
MOBA_TOPK = 3
N_ADA = 9
EPS = 1e-6
NEG_INF = -1e30
LOG2_E = 1.4426950408889634

LANES = 128
VMEM_LIMIT = 56 * 1024 * 1024

BF16 = jnp.bfloat16
F32 = jnp.float32

_NT = (((1,), (1,)), ((), ()))


def _params(*sem):
    return pltpu.CompilerParams(dimension_semantics=sem, vmem_limit_bytes=VMEM_LIMIT)


def _sigmoid(x):
    return 1.0 / (1.0 + jnp.exp(-x))


def _norm_modulate(x, g, shift, scale):
    ms = jnp.mean(x * x, axis=-1, keepdims=True)
    y = (x * lax.rsqrt(ms + EPS)) * g
    return y * (1.0 + scale) + shift


def _head_rmsnorm(a, gain):
    ms = jnp.mean(a * a, axis=-1, keepdims=True)
    return (a * lax.rsqrt(ms + EPS)) * gain


def _ada_kernel(c_ref, w_ref, b_ref, o_ref):
    c = c_ref[...]
    s = c * _sigmoid(c)
    o_ref[...] = jnp.sum(s * w_ref[...], axis=0, keepdims=True) + b_ref[...]


def _ada(c, w_ada, b_ada, *, tn=1024):
    d, n = w_ada.shape
    return pl.pallas_call(
        _ada_kernel,
        grid=(n // tn,),
        in_specs=[pl.BlockSpec((d, 1), lambda j: (0, 0)),
                  pl.BlockSpec((d, tn), lambda j: (0, j)),
                  pl.BlockSpec((1, tn), lambda j: (0, j))],
        out_specs=pl.BlockSpec((1, tn), lambda j: (0, j)),
        out_shape=jax.ShapeDtypeStruct((1, n), F32),
        compiler_params=_params("arbitrary"),
        name="ada",
    )(c.reshape(d, 1), w_ada, b_ada.reshape(1, n))


DMA_SPLIT = 4


def _row_slab_specs(rows, cols, index_map, **kw):
    def slab(p):
        def imap(*idx):
            r, c = index_map(*idx)
            return r * DMA_SPLIT + p, c
        return pl.BlockSpec((rows // DMA_SPLIT, cols), imap, **kw)
    return [slab(p) for p in range(DMA_SPLIT)]


def _bf16_rows(refs):
    return jnp.concatenate([r[...].astype(BF16) for r in refs], axis=0)


def _ffn_kernel(*refs, sub):
    n = DMA_SPLIT
    x_refs, (ada_ref, g_ref) = refs[0:n], refs[n:n + 2]
    wg_refs, wu_refs, wd_refs = refs[n + 2:2 * n + 2], refs[2 * n + 2:3 * n + 2], refs[3 * n + 2:4 * n + 2]
    o_ref, u_ref = refs[4 * n + 2:]
    f = pl.program_id(1)
    slab = x_refs[0].shape[0]

    @pl.when(f == 0)
    def _():
        for p, x_ref in enumerate(x_refs):
            u = _norm_modulate(x_ref[...], g_ref[...],
                               ada_ref[3 * sub:3 * sub + 1, :], ada_ref[3 * sub + 1:3 * sub + 2, :])
            u_ref[p * slab:(p + 1) * slab, :] = u.astype(BF16)
        o_ref[...] = jnp.zeros_like(o_ref)

    u = u_ref[...]
    hg = jnp.dot(u, _bf16_rows(wg_refs), preferred_element_type=F32)
    hu = jnp.dot(u, _bf16_rows(wu_refs), preferred_element_type=F32)
    h = (hg * _sigmoid(hg)) * hu
    o_ref[...] += jnp.dot(h.astype(BF16), _bf16_rows(wd_refs), preferred_element_type=F32)

    @pl.when(f == pl.num_programs(1) - 1)
    def _():
        gate = ada_ref[3 * sub + 2:3 * sub + 3, :]
        for p, x_ref in enumerate(x_refs):
            rows = slice(p * slab, (p + 1) * slab)
            o_ref[rows, :] = x_ref[...] + (0.5 * gate) * o_ref[rows, :]


def _ffn(x, ada, g, w_gate, w_up, w_down, *, sub, tm=1024, tf=256):
    t, d = x.shape
    dff = w_gate.shape[1]
    n = DMA_SPLIT
    return pl.pallas_call(
        functools.partial(_ffn_kernel, sub=sub),
        grid=(t // tm, dff // tf),
        in_specs=(_row_slab_specs(tm, d, lambda i, f: (i, 0), pipeline_mode=pl.Buffered(1))
                  + [pl.BlockSpec((N_ADA, d), lambda i, f: (0, 0)),
                     pl.BlockSpec((1, d), lambda i, f: (0, 0))]
                  + _row_slab_specs(d, tf, lambda i, f: (0, f))
                  + _row_slab_specs(d, tf, lambda i, f: (0, f))
                  + _row_slab_specs(tf, d, lambda i, f: (f, 0))),
        out_specs=pl.BlockSpec((tm, d), lambda i, f: (i, 0)),
        out_shape=jax.ShapeDtypeStruct((t, d), F32),
        scratch_shapes=[pltpu.VMEM((tm, d), BF16)],
        compiler_params=_params("parallel", "arbitrary"),
        name=f"ffn{sub}",
    )(*([x] * n), ada, g.reshape(1, d), *([w_gate] * n), *([w_up] * n), *([w_down] * n))


HPB = HEADS_PER_DIL_GROUP
PROJ_TN = HPB * HEAD_DIM
BLK_A_END = 3 * N_DIL
BLK_BQK0 = BLK_A_END
BLK_BV0 = BLK_BQK0 + 2 * (N_HEADS_B // HPB)
BLK_GATE0 = BLK_BV0 + N_HEADS_B // HPB


MOBA_VT_PAD = 16
MOBA_VT_ROWS = HEAD_DIM + MOBA_VT_PAD


PROJ_ROWS = MOBA_BLOCK


def _proj_kernel(*refs):
    n = DMA_SPLIT
    nflag_ref, x_refs, (ada_ref, g_ref) = refs[0], refs[1:n + 1], refs[n + 1:n + 3]
    win_refs, wgate_refs = refs[n + 3:2 * n + 3], refs[2 * n + 3:3 * n + 3]
    (gain_ref, kw_ref, a0_ref, a1_ref, a2_ref, bqk_ref, bvt_ref, gate_ref, u_ref, stg_ref) = refs[3 * n + 3:]
    j = pl.program_id(1)
    slab = x_refs[0].shape[0]
    tm = slab * n
    n_chunks = tm // PROJ_ROWS

    @pl.when(j == 0)
    def _():
        for p, x_ref in enumerate(x_refs):
            u = _norm_modulate(x_ref[...], g_ref[...], ada_ref[3:4, :], ada_ref[4:5, :])
            u_ref[p * slab:(p + 1) * slab, :] = u.astype(BF16)

    def chunks(w_refs):
        w = _bf16_rows(w_refs)
        for c in range(n_chunks):
            rows = slice(c * PROJ_ROWS, (c + 1) * PROJ_ROWS)
            yield c, jnp.dot(u_ref[rows, :], w, preferred_element_type=F32)

    def head(res, h):
        a = res[:, h * HEAD_DIM:(h + 1) * HEAD_DIM]
        flag = nflag_ref[j]
        ms = jnp.mean(a * a, axis=-1, keepdims=True)
        return (a * (lax.rsqrt(ms + EPS) * flag + (1.0 - flag))) * gain_ref[0, h:h + 1, :]

    def dilated_block(o_ref, r):
        n = PROJ_ROWS // r
        for c, res in chunks(win_refs):
            for h in range(HPB):
                y = head(res, h)
                if r == 1:
                    o_ref[h, c * n:(c + 1) * n, :] = y.astype(BF16)
                else:
                    stg_ref[c % 2, h] = y
                    for rho in range(r):
                        part = stg_ref.at[c % 2, h][pl.ds(rho, n, stride=r), :]
                        o_ref[h, c * n:(c + 1) * n, rho * HEAD_DIM:(rho + 1) * HEAD_DIM] = part.astype(BF16)

    a_refs = (a0_ref, a1_ref, a2_ref)
    for g, (_, r) in enumerate(DIL_PATTERNS):
        pl.when((j < BLK_A_END) & (lax.rem(j, N_DIL) == g))(functools.partial(dilated_block, a_refs[g], r))

    @pl.when((j >= BLK_BQK0) & (j < BLK_BV0))
    def _():
        for c, res in chunks(win_refs):
            for h in range(HPB):
                bqk_ref[h, c * PROJ_ROWS:(c + 1) * PROJ_ROWS, :] = head(res, h).astype(BF16)

    @pl.when((j >= BLK_BV0) & (j < BLK_GATE0))
    def _():
        for c, res in chunks(win_refs):
            for h in range(HPB):
                kw = kw_ref[0, h]
                yt = res[:, h * HEAD_DIM:(h + 1) * HEAD_DIM].T
                bvt_ref[h, c, 0:HEAD_DIM, :] = (yt * kw[0:1, :]).astype(BF16)
                bvt_ref[h, c, HEAD_DIM:, :] = kw.astype(BF16)

    @pl.when(j >= BLK_GATE0)
    def _():
        for c, res in chunks(wgate_refs):
            gate_ref[c * PROJ_ROWS:(c + 1) * PROJ_ROWS, :] = _sigmoid(res).astype(BF16)


def _proj(x, ada, g, w_in, w_gate, gains, norm_flags, key_weights, *, tm=1024):
    t, d = x.shape
    n_gate = w_gate.shape[1] // PROJ_TN
    n_blk = BLK_GATE0 + n_gate
    assert w_in.shape[1] == BLK_GATE0 * PROJ_TN
    n_bv = BLK_GATE0 - BLK_BV0

    def a_spec(gi, r):
        return pl.BlockSpec((HPB, tm // r, r * HEAD_DIM),
                            lambda i, j: (jnp.clip((j - gi + N_DIL - 1) // N_DIL, 0, 2), i, 0))

    a_shapes = [jax.ShapeDtypeStruct((3 * HPB, t // r, r * HEAD_DIM), BF16) for _, r in DIL_PATTERNS]
    out_shape = a_shapes + [
        jax.ShapeDtypeStruct((2 * N_HEADS_B, t, HEAD_DIM), BF16),
        jax.ShapeDtypeStruct((N_HEADS_B, t // MOBA_BLOCK, MOBA_VT_ROWS, MOBA_BLOCK), BF16),
        jax.ShapeDtypeStruct((t, n_gate * PROJ_TN), BF16)]
    out_specs = [a_spec(gi, r) for gi, (_, r) in enumerate(DIL_PATTERNS)] + [
        pl.BlockSpec((HPB, tm, HEAD_DIM),
                     lambda i, j: (jnp.clip(j - BLK_BQK0, 0, BLK_BV0 - BLK_BQK0 - 1), i, 0)),
        pl.BlockSpec((HPB, tm // MOBA_BLOCK, MOBA_VT_ROWS, MOBA_BLOCK),
                     lambda i, j: (jnp.clip(j - BLK_BV0, 0, n_bv - 1), i, 0, 0)),
        pl.BlockSpec((tm, PROJ_TN), lambda i, j: (i, jnp.clip(j - BLK_GATE0, 0, n_gate - 1)))]
    return pl.pallas_call(
        _proj_kernel,
        grid=(t // tm, n_blk),
        in_specs=([pl.BlockSpec(memory_space=pltpu.SMEM)]
                  + _row_slab_specs(tm, d, lambda i, j: (i, 0), pipeline_mode=pl.Buffered(1))
                  + [pl.BlockSpec((N_ADA, d), lambda i, j: (0, 0)),
                     pl.BlockSpec((1, d), lambda i, j: (0, 0))]
                  + _row_slab_specs(d, PROJ_TN, lambda i, j: (0, jnp.minimum(j, BLK_GATE0 - 1)))
                  + _row_slab_specs(d, PROJ_TN, lambda i, j: (0, jnp.clip(j - BLK_GATE0, 0, n_gate - 1)))
                  + [pl.BlockSpec((1, HPB, HEAD_DIM), lambda i, j: (j, 0, 0)),
                     pl.BlockSpec((1, HPB, MOBA_VT_PAD, MOBA_BLOCK),
                                  lambda i, j: (jnp.clip(j - BLK_BV0, 0, n_bv - 1), 0, 0, 0))]),
        out_specs=out_specs,
        out_shape=out_shape,
        scratch_shapes=[pltpu.VMEM((tm, d), BF16), pltpu.VMEM((2, HPB, PROJ_ROWS, HEAD_DIM), F32)],
        compiler_params=_params("arbitrary", "arbitrary"),
        name="proj",
    )(norm_flags, *([x] * DMA_SPLIT), ada, g.reshape(1, d), *([w_in] * DMA_SPLIT), *([w_gate] * DMA_SPLIT),
      gains.reshape(n_blk, HPB, HEAD_DIM),
      key_weights.reshape(n_bv, HPB, MOBA_VT_PAD, MOBA_BLOCK))


DIL_TOKENS = 2048


def _band_block(q, k2, v2, bias):
    s = lax.dot_general(q, k2, _NT, preferred_element_type=F32) + bias
    m = jnp.max(s, axis=-1, keepdims=True)
    p = jnp.exp2(s - m)
    denom = jnp.sum(p, axis=-1, keepdims=True)
    o = jnp.dot(p.astype(BF16), v2, preferred_element_type=F32) / denom
    lse = jnp.broadcast_to(m + jnp.log2(denom), (BAND, LANES))
    return o, lse


def _dilated_kernel(slopes_ref,
                    q0, k0, v0, kp0, vp0,
                    q1, k1, v1, kp1, vp1,
                    q2, k2, v2, kp2, vp2,
                    y_ref, o_scr, l_scr):
    j = pl.program_id(0)
    b = pl.program_id(1)

    qi = lax.broadcasted_iota(jnp.int32, (BAND, 2 * BAND), 0)
    ki = lax.broadcasted_iota(jnp.int32, (BAND, 2 * BAND), 1)
    dist = BAND + qi - ki
    in_band = (dist >= 0) & (dist <= BAND)
    dist_f = dist.astype(F32)
    first_ok = (ki + jnp.minimum(b, 1) * BAND) >= BAND

    groups = ((q0, k0, v0, kp0, vp0), (q1, k1, v1, kp1, vp1), (q2, k2, v2, kp2, vp2))
    for g, (_, r) in enumerate(DIL_PATTERNS):
        q_ref, k_ref, v_ref, kp_ref, vp_ref = groups[g]
        slope = slopes_ref[g * HEADS_PER_DIL_GROUP + j]
        bias = jnp.where(in_band, dist_f * (-slope * r), NEG_INF)
        bias_first = jnp.where(first_ok, bias, NEG_INF)
        n_blk = DIL_TOKENS // (r * BAND)
        for rho in range(r):
            cols = slice(rho * HEAD_DIM, (rho + 1) * HEAD_DIM)

            def store(i, o, lse, g=g, r=r, rho=rho):
                rows = pl.ds(i * (BAND * r) + rho, BAND, stride=r) if r > 1 else pl.ds(i * BAND, BAND)
                o_scr.at[g][rows, :] = o
                l_scr.at[g][rows, :] = lse

            kk = jnp.concatenate([kp_ref[:, cols], k_ref[0:BAND, cols]], axis=0)
            vv = jnp.concatenate([vp_ref[:, cols], v_ref[0:BAND, cols]], axis=0)
            o, lse = _band_block(q_ref[0:BAND, cols], kk, vv, bias_first)
            store(0, o, lse)

            if n_blk > 1:
                def body(i, carry, q_ref=q_ref, k_ref=k_ref, v_ref=v_ref, cols=cols, bias=bias, store=store):
                    lo = pl.multiple_of(i * BAND - BAND, BAND)
                    qs = pl.multiple_of(i * BAND, BAND)
                    o, lse = _band_block(q_ref[pl.ds(qs, BAND), cols],
                                         k_ref[pl.ds(lo, 2 * BAND), cols],
                                         v_ref[pl.ds(lo, 2 * BAND), cols], bias)
                    store(i, o, lse)
                    return carry
                lax.fori_loop(1, n_blk, body, 0)

    l0, l1, l2 = l_scr[0], l_scr[1], l_scr[2]
    mx = jnp.maximum(jnp.maximum(l0, l1), l2)
    e0, e1, e2 = jnp.exp2(l0 - mx), jnp.exp2(l1 - mx), jnp.exp2(l2 - mx)
    tot = e0 + e1 + e2
    y = (e0 / tot) * o_scr[0] + (e1 / tot) * o_scr[1] + (e2 / tot) * o_scr[2]
    y_ref[...] = y.astype(BF16)


def _dilated(a_views, slopes):
    t = a_views[0].shape[1]
    dh = HEAD_DIM
    nb = t // DIL_TOKENS
    hq, hk, hv = 0, HPB, 2 * HPB
    args, specs = [], []
    for g, (_, r) in enumerate(DIL_PATTERNS):
        rows = DIL_TOKENS // r
        prev_per_blk = rows // BAND
        width = r * dh

        def cur(base, rows=rows, width=width):
            return pl.BlockSpec((None, rows, width), lambda j, b: (base + j, b, 0))

        def prev(base, width=width, ppb=prev_per_blk):
            return pl.BlockSpec((None, BAND, width),
                                lambda j, b: (base + j, jnp.maximum(b * ppb - 1, 0), 0))

        args += [a_views[g]] * 5
        specs += [cur(hq), cur(hk), cur(hv), prev(hk), prev(hv)]

    return pl.pallas_call(
        _dilated_kernel,
        grid=(HEADS_PER_DIL_GROUP, nb),
        in_specs=[pl.BlockSpec(memory_space=pltpu.SMEM)] + specs,
        out_specs=pl.BlockSpec((DIL_TOKENS, dh), lambda j, b: (b, j)),
        out_shape=jax.ShapeDtypeStruct((t, HEADS_PER_DIL_GROUP * dh), BF16),
        scratch_shapes=[pltpu.VMEM((N_DIL, DIL_TOKENS, dh), F32),
                        pltpu.VMEM((N_DIL, DIL_TOKENS, LANES), F32)],
        compiler_params=_params("arbitrary", "arbitrary"),
        name="dilated",
    )(slopes, *args)


MOBA_UNROLL = 4


MOBA_HEADS_PER_STEP = 4


def _moba_kernel(slopes_ref, q_ref, k_ref, vt_ref, y_ref, kmean_scr, sel_scr, acc_scr, s_scr, *, n_blocks):
    hg = pl.program_id(0)
    qb = pl.program_id(1)
    blk = MOBA_BLOCK
    n_h, _, dh = q_ref.shape
    heads = range(n_h)
    slope = [slopes_ref[N_HEADS_A + hg * n_h + hh] for hh in heads]

    @pl.when(qb == 0)
    def _():
        def body(n, carry):
            for hh in heads:
                kn = k_ref[hh, pl.ds(pl.multiple_of(n * blk, blk), blk), :].astype(F32)
                kmean_scr[hh, pl.ds(n, 1), :] = jnp.mean(kn, axis=0, keepdims=True)
            return carry
        lax.fori_loop(0, n_blocks, body, 0)

    q = [q_ref[hh] for hh in heads]

    blk_id = lax.broadcasted_iota(jnp.int32, (n_blocks, blk), 0)
    blk_id_f = blk_id.astype(F32)
    past = blk_id < qb
    key_i = lax.broadcasted_iota(jnp.int32, (blk, blk), 0)
    qry_i = lax.broadcasted_iota(jnp.int32, (blk, blk), 1)
    own = pl.multiple_of(qb * blk, blk)

    in_block_max = [slope[hh] * (blk - 1) for hh in heads]

    m0 = []
    for hh in heads:
        gate = lax.dot_general(kmean_scr[hh], q[hh].astype(F32), _NT,
                               precision=lax.Precision.HIGHEST, preferred_element_type=F32)
        work = jnp.where(past, gate, NEG_INF)
        sel = jnp.zeros((n_blocks, blk), F32)
        for _ in range(MOBA_TOPK):
            mx = jnp.max(work, axis=0, keepdims=True)
            first = jnp.min(jnp.where(work == mx, blk_id_f, float(n_blocks)), axis=0, keepdims=True)
            hit = blk_id_f == first
            sel = jnp.where(hit, 1.0, sel)
            work = jnp.where(hit, -jnp.inf, work)
        sel_scr[hh] = jnp.where(past, sel, 0.0)

        s = lax.dot_general(k_ref[hh, pl.ds(own, blk), :], q[hh], _NT, preferred_element_type=F32)
        s = jnp.where(key_i <= qry_i, s, NEG_INF)
        m0.append(jnp.max(s, axis=0, keepdims=True) + in_block_max[hh])
        p = jnp.exp2(s - m0[hh])
        acc_scr[hh] = jnp.dot(vt_ref[hh, qb], p.astype(BF16), preferred_element_type=F32)

    width = MOBA_UNROLL
    n_chunks = (qb + width - 1) // width

    def chunk_scores(hh, ci, slot):
        start = pl.multiple_of(ci * (width * blk), width * blk)
        s_scr[slot, hh] = lax.dot_general(k_ref[hh, pl.ds(start, width * blk), :], q[hh], _NT,
                                          preferred_element_type=F32)

    def past_chunk(ci, m_run, src, dst):
        n0 = ci * width
        nxt = jnp.minimum(ci + 1, n_chunks - 1)
        out = []
        for hh in heads:
            chunk_scores(hh, nxt, dst)
            c, chosen = [], []
            m_chunk = jnp.full((1, blk), NEG_INF, F32)
            for a in range(width):
                s = s_scr[src, hh, a * blk:(a + 1) * blk, :]
                c.append(slope[hh] * jnp.full((1, blk), (n0 + a - qb) * blk, jnp.int32).astype(F32))
                chosen.append(sel_scr[hh, pl.ds(n0 + a, 1), :] > 0.5)
                m_blk = jnp.max(s, axis=0, keepdims=True) + (c[a] + in_block_max[hh])
                m_chunk = jnp.maximum(m_chunk, jnp.where(chosen[a], m_blk, NEG_INF))
            pv = jnp.zeros(acc_scr.shape[1:], F32)
            for a in range(width):
                s = s_scr[src, hh, a * blk:(a + 1) * blk, :]
                p = jnp.exp2(s - jnp.where(chosen[a], m_chunk - c[a], -NEG_INF))
                pv = pv + jnp.dot(vt_ref[hh, n0 + a], p.astype(BF16), preferred_element_type=F32)
            m_new = jnp.maximum(m_run[hh], m_chunk)
            alpha = jnp.exp2(m_run[hh] - m_new)
            beta = jnp.exp2(m_chunk - m_new)
            acc_scr[hh] = alpha * acc_scr[hh] + beta * pv
            out.append(m_new)
        return tuple(out)

    for hh in heads:
        chunk_scores(hh, 0, 0)

    def body(ci, m_run):
        return lax.cond(lax.rem(ci, 2) == 0,
                        lambda m: past_chunk(ci, m, 0, 1), lambda m: past_chunk(ci, m, 1, 0), m_run)

    lax.fori_loop(0, n_chunks, body, tuple(m0))
    for hh in heads:
        y = acc_scr[hh, 0:dh, :] / acc_scr[hh, dh:dh + 1, :]
        y_ref[:, hh * dh:(hh + 1) * dh] = y.T.astype(BF16)


def _moba(bqk, bvt, slopes):
    _, t, dh = bqk.shape
    n_blocks = t // MOBA_BLOCK
    assert n_blocks % MOBA_UNROLL == 0, "the padded last chunk must stay inside the key array"
    n_h = MOBA_HEADS_PER_STEP
    n_groups = N_HEADS_B // n_h
    return pl.pallas_call(
        functools.partial(_moba_kernel, n_blocks=n_blocks),
        grid=(n_groups, n_blocks),
        in_specs=[pl.BlockSpec(memory_space=pltpu.SMEM),
                  pl.BlockSpec((n_h, MOBA_BLOCK, dh), lambda h, i: (h, i, 0)),
                  pl.BlockSpec((n_h, t, dh), lambda h, i: (n_groups + h, 0, 0)),
                  pl.BlockSpec((n_h, n_blocks, MOBA_VT_ROWS, MOBA_BLOCK), lambda h, i: (h, 0, 0, 0))],
        out_specs=pl.BlockSpec((MOBA_BLOCK, n_h * dh), lambda h, i: (i, h)),
        out_shape=jax.ShapeDtypeStruct((t, N_HEADS_B * dh), BF16),
        scratch_shapes=[pltpu.VMEM((n_h, n_blocks, dh), F32),
                        pltpu.VMEM((n_h, n_blocks, MOBA_BLOCK), F32),
                        pltpu.VMEM((n_h, MOBA_VT_ROWS, MOBA_BLOCK), F32),
                        pltpu.VMEM((2, n_h, MOBA_UNROLL * MOBA_BLOCK, MOBA_BLOCK), F32)],
        compiler_params=_params("arbitrary", "arbitrary"),
        name="moba",
    )(slopes, bqk, bqk, bvt)


def _merge_kernel(x_ref, ada_ref, ya_ref, yb_ref, gates_a_ref, gates_b_ref, wa_ref, wb_ref, wo_ref, o_ref):
    pa = jnp.dot(ya_ref[...], wa_ref[...], preferred_element_type=F32)
    pb = jnp.dot(yb_ref[...], wb_ref[...], preferred_element_type=F32)
    merged = gates_a_ref[...].astype(F32) * pa + gates_b_ref[...].astype(F32) * pb
    out = jnp.dot(merged.astype(BF16), wo_ref[...], preferred_element_type=F32)
    o_ref[...] = x_ref[...] + ada_ref[5:6, :] * out


def _merge(x, ada, y_a, y_b, gates, w_a, w_b, w_o, *, tm=256):
    t, d = x.shape
    const = lambda i: (0, 0)
    return pl.pallas_call(
        _merge_kernel,
        grid=(t // tm,),
        in_specs=[pl.BlockSpec((tm, d), lambda i: (i, 0)),
                  pl.BlockSpec((N_ADA, d), const),
                  pl.BlockSpec((tm, y_a.shape[1]), lambda i: (i, 0)),
                  pl.BlockSpec((tm, y_b.shape[1]), lambda i: (i, 0)),
                  pl.BlockSpec((tm, d), lambda i: (i, 0)),
                  pl.BlockSpec((tm, d), lambda i: (i, 1)),
                  pl.BlockSpec(w_a.shape, const),
                  pl.BlockSpec(w_b.shape, const),
                  pl.BlockSpec(w_o.shape, const)],
        out_specs=pl.BlockSpec((tm, d), lambda i: (i, 0)),
        out_shape=jax.ShapeDtypeStruct((t, d), F32),
        compiler_params=_params("parallel"),
        name="merge",
    )(x, ada, y_a, y_b, gates, gates, w_a, w_b, w_o)


def _layer(x, c, w_ada, b_ada, g_ffn1, ffn1_w_gate, ffn1_w_up, ffn1_w_down,
           g_mix, w_in, q_norm, k_norm, w_gate, w_branch_a, w_branch_b, w_out,
           g_ffn2, ffn2_w_gate, ffn2_w_up, ffn2_w_down):
    t, d = x.shape
    ada = _ada(c, w_ada, b_ada).reshape(N_ADA, d)
    slopes = jnp.exp2(-8.0 * jnp.arange(1, N_HEADS + 1, dtype=F32) / N_HEADS) * LOG2_E

    x = _ffn(x, ada, g_ffn1, ffn1_w_gate, ffn1_w_up, ffn1_w_down, sub=0)

    scale = HEAD_DIM ** -0.5 * LOG2_E
    n_gate_heads = w_gate.shape[1] // HEAD_DIM
    gains = jnp.concatenate([q_norm[:N_HEADS_A] * scale, k_norm[:N_HEADS_A], jnp.ones((N_HEADS_A, HEAD_DIM), F32),
                             q_norm[N_HEADS_A:] * scale, k_norm[N_HEADS_A:],
                             jnp.ones((N_HEADS_B + n_gate_heads, HEAD_DIM), F32)], axis=0)
    norm_flags = jnp.concatenate([jnp.ones((2 * N_DIL,), F32), jnp.zeros((N_DIL,), F32),
                                  jnp.ones((BLK_BV0 - BLK_BQK0,), F32),
                                  jnp.zeros((BLK_GATE0 - BLK_BV0 + n_gate_heads // HPB,), F32)])
    offsets = jnp.arange(MOBA_BLOCK, dtype=F32)
    key_weights = jnp.exp2(slopes[N_HEADS_A:, None, None] * offsets[None, None, :])
    key_weights = jnp.pad(key_weights, ((0, 0), (0, MOBA_VT_PAD - 1), (0, 0)))
    a0, a1, a2, bqk, bvt, gates = _proj(x, ada, g_mix, w_in, w_gate, gains, norm_flags, key_weights)

    y_a = _dilated((a0, a1, a2), slopes)
    y_b = _moba(bqk, bvt, slopes)

    x = _merge(x, ada, y_a, y_b, gates, w_branch_a.astype(BF16), w_branch_b.astype(BF16), w_out.astype(BF16))
    x = _ffn(x, ada, g_ffn2, ffn2_w_gate, ffn2_w_up, ffn2_w_down, sub=2)
    return x


def kernel(x, c, w_ada, b_ada, g_ffn1, ffn1_w_gate, ffn1_w_up, ffn1_w_down, g_mix, w_in, q_norm, k_norm,
           w_gate, w_branch_a, w_branch_b, w_out, g_ffn2, ffn2_w_gate, ffn2_w_up, ffn2_w_down):
    batch, depth = x.shape[0], w_ada.shape[0]
    outs = []
    for bi in range(batch):
        xb = x[bi]
        for l in range(depth):
            xb = _layer(xb, c[bi], w_ada[l], b_ada[l], g_ffn1[l], ffn1_w_gate[l], ffn1_w_up[l],
                        ffn1_w_down[l], g_mix[l], w_in[l], q_norm[l], k_norm[l], w_gate[l],
                        w_branch_a[l], w_branch_b[l], w_out[l], g_ffn2[l], ffn2_w_gate[l],
                        ffn2_w_up[l], ffn2_w_down[l])
        outs.append(xb)
    return jnp.stack(outs, axis=0)
```

```python
import functools

import jax
import jax.numpy as jnp
from jax import lax
from jax.experimental import pallas as pl
from jax.experimental.pallas import tpu as pltpu

HEAD_DIM = 128
DIL_PATTERNS = ((128, 1), (512, 4), (2048, 16))
N_DIL = len(DIL_PATTERNS)
HEADS_PER_DIL_GROUP = 4
N_HEADS_A = HEADS_PER_DIL_GROUP * N_DIL
N_HEADS_B = 8
N_HEADS = N_HEADS_A + N_HEADS_B
BAND = 128
MOBA_BLOCK = 256
MOBA_TOPK = 3
N_ADA = 9
EPS = 1e-6
NEG_INF = -1e30
LOG2_E = 1.4426950408889634

LANES = 128
VMEM_LIMIT = 60 * 1024 * 1024

BF16 = jnp.bfloat16
F32 = jnp.float32

_NT = (((1,), (1,)), ((), ()))


def _params(*sem):
    return pltpu.CompilerParams(dimension_semantics=sem, vmem_limit_bytes=VMEM_LIMIT)


def _sigmoid(x):
    return 1.0 / (1.0 + jnp.exp(-x))


def _norm_modulate(x, g, shift, scale):
    ms = jnp.mean(x * x, axis=-1, keepdims=True)
    y = (x * lax.rsqrt(ms + EPS)) * g
    return y * (1.0 + scale) + shift


def _head_rmsnorm(a, gain):
    ms = jnp.mean(a * a, axis=-1, keepdims=True)
    return (a * lax.rsqrt(ms + EPS)) * gain


def _ada_kernel(c_ref, w_ref, b_ref, o_ref):
    c = c_ref[...]
    s = c * _sigmoid(c)
    o_ref[...] = jnp.sum(s * w_ref[...], axis=0, keepdims=True) + b_ref[...]


def _ada(c, w_ada, b_ada, *, tn=1024):
    d, n = w_ada.shape
    return pl.pallas_call(
        _ada_kernel,
        grid=(n // tn,),
        in_specs=[pl.BlockSpec((d, 1), lambda j: (0, 0)),
                  pl.BlockSpec((d, tn), lambda j: (0, j)),
                  pl.BlockSpec((1, tn), lambda j: (0, j))],
        out_specs=pl.BlockSpec((1, tn), lambda j: (0, j)),
        out_shape=jax.ShapeDtypeStruct((1, n), F32),
        compiler_params=_params("arbitrary"),
        name="ada",
    )(c.reshape(d, 1), w_ada, b_ada.reshape(1, n))


DMA_SPLIT = 1


def _row_slab_specs(rows, cols, index_map, **kw):
    def slab(p):
        def imap(*idx):
            r, c = index_map(*idx)
            return r * DMA_SPLIT + p, c
        return pl.BlockSpec((rows // DMA_SPLIT, cols), imap, **kw)
    return [slab(p) for p in range(DMA_SPLIT)]


def _bf16_rows(refs):
    return jnp.concatenate([r[...].astype(BF16) for r in refs], axis=0)


def _ffn_kernel(*refs, sub):
    n = DMA_SPLIT
    x_refs, (ada_ref, g_ref) = refs[0:n], refs[n:n + 2]
    wg_refs, wu_refs, wd_refs = refs[n + 2:2 * n + 2], refs[2 * n + 2:3 * n + 2], refs[3 * n + 2:4 * n + 2]
    o_ref, u_ref = refs[4 * n + 2:]
    f = pl.program_id(1)
    slab = x_refs[0].shape[0]

    @pl.when(f == 0)
    def _():
        for p, x_ref in enumerate(x_refs):
            u = _norm_modulate(x_ref[...], g_ref[...],
                               ada_ref[3 * sub:3 * sub + 1, :], ada_ref[3 * sub + 1:3 * sub + 2, :])
            u_ref[p * slab:(p + 1) * slab, :] = u.astype(BF16)
        o_ref[...] = jnp.zeros_like(o_ref)

    u = u_ref[...]
    hg = jnp.dot(u, _bf16_rows(wg_refs), preferred_element_type=F32)
    hu = jnp.dot(u, _bf16_rows(wu_refs), preferred_element_type=F32)
    h = (hg * _sigmoid(hg)) * hu
    o_ref[...] += jnp.dot(h.astype(BF16), _bf16_rows(wd_refs), preferred_element_type=F32)

    @pl.when(f == pl.num_programs(1) - 1)
    def _():
        gate = ada_ref[3 * sub + 2:3 * sub + 3, :]
        for p, x_ref in enumerate(x_refs):
            rows = slice(p * slab, (p + 1) * slab)
            o_ref[rows, :] = x_ref[...] + (0.5 * gate) * o_ref[rows, :]


def _ffn(x, ada, g, w_gate, w_up, w_down, *, sub, tm=1024, tf=256):
    t, d = x.shape
    dff = w_gate.shape[1]
    n = DMA_SPLIT
    return pl.pallas_call(
        functools.partial(_ffn_kernel, sub=sub),
        grid=(t // tm, dff // tf),
        in_specs=(_row_slab_specs(tm, d, lambda i, f: (i, 0))
                  + [pl.BlockSpec((N_ADA, d), lambda i, f: (0, 0)),
                     pl.BlockSpec((1, d), lambda i, f: (0, 0))]
                  + _row_slab_specs(d, tf, lambda i, f: (0, f))
                  + _row_slab_specs(d, tf, lambda i, f: (0, f))
                  + _row_slab_specs(tf, d, lambda i, f: (f, 0))),
        out_specs=pl.BlockSpec((tm, d), lambda i, f: (i, 0)),
        out_shape=jax.ShapeDtypeStruct((t, d), F32),
        scratch_shapes=[pltpu.VMEM((tm, d), BF16)],
        compiler_params=_params("parallel", "arbitrary"),
        name=f"ffn{sub}",
    )(*([x] * n), ada, g.reshape(1, d), *([w_gate] * n), *([w_up] * n), *([w_down] * n))


HPB = HEADS_PER_DIL_GROUP
PROJ_TN = HPB * HEAD_DIM
BLK_A_END = 3 * N_DIL
BLK_BQK0 = BLK_A_END
BLK_BV0 = BLK_BQK0 + 2 * (N_HEADS_B // HPB)
BLK_GATE0 = BLK_BV0 + N_HEADS_B // HPB


MOBA_VT_PAD = 16
MOBA_VT_ROWS = HEAD_DIM + MOBA_VT_PAD


PROJ_ROWS = MOBA_BLOCK


def _proj_kernel(*refs):
    n = DMA_SPLIT
    nflag_ref, x_refs, (ada_ref, g_ref) = refs[0], refs[1:n + 1], refs[n + 1:n + 3]
    win_refs, wgate_refs = refs[n + 3:2 * n + 3], refs[2 * n + 3:3 * n + 3]
    (gain_ref, kw_ref, a0_ref, a1_ref, a2_ref, bqk_ref, bvt_ref, gate_ref, u_ref, stg_ref) = refs[3 * n + 3:]
    j = pl.program_id(1)
    slab = x_refs[0].shape[0]
    tm = slab * n
    n_chunks = tm // PROJ_ROWS

    @pl.when(j == 0)
    def _():
        for p, x_ref in enumerate(x_refs):
            u = _norm_modulate(x_ref[...], g_ref[...], ada_ref[3:4, :], ada_ref[4:5, :])
            u_ref[p * slab:(p + 1) * slab, :] = u.astype(BF16)

    def chunks(w_refs):
        w = _bf16_rows(w_refs)
        for c in range(n_chunks):
            rows = slice(c * PROJ_ROWS, (c + 1) * PROJ_ROWS)
            yield c, jnp.dot(u_ref[rows, :], w, preferred_element_type=F32)

    def head(res, h):
        a = res[:, h * HEAD_DIM:(h + 1) * HEAD_DIM]
        flag = nflag_ref[j]
        ms = jnp.mean(a * a, axis=-1, keepdims=True)
        return (a * (lax.rsqrt(ms + EPS) * flag + (1.0 - flag))) * gain_ref[0, h:h + 1, :]

    def dilated_block(o_ref, r):
        n = PROJ_ROWS // r
        for c, res in chunks(win_refs):
            for h in range(HPB):
                y = head(res, h)
                if r == 1:
                    o_ref[h, c * n:(c + 1) * n, :] = y.astype(BF16)
                else:
                    stg_ref[c % 2, h] = y
                    for rho in range(r):
                        part = stg_ref.at[c % 2, h][pl.ds(rho, n, stride=r), :]
                        o_ref[h, c * n:(c + 1) * n, rho * HEAD_DIM:(rho + 1) * HEAD_DIM] = part.astype(BF16)

    a_refs = (a0_ref, a1_ref, a2_ref)
    for g, (_, r) in enumerate(DIL_PATTERNS):
        pl.when((j < BLK_A_END) & (lax.rem(j, N_DIL) == g))(functools.partial(dilated_block, a_refs[g], r))

    @pl.when((j >= BLK_BQK0) & (j < BLK_BV0))
    def _():
        for c, res in chunks(win_refs):
            for h in range(HPB):
                bqk_ref[h, c * PROJ_ROWS:(c + 1) * PROJ_ROWS, :] = head(res, h).astype(BF16)

    @pl.when((j >= BLK_BV0) & (j < BLK_GATE0))
    def _():
        for c, res in chunks(win_refs):
            for h in range(HPB):
                kw = kw_ref[0, h]
                yt = res[:, h * HEAD_DIM:(h + 1) * HEAD_DIM].T
                bvt_ref[h, c, 0:HEAD_DIM, :] = (yt * kw[0:1, :]).astype(BF16)
                bvt_ref[h, c, HEAD_DIM:, :] = kw.astype(BF16)

    @pl.when(j >= BLK_GATE0)
    def _():
        for c, res in chunks(wgate_refs):
            gate_ref[c * PROJ_ROWS:(c + 1) * PROJ_ROWS, :] = _sigmoid(res).astype(BF16)


def _proj(x, ada, g, w_in, w_gate, gains, norm_flags, key_weights, *, tm=1024):
    t, d = x.shape
    n_gate = w_gate.shape[1] // PROJ_TN
    n_blk = BLK_GATE0 + n_gate
    assert w_in.shape[1] == BLK_GATE0 * PROJ_TN
    n_bv = BLK_GATE0 - BLK_BV0

    def a_spec(gi, r):
        return pl.BlockSpec((HPB, tm // r, r * HEAD_DIM),
                            lambda i, j: (jnp.clip((j - gi + N_DIL - 1) // N_DIL, 0, 2), i, 0))

    a_shapes = [jax.ShapeDtypeStruct((3 * HPB, t // r, r * HEAD_DIM), BF16) for _, r in DIL_PATTERNS]
    out_shape = a_shapes + [
        jax.ShapeDtypeStruct((2 * N_HEADS_B, t, HEAD_DIM), BF16),
        jax.ShapeDtypeStruct((N_HEADS_B, t // MOBA_BLOCK, MOBA_VT_ROWS, MOBA_BLOCK), BF16),
        jax.ShapeDtypeStruct((t, n_gate * PROJ_TN), BF16)]
    out_specs = [a_spec(gi, r) for gi, (_, r) in enumerate(DIL_PATTERNS)] + [
        pl.BlockSpec((HPB, tm, HEAD_DIM),
                     lambda i, j: (jnp.clip(j - BLK_BQK0, 0, BLK_BV0 - BLK_BQK0 - 1), i, 0)),
        pl.BlockSpec((HPB, tm // MOBA_BLOCK, MOBA_VT_ROWS, MOBA_BLOCK),
                     lambda i, j: (jnp.clip(j - BLK_BV0, 0, n_bv - 1), i, 0, 0)),
        pl.BlockSpec((tm, PROJ_TN), lambda i, j: (i, jnp.clip(j - BLK_GATE0, 0, n_gate - 1)))]
    return pl.pallas_call(
        _proj_kernel,
        grid=(t // tm, n_blk),
        in_specs=([pl.BlockSpec(memory_space=pltpu.SMEM)]
                  + _row_slab_specs(tm, d, lambda i, j: (i, 0))
                  + [pl.BlockSpec((N_ADA, d), lambda i, j: (0, 0)),
                     pl.BlockSpec((1, d), lambda i, j: (0, 0))]
                  + _row_slab_specs(d, PROJ_TN, lambda i, j: (0, jnp.minimum(j, BLK_GATE0 - 1)))
                  + _row_slab_specs(d, PROJ_TN, lambda i, j: (0, jnp.clip(j - BLK_GATE0, 0, n_gate - 1)))
                  + [pl.BlockSpec((1, HPB, HEAD_DIM), lambda i, j: (j, 0, 0)),
                     pl.BlockSpec((1, HPB, MOBA_VT_PAD, MOBA_BLOCK),
                                  lambda i, j: (jnp.clip(j - BLK_BV0, 0, n_bv - 1), 0, 0, 0))]),
        out_specs=out_specs,
        out_shape=out_shape,
        scratch_shapes=[pltpu.VMEM((tm, d), BF16), pltpu.VMEM((2, HPB, PROJ_ROWS, HEAD_DIM), F32)],
        compiler_params=_params("arbitrary", "arbitrary"),
        name="proj",
    )(norm_flags, *([x] * DMA_SPLIT), ada, g.reshape(1, d), *([w_in] * DMA_SPLIT), *([w_gate] * DMA_SPLIT),
      gains.reshape(n_blk, HPB, HEAD_DIM),
      key_weights.reshape(n_bv, HPB, MOBA_VT_PAD, MOBA_BLOCK))


DIL_TOKENS = 2048


def _band_block(q, k2, v2, bias):
    s = lax.dot_general(q, k2, _NT, preferred_element_type=F32) + bias
    m = jnp.max(s, axis=-1, keepdims=True)
    p = jnp.exp2(s - m)
    denom = jnp.sum(p, axis=-1, keepdims=True)
    o = jnp.dot(p.astype(BF16), v2, preferred_element_type=F32) / denom
    lse = jnp.broadcast_to(m + jnp.log2(denom), (BAND, LANES))
    return o, lse


def _dilated_kernel(slopes_ref,
                    q0, k0, v0, kp0, vp0,
                    q1, k1, v1, kp1, vp1,
                    q2, k2, v2, kp2, vp2,
                    y_ref, o_scr, l_scr):
    j = pl.program_id(0)
    b = pl.program_id(1)

    qi = lax.broadcasted_iota(jnp.int32, (BAND, 2 * BAND), 0)
    ki = lax.broadcasted_iota(jnp.int32, (BAND, 2 * BAND), 1)
    dist = BAND + qi - ki
    in_band = (dist >= 0) & (dist <= BAND)
    dist_f = dist.astype(F32)
    first_ok = (ki + jnp.minimum(b, 1) * BAND) >= BAND

    groups = ((q0, k0, v0, kp0, vp0), (q1, k1, v1, kp1, vp1), (q2, k2, v2, kp2, vp2))
    for g, (_, r) in enumerate(DIL_PATTERNS):
        q_ref, k_ref, v_ref, kp_ref, vp_ref = groups[g]
        slope = slopes_ref[g * HEADS_PER_DIL_GROUP + j]
        bias = jnp.where(in_band, dist_f * (-slope * r), NEG_INF)
        bias_first = jnp.where(first_ok, bias, NEG_INF)
        n_blk = DIL_TOKENS // (r * BAND)
        for rho in range(r):
            cols = slice(rho * HEAD_DIM, (rho + 1) * HEAD_DIM)

            def store(i, o, lse, g=g, r=r, rho=rho):
                rows = pl.ds(i * (BAND * r) + rho, BAND, stride=r) if r > 1 else pl.ds(i * BAND, BAND)
                o_scr.at[g][rows, :] = o
                l_scr.at[g][rows, :] = lse

            kk = jnp.concatenate([kp_ref[:, cols], k_ref[0:BAND, cols]], axis=0)
            vv = jnp.concatenate([vp_ref[:, cols], v_ref[0:BAND, cols]], axis=0)
            o, lse = _band_block(q_ref[0:BAND, cols], kk, vv, bias_first)
            store(0, o, lse)

            if n_blk > 1:
                def body(i, carry, q_ref=q_ref, k_ref=k_ref, v_ref=v_ref, cols=cols, bias=bias, store=store):
                    lo = pl.multiple_of(i * BAND - BAND, BAND)
                    qs = pl.multiple_of(i * BAND, BAND)
                    o, lse = _band_block(q_ref[pl.ds(qs, BAND), cols],
                                         k_ref[pl.ds(lo, 2 * BAND), cols],
                                         v_ref[pl.ds(lo, 2 * BAND), cols], bias)
                    store(i, o, lse)
                    return carry
                lax.fori_loop(1, n_blk, body, 0)

    l0, l1, l2 = l_scr[0], l_scr[1], l_scr[2]
    mx = jnp.maximum(jnp.maximum(l0, l1), l2)
    e0, e1, e2 = jnp.exp2(l0 - mx), jnp.exp2(l1 - mx), jnp.exp2(l2 - mx)
    tot = e0 + e1 + e2
    y = (e0 / tot) * o_scr[0] + (e1 / tot) * o_scr[1] + (e2 / tot) * o_scr[2]
    y_ref[...] = y.astype(BF16)


def _dilated(a_views, slopes):
    t = a_views[0].shape[1]
    dh = HEAD_DIM
    nb = t // DIL_TOKENS
    hq, hk, hv = 0, HPB, 2 * HPB
    args, specs = [], []
    for g, (_, r) in enumerate(DIL_PATTERNS):
        rows = DIL_TOKENS // r
        prev_per_blk = rows // BAND
        width = r * dh

        def cur(base, rows=rows, width=width):
            return pl.BlockSpec((None, rows, width), lambda j, b: (base + j, b, 0))

        def prev(base, width=width, ppb=prev_per_blk):
            return pl.BlockSpec((None, BAND, width),
                                lambda j, b: (base + j, jnp.maximum(b * ppb - 1, 0), 0))

        args += [a_views[g]] * 5
        specs += [cur(hq), cur(hk), cur(hv), prev(hk), prev(hv)]

    return pl.pallas_call(
        _dilated_kernel,
        grid=(HEADS_PER_DIL_GROUP, nb),
        in_specs=[pl.BlockSpec(memory_space=pltpu.SMEM)] + specs,
        out_specs=pl.BlockSpec((DIL_TOKENS, dh), lambda j, b: (b, j)),
        out_shape=jax.ShapeDtypeStruct((t, HEADS_PER_DIL_GROUP * dh), BF16),
        scratch_shapes=[pltpu.VMEM((N_DIL, DIL_TOKENS, dh), F32),
                        pltpu.VMEM((N_DIL, DIL_TOKENS, LANES), F32)],
        compiler_params=_params("arbitrary", "arbitrary"),
        name="dilated",
    )(slopes, *args)


MOBA_UNROLL = 4


MOBA_HEADS_PER_STEP = 4


def _moba_kernel(slopes_ref, q_ref, k_ref, vt_ref, y_ref, kmean_scr, sel_scr, acc_scr, s_scr, *, n_blocks):
    hg = pl.program_id(0)
    qb = pl.program_id(1)
    blk = MOBA_BLOCK
    n_h, _, dh = q_ref.shape
    heads = range(n_h)
    slope = [slopes_ref[N_HEADS_A + hg * n_h + hh] for hh in heads]

    @pl.when(qb == 0)
    def _():
        def body(n, carry):
            for hh in heads:
                kn = k_ref[hh, pl.ds(pl.multiple_of(n * blk, blk), blk), :].astype(F32)
                kmean_scr[hh, pl.ds(n, 1), :] = jnp.mean(kn, axis=0, keepdims=True)
            return carry
        lax.fori_loop(0, n_blocks, body, 0)

    q = [q_ref[hh] for hh in heads]

    blk_id = lax.broadcasted_iota(jnp.int32, (n_blocks, blk), 0)
    blk_id_f = blk_id.astype(F32)
    past = blk_id < qb
    key_i = lax.broadcasted_iota(jnp.int32, (blk, blk), 0)
    qry_i = lax.broadcasted_iota(jnp.int32, (blk, blk), 1)
    own = pl.multiple_of(qb * blk, blk)

    in_block_max = [slope[hh] * (blk - 1) for hh in heads]

    m0 = []
    for hh in heads:
        gate = lax.dot_general(kmean_scr[hh], q[hh].astype(F32), _NT,
                               precision=lax.Precision.HIGHEST, preferred_element_type=F32)
        work = jnp.where(past, gate, NEG_INF)
        sel = jnp.zeros((n_blocks, blk), F32)
        for _ in range(MOBA_TOPK):
            mx = jnp.max(work, axis=0, keepdims=True)
            first = jnp.min(jnp.where(work == mx, blk_id_f, float(n_blocks)), axis=0, keepdims=True)
            hit = blk_id_f == first
            sel = jnp.where(hit, 1.0, sel)
            work = jnp.where(hit, -jnp.inf, work)
        sel_scr[hh] = jnp.where(past, sel, 0.0)

        s = lax.dot_general(k_ref[hh, pl.ds(own, blk), :], q[hh], _NT, preferred_element_type=F32)
        s = jnp.where(key_i <= qry_i, s, NEG_INF)
        m0.append(jnp.max(s, axis=0, keepdims=True) + in_block_max[hh])
        p = jnp.exp2(s - m0[hh])
        acc_scr[hh] = jnp.dot(vt_ref[hh, qb], p.astype(BF16), preferred_element_type=F32)

    width = MOBA_UNROLL
    n_chunks = (qb + width - 1) // width

    def chunk_scores(hh, ci, slot):
        start = pl.multiple_of(ci * (width * blk), width * blk)
        s_scr[slot, hh] = lax.dot_general(k_ref[hh, pl.ds(start, width * blk), :], q[hh], _NT,
                                          preferred_element_type=F32)

    def past_chunk(ci, m_run, src, dst):
        n0 = ci * width
        nxt = jnp.minimum(ci + 1, n_chunks - 1)
        out = []
        for hh in heads:
            chunk_scores(hh, nxt, dst)
            c, chosen = [], []
            m_chunk = jnp.full((1, blk), NEG_INF, F32)
            for a in range(width):
                s = s_scr[src, hh, a * blk:(a + 1) * blk, :]
                c.append(slope[hh] * jnp.full((1, blk), (n0 + a - qb) * blk, jnp.int32).astype(F32))
                chosen.append(sel_scr[hh, pl.ds(n0 + a, 1), :] > 0.5)
                m_blk = jnp.max(s, axis=0, keepdims=True) + (c[a] + in_block_max[hh])
                m_chunk = jnp.maximum(m_chunk, jnp.where(chosen[a], m_blk, NEG_INF))
            pv = jnp.zeros(acc_scr.shape[1:], F32)
            for a in range(width):
                s = s_scr[src, hh, a * blk:(a + 1) * blk, :]
                p = jnp.exp2(s - jnp.where(chosen[a], m_chunk - c[a], -NEG_INF))
                pv = pv + jnp.dot(vt_ref[hh, n0 + a], p.astype(BF16), preferred_element_type=F32)
            m_new = jnp.maximum(m_run[hh], m_chunk)
            alpha = jnp.exp2(m_run[hh] - m_new)
            beta = jnp.exp2(m_chunk - m_new)
            acc_scr[hh] = alpha * acc_scr[hh] + beta * pv
            out.append(m_new)
        return tuple(out)

    for hh in heads:
        chunk_scores(hh, 0, 0)

    def body(ci, m_run):
        return lax.cond(lax.rem(ci, 2) == 0,
                        lambda m: past_chunk(ci, m, 0, 1), lambda m: past_chunk(ci, m, 1, 0), m_run)

    lax.fori_loop(0, n_chunks, body, tuple(m0))
    for hh in heads:
        y = acc_scr[hh, 0:dh, :] / acc_scr[hh, dh:dh + 1, :]
        y_ref[:, hh * dh:(hh + 1) * dh] = y.T.astype(BF16)


def _moba(bqk, bvt, slopes):
    _, t, dh = bqk.shape
    n_blocks = t // MOBA_BLOCK
    assert n_blocks % MOBA_UNROLL == 0, "the padded last chunk must stay inside the key array"
    n_h = MOBA_HEADS_PER_STEP
    n_groups = N_HEADS_B // n_h
    return pl.pallas_call(
        functools.partial(_moba_kernel, n_blocks=n_blocks),
        grid=(n_groups, n_blocks),
        in_specs=[pl.BlockSpec(memory_space=pltpu.SMEM),
                  pl.BlockSpec((n_h, MOBA_BLOCK, dh), lambda h, i: (h, i, 0)),
                  pl.BlockSpec((n_h, t, dh), lambda h, i: (n_groups + h, 0, 0)),
                  pl.BlockSpec((n_h, n_blocks, MOBA_VT_ROWS, MOBA_BLOCK), lambda h, i: (h, 0, 0, 0))],
        out_specs=pl.BlockSpec((MOBA_BLOCK, n_h * dh), lambda h, i: (i, h)),
        out_shape=jax.ShapeDtypeStruct((t, N_HEADS_B * dh), BF16),
        scratch_shapes=[pltpu.VMEM((n_h, n_blocks, dh), F32),
                        pltpu.VMEM((n_h, n_blocks, MOBA_BLOCK), F32),
                        pltpu.VMEM((n_h, MOBA_VT_ROWS, MOBA_BLOCK), F32),
                        pltpu.VMEM((2, n_h, MOBA_UNROLL * MOBA_BLOCK, MOBA_BLOCK), F32)],
        compiler_params=_params("arbitrary", "arbitrary"),
        name="moba",
    )(slopes, bqk, bqk, bvt)


def _merge_kernel(x_ref, ada_ref, ya_ref, yb_ref, gates_a_ref, gates_b_ref, wa_ref, wb_ref, wo_ref, o_ref):
    pa = jnp.dot(ya_ref[...], wa_ref[...], preferred_element_type=F32)
    pb = jnp.dot(yb_ref[...], wb_ref[...], preferred_element_type=F32)
    merged = gates_a_ref[...].astype(F32) * pa + gates_b_ref[...].astype(F32) * pb
    out = jnp.dot(merged.astype(BF16), wo_ref[...], preferred_element_type=F32)
    o_ref[...] = x_ref[...] + ada_ref[5:6, :] * out


def _merge(x, ada, y_a, y_b, gates, w_a, w_b, w_o, *, tm=256):
    t, d = x.shape
    const = lambda i: (0, 0)
    return pl.pallas_call(
        _merge_kernel,
        grid=(t // tm,),
        in_specs=[pl.BlockSpec((tm, d), lambda i: (i, 0)),
                  pl.BlockSpec((N_ADA, d), const),
                  pl.BlockSpec((tm, y_a.shape[1]), lambda i: (i, 0)),
                  pl.BlockSpec((tm, y_b.shape[1]), lambda i: (i, 0)),
                  pl.BlockSpec((tm, d), lambda i: (i, 0)),
                  pl.BlockSpec((tm, d), lambda i: (i, 1)),
                  pl.BlockSpec(w_a.shape, const),
                  pl.BlockSpec(w_b.shape, const),
                  pl.BlockSpec(w_o.shape, const)],
        out_specs=pl.BlockSpec((tm, d), lambda i: (i, 0)),
        out_shape=jax.ShapeDtypeStruct((t, d), F32),
        compiler_params=_params("parallel"),
        name="merge",
    )(x, ada, y_a, y_b, gates, gates, w_a, w_b, w_o)


def _layer(x, c, w_ada, b_ada, g_ffn1, ffn1_w_gate, ffn1_w_up, ffn1_w_down,
           g_mix, w_in, q_norm, k_norm, w_gate, w_branch_a, w_branch_b, w_out,
           g_ffn2, ffn2_w_gate, ffn2_w_up, ffn2_w_down):
    t, d = x.shape
    ada = _ada(c, w_ada, b_ada).reshape(N_ADA, d)
    slopes = jnp.exp2(-8.0 * jnp.arange(1, N_HEADS + 1, dtype=F32) / N_HEADS) * LOG2_E

    x = _ffn(x, ada, g_ffn1, ffn1_w_gate, ffn1_w_up, ffn1_w_down, sub=0)

    scale = HEAD_DIM ** -0.5 * LOG2_E
    n_gate_heads = w_gate.shape[1] // HEAD_DIM
    gains = jnp.concatenate([q_norm[:N_HEADS_A] * scale, k_norm[:N_HEADS_A], jnp.ones((N_HEADS_A, HEAD_DIM), F32),
                             q_norm[N_HEADS_A:] * scale, k_norm[N_HEADS_A:],
                             jnp.ones((N_HEADS_B + n_gate_heads, HEAD_DIM), F32)], axis=0)
    norm_flags = jnp.concatenate([jnp.ones((2 * N_DIL,), F32), jnp.zeros((N_DIL,), F32),
                                  jnp.ones((BLK_BV0 - BLK_BQK0,), F32),
                                  jnp.zeros((BLK_GATE0 - BLK_BV0 + n_gate_heads // HPB,), F32)])
    offsets = jnp.arange(MOBA_BLOCK, dtype=F32)
    key_weights = jnp.exp2(slopes[N_HEADS_A:, None, None] * offsets[None, None, :])
    key_weights = jnp.pad(key_weights, ((0, 0), (0, MOBA_VT_PAD - 1), (0, 0)))
    a0, a1, a2, bqk, bvt, gates = _proj(x, ada, g_mix, w_in, w_gate, gains, norm_flags, key_weights)

    y_a = _dilated((a0, a1, a2), slopes)
    y_b = _moba(bqk, bvt, slopes)

    x = _merge(x, ada, y_a, y_b, gates, w_branch_a.astype(BF16), w_branch_b.astype(BF16), w_out.astype(BF16))
    x = _ffn(x, ada, g_ffn2, ffn2_w_gate, ffn2_w_up, ffn2_w_down, sub=2)
    return x


def kernel(x, c, w_ada, b_ada, g_ffn1, ffn1_w_gate, ffn1_w_up, ffn1_w_down, g_mix, w_in, q_norm, k_norm,
           w_gate, w_branch_a, w_branch_b, w_out, g_ffn2, ffn2_w_gate, ffn2_w_up, ffn2_w_down):
    batch, depth = x.shape[0], w_ada.shape[0]
    outs = []
    for bi in range(batch):
        xb = x[bi]
        for l in range(depth):
            xb = _layer(xb, c[bi], w_ada[l], b_ada[l], g_ffn1[l], ffn1_w_gate[l], ffn1_w_up[l],
                        ffn1_w_down[l], g_mix[l], w_in[l], q_norm[l], k_norm[l], w_gate[l],
                        w_branch_a[l], w_branch_b[l], w_out[l], g_ffn2[l], ffn2_w_gate[l],
                        ffn2_w_up[l], ffn2_w_down[l])
        outs.append(xb)
    return jnp.stack(outs, axis=0)
```

```python
import functools

import jax
import jax.numpy as jnp
from jax import lax
from jax.experimental import pallas as pl
from jax.experimental.pallas import tpu as pltpu

HEAD_DIM = 128
DIL_PATTERNS = ((128, 1), (512, 4), (2048, 16))
N_DIL = len(DIL_PATTERNS)
HEADS_PER_DIL_GROUP = 4
N_HEADS_A = HEADS_PER_DIL_GROUP * N_DIL
N_HEADS_B = 8
N_HEADS = N_HEADS_A + N_HEADS_B
BAND = 128
MOBA_BLOCK = 256
MOBA_TOPK = 3
N_ADA = 9
EPS = 1e-6
NEG_INF = -1e30
LOG2_E = 1.4426950408889634

LANES = 128
VMEM_LIMIT = 60 * 1024 * 1024

BF16 = jnp.bfloat16
F32 = jnp.float32

_NT = (((1,), (1,)), ((), ()))


def _params(*sem):
    return pltpu.CompilerParams(dimension_semantics=sem, vmem_limit_bytes=VMEM_LIMIT)


def _sigmoid(x):
    return 1.0 / (1.0 + jnp.exp(-x))


def _norm_modulate(x, g, shift, scale):
    ms = jnp.mean(x * x, axis=-1, keepdims=True)
    return (x * lax.rsqrt(ms + EPS)) * (g * (1.0 + scale)) + shift


def _head_rmsnorm(a, gain):
    ms = jnp.mean(a * a, axis=-1, keepdims=True)
    return (a * lax.rsqrt(ms + EPS)) * gain


def _ada_kernel(c_ref, w_ref, b_ref, o_ref):
    c = c_ref[...]
    s = c * _sigmoid(c)
    o_ref[...] = jnp.sum(s * w_ref[...], axis=0, keepdims=True) + b_ref[...]


def _ada(c, w_ada, b_ada, *, tn=1024):
    d, n = w_ada.shape
    return pl.pallas_call(
        _ada_kernel,
        grid=(n // tn,),
        in_specs=[pl.BlockSpec((d, 1), lambda j: (0, 0)),
                  pl.BlockSpec((d, tn), lambda j: (0, j)),
                  pl.BlockSpec((1, tn), lambda j: (0, j))],
        out_specs=pl.BlockSpec((1, tn), lambda j: (0, j)),
        out_shape=jax.ShapeDtypeStruct((1, n), F32),
        compiler_params=_params("arbitrary"),
        name="ada",
    )(c.reshape(d, 1), w_ada, b_ada.reshape(1, n))


DMA_SPLIT = 1


def _row_slab_specs(rows, cols, index_map, **kw):
    def slab(p):
        def imap(*idx):
            r, c = index_map(*idx)
            return r * DMA_SPLIT + p, c
        return pl.BlockSpec((rows // DMA_SPLIT, cols), imap, **kw)
    return [slab(p) for p in range(DMA_SPLIT)]


def _bf16_rows(refs):
    return jnp.concatenate([r[...].astype(BF16) for r in refs], axis=0)


def _ffn_kernel(*refs, sub):
    n = DMA_SPLIT
    x_refs, (ada_ref, g_ref) = refs[0:n], refs[n:n + 2]
    wg_refs, wu_refs, wd_refs = refs[n + 2:2 * n + 2], refs[2 * n + 2:3 * n + 2], refs[3 * n + 2:4 * n + 2]
    o_ref, u_ref = refs[4 * n + 2:]
    f = pl.program_id(1)
    slab = x_refs[0].shape[0]

    @pl.when(f == 0)
    def _():
        for p, x_ref in enumerate(x_refs):
            u = _norm_modulate(x_ref[...], g_ref[...],
                               ada_ref[3 * sub:3 * sub + 1, :], ada_ref[3 * sub + 1:3 * sub + 2, :])
            u_ref[p * slab:(p + 1) * slab, :] = u.astype(BF16)
        o_ref[...] = jnp.zeros_like(o_ref)

    u = u_ref[...]
    hg = jnp.dot(u, _bf16_rows(wg_refs), preferred_element_type=F32)
    hu = jnp.dot(u, _bf16_rows(wu_refs), preferred_element_type=F32)
    h = (hg * _sigmoid(hg)) * hu
    o_ref[...] += jnp.dot(h.astype(BF16), _bf16_rows(wd_refs), preferred_element_type=F32)

    @pl.when(f == pl.num_programs(1) - 1)
    def _():
        gate = ada_ref[3 * sub + 2:3 * sub + 3, :]
        for p, x_ref in enumerate(x_refs):
            rows = slice(p * slab, (p + 1) * slab)
            o_ref[rows, :] = x_ref[...] + (0.5 * gate) * o_ref[rows, :]


def _ffn(x, ada, g, w_gate, w_up, w_down, *, sub, tm=1024, tf=256):
    t, d = x.shape
    dff = w_gate.shape[1]
    n = DMA_SPLIT
    return pl.pallas_call(
        functools.partial(_ffn_kernel, sub=sub),
        grid=(t // tm, dff // tf),
        in_specs=(_row_slab_specs(tm, d, lambda i, f: (i, 0))
                  + [pl.BlockSpec((N_ADA, d), lambda i, f: (0, 0)),
                     pl.BlockSpec((1, d), lambda i, f: (0, 0))]
                  + _row_slab_specs(d, tf, lambda i, f: (0, f))
                  + _row_slab_specs(d, tf, lambda i, f: (0, f))
                  + _row_slab_specs(tf, d, lambda i, f: (f, 0))),
        out_specs=pl.BlockSpec((tm, d), lambda i, f: (i, 0)),
        out_shape=jax.ShapeDtypeStruct((t, d), F32),
        scratch_shapes=[pltpu.VMEM((tm, d), BF16)],
        compiler_params=_params("parallel", "arbitrary"),
        name=f"ffn{sub}",
    )(*([x] * n), ada, g.reshape(1, d), *([w_gate] * n), *([w_up] * n), *([w_down] * n))


HPB = HEADS_PER_DIL_GROUP
PROJ_TN = HPB * HEAD_DIM
BLK_A_END = 3 * N_DIL
BLK_BQK0 = BLK_A_END
BLK_BV0 = BLK_BQK0 + 2 * (N_HEADS_B // HPB)
BLK_GATE0 = BLK_BV0 + N_HEADS_B // HPB


MOBA_VT_PAD = 16
MOBA_VT_ROWS = HEAD_DIM + MOBA_VT_PAD


PROJ_ROWS = MOBA_BLOCK


def _proj_kernel(*refs):
    n = DMA_SPLIT
    nflag_ref, x_refs, (ada_ref, g_ref) = refs[0], refs[1:n + 1], refs[n + 1:n + 3]
    win_refs, wgate_refs = refs[n + 3:2 * n + 3], refs[2 * n + 3:3 * n + 3]
    (gain_ref, kw_ref, a0_ref, a1_ref, a2_ref, bqk_ref, bvt_ref, gate_ref, u_ref, stg_ref) = refs[3 * n + 3:]
    j = pl.program_id(1)
    slab = x_refs[0].shape[0]
    tm = slab * n
    n_chunks = tm // PROJ_ROWS

    @pl.when(j == 0)
    def _():
        for p, x_ref in enumerate(x_refs):
            u = _norm_modulate(x_ref[...], g_ref[...], ada_ref[3:4, :], ada_ref[4:5, :])
            u_ref[p * slab:(p + 1) * slab, :] = u.astype(BF16)

    def chunks(w_refs):
        w = _bf16_rows(w_refs)
        for c in range(n_chunks):
            rows = slice(c * PROJ_ROWS, (c + 1) * PROJ_ROWS)
            yield c, jnp.dot(u_ref[rows, :], w, preferred_element_type=F32)

    def head(res, h):
        a = res[:, h * HEAD_DIM:(h + 1) * HEAD_DIM]
        flag = nflag_ref[j]
        ms = jnp.mean(a * a, axis=-1, keepdims=True)
        return (a * (lax.rsqrt(ms + EPS) * flag + (1.0 - flag))) * gain_ref[0, h:h + 1, :]

    def dilated_block(o_ref, r):
        n = PROJ_ROWS // r
        for c, res in chunks(win_refs):
            for h in range(HPB):
                y = head(res, h)
                if r == 1:
                    o_ref[h, c * n:(c + 1) * n, :] = y.astype(BF16)
                else:
                    stg_ref[c % 2, h] = y
                    for rho in range(r):
                        part = stg_ref.at[c % 2, h][pl.ds(rho, n, stride=r), :]
                        o_ref[h, c * n:(c + 1) * n, rho * HEAD_DIM:(rho + 1) * HEAD_DIM] = part.astype(BF16)

    a_refs = (a0_ref, a1_ref, a2_ref)
    for g, (_, r) in enumerate(DIL_PATTERNS):
        pl.when((j < BLK_A_END) & (lax.rem(j, N_DIL) == g))(functools.partial(dilated_block, a_refs[g], r))

    @pl.when((j >= BLK_BQK0) & (j < BLK_BV0))
    def _():
        for c, res in chunks(win_refs):
            for h in range(HPB):
                bqk_ref[h, c * PROJ_ROWS:(c + 1) * PROJ_ROWS, :] = head(res, h).astype(BF16)

    @pl.when((j >= BLK_BV0) & (j < BLK_GATE0))
    def _():
        for c, res in chunks(win_refs):
            for h in range(HPB):
                kw = kw_ref[0, h]
                yt = res[:, h * HEAD_DIM:(h + 1) * HEAD_DIM].T
                bvt_ref[h, c, 0:HEAD_DIM, :] = (yt * kw[0:1, :]).astype(BF16)
                bvt_ref[h, c, HEAD_DIM:, :] = kw.astype(BF16)

    @pl.when(j >= BLK_GATE0)
    def _():
        for c, res in chunks(wgate_refs):
            gate_ref[c * PROJ_ROWS:(c + 1) * PROJ_ROWS, :] = _sigmoid(res).astype(BF16)


def _proj(x, ada, g, w_in, w_gate, gains, norm_flags, key_weights, *, tm=1024):
    t, d = x.shape
    n_gate = w_gate.shape[1] // PROJ_TN
    n_blk = BLK_GATE0 + n_gate
    assert w_in.shape[1] == BLK_GATE0 * PROJ_TN
    n_bv = BLK_GATE0 - BLK_BV0

    def a_spec(gi, r):
        return pl.BlockSpec((HPB, tm // r, r * HEAD_DIM),
                            lambda i, j: (jnp.clip((j - gi + N_DIL - 1) // N_DIL, 0, 2), i, 0))

    a_shapes = [jax.ShapeDtypeStruct((3 * HPB, t // r, r * HEAD_DIM), BF16) for _, r in DIL_PATTERNS]
    out_shape = a_shapes + [
        jax.ShapeDtypeStruct((2 * N_HEADS_B, t, HEAD_DIM), BF16),
        jax.ShapeDtypeStruct((N_HEADS_B, t // MOBA_BLOCK, MOBA_VT_ROWS, MOBA_BLOCK), BF16),
        jax.ShapeDtypeStruct((t, n_gate * PROJ_TN), BF16)]
    out_specs = [a_spec(gi, r) for gi, (_, r) in enumerate(DIL_PATTERNS)] + [
        pl.BlockSpec((HPB, tm, HEAD_DIM),
                     lambda i, j: (jnp.clip(j - BLK_BQK0, 0, BLK_BV0 - BLK_BQK0 - 1), i, 0)),
        pl.BlockSpec((HPB, tm // MOBA_BLOCK, MOBA_VT_ROWS, MOBA_BLOCK),
                     lambda i, j: (jnp.clip(j - BLK_BV0, 0, n_bv - 1), i, 0, 0)),
        pl.BlockSpec((tm, PROJ_TN), lambda i, j: (i, jnp.clip(j - BLK_GATE0, 0, n_gate - 1)))]
    return pl.pallas_call(
        _proj_kernel,
        grid=(t // tm, n_blk),
        in_specs=([pl.BlockSpec(memory_space=pltpu.SMEM)]
                  + _row_slab_specs(tm, d, lambda i, j: (i, 0))
                  + [pl.BlockSpec((N_ADA, d), lambda i, j: (0, 0)),
                     pl.BlockSpec((1, d), lambda i, j: (0, 0))]
                  + _row_slab_specs(d, PROJ_TN, lambda i, j: (0, jnp.minimum(j, BLK_GATE0 - 1)))
                  + _row_slab_specs(d, PROJ_TN, lambda i, j: (0, jnp.clip(j - BLK_GATE0, 0, n_gate - 1)))
                  + [pl.BlockSpec((1, HPB, HEAD_DIM), lambda i, j: (j, 0, 0)),
                     pl.BlockSpec((1, HPB, MOBA_VT_PAD, MOBA_BLOCK),
                                  lambda i, j: (jnp.clip(j - BLK_BV0, 0, n_bv - 1), 0, 0, 0))]),
        out_specs=out_specs,
        out_shape=out_shape,
        scratch_shapes=[pltpu.VMEM((tm, d), BF16), pltpu.VMEM((2, HPB, PROJ_ROWS, HEAD_DIM), F32)],
        compiler_params=_params("arbitrary", "arbitrary"),
        name="proj",
    )(norm_flags, *([x] * DMA_SPLIT), ada, g.reshape(1, d), *([w_in] * DMA_SPLIT), *([w_gate] * DMA_SPLIT),
      gains.reshape(n_blk, HPB, HEAD_DIM),
      key_weights.reshape(n_bv, HPB, MOBA_VT_PAD, MOBA_BLOCK))


DIL_TOKENS = 2048


def _band_block(q, k2, v2, bias):
    s = lax.dot_general(q, k2, _NT, preferred_element_type=F32) + bias
    m = jnp.max(s, axis=-1, keepdims=True)
    p = jnp.exp2(s - m)
    denom = jnp.sum(p, axis=-1, keepdims=True)
    o = jnp.dot(p.astype(BF16), v2, preferred_element_type=F32) / denom
    lse = jnp.broadcast_to(m + jnp.log2(denom), (BAND, LANES))
    return o, lse


def _dilated_kernel(slopes_ref,
                    q0, k0, v0, kp0, vp0,
                    q1, k1, v1, kp1, vp1,
                    q2, k2, v2, kp2, vp2,
                    y_ref, o_scr, l_scr):
    j = pl.program_id(0)
    b = pl.program_id(1)

    qi = lax.broadcasted_iota(jnp.int32, (BAND, 2 * BAND), 0)
    ki = lax.broadcasted_iota(jnp.int32, (BAND, 2 * BAND), 1)
    dist = BAND + qi - ki
    in_band = (dist >= 0) & (dist <= BAND)
    dist_f = dist.astype(F32)
    first_ok = (ki + jnp.minimum(b, 1) * BAND) >= BAND

    groups = ((q0, k0, v0, kp0, vp0), (q1, k1, v1, kp1, vp1), (q2, k2, v2, kp2, vp2))
    for g, (_, r) in enumerate(DIL_PATTERNS):
        q_ref, k_ref, v_ref, kp_ref, vp_ref = groups[g]
        slope = slopes_ref[g * HEADS_PER_DIL_GROUP + j]
        bias = jnp.where(in_band, dist_f * (-slope * r), NEG_INF)
        bias_first = jnp.where(first_ok, bias, NEG_INF)
        n_blk = DIL_TOKENS // (r * BAND)
        for rho in range(r):
            cols = slice(rho * HEAD_DIM, (rho + 1) * HEAD_DIM)

            def store(i, o, lse, g=g, r=r, rho=rho):
                rows = pl.ds(i * (BAND * r) + rho, BAND, stride=r) if r > 1 else pl.ds(i * BAND, BAND)
                o_scr.at[g][rows, :] = o
                l_scr.at[g][rows, :] = lse

            kk = jnp.concatenate([kp_ref[:, cols], k_ref[0:BAND, cols]], axis=0)
            vv = jnp.concatenate([vp_ref[:, cols], v_ref[0:BAND, cols]], axis=0)
            o, lse = _band_block(q_ref[0:BAND, cols], kk, vv, bias_first)
            store(0, o, lse)

            for i in range(1, n_blk):
                o, lse = _band_block(q_ref[i * BAND:(i + 1) * BAND, cols],
                                     k_ref[(i - 1) * BAND:(i + 1) * BAND, cols],
                                     v_ref[(i - 1) * BAND:(i + 1) * BAND, cols], bias)
                store(i, o, lse)

    l0, l1, l2 = l_scr[0], l_scr[1], l_scr[2]
    mx = jnp.maximum(jnp.maximum(l0, l1), l2)
    e0, e1, e2 = jnp.exp2(l0 - mx), jnp.exp2(l1 - mx), jnp.exp2(l2 - mx)
    tot = e0 + e1 + e2
    y = (e0 / tot) * o_scr[0] + (e1 / tot) * o_scr[1] + (e2 / tot) * o_scr[2]
    y_ref[...] = y.astype(BF16)


def _dilated(a_views, slopes):
    t = a_views[0].shape[1]
    dh = HEAD_DIM
    nb = t // DIL_TOKENS
    hq, hk, hv = 0, HPB, 2 * HPB
    args, specs = [], []
    for g, (_, r) in enumerate(DIL_PATTERNS):
        rows = DIL_TOKENS // r
        prev_per_blk = rows // BAND
        width = r * dh

        def cur(base, rows=rows, width=width):
            return pl.BlockSpec((None, rows, width), lambda j, b: (base + j, b, 0))

        def prev(base, width=width, ppb=prev_per_blk):
            return pl.BlockSpec((None, BAND, width),
                                lambda j, b: (base + j, jnp.maximum(b * ppb - 1, 0), 0))

        args += [a_views[g]] * 5
        specs += [cur(hq), cur(hk), cur(hv), prev(hk), prev(hv)]

    return pl.pallas_call(
        _dilated_kernel,
        grid=(HEADS_PER_DIL_GROUP, nb),
        in_specs=[pl.BlockSpec(memory_space=pltpu.SMEM)] + specs,
        out_specs=pl.BlockSpec((DIL_TOKENS, dh), lambda j, b: (b, j)),
        out_shape=jax.ShapeDtypeStruct((t, HEADS_PER_DIL_GROUP * dh), BF16),
        scratch_shapes=[pltpu.VMEM((N_DIL, DIL_TOKENS, dh), F32),
                        pltpu.VMEM((N_DIL, DIL_TOKENS, LANES), F32)],
        compiler_params=_params("arbitrary", "arbitrary"),
        name="dilated",
    )(slopes, *args)


MOBA_UNROLL = 4


MOBA_HEADS_PER_STEP = 4


def _moba_kernel(slopes_ref, q_ref, k_ref, vt_ref, y_ref, kmean_scr, sel_scr, acc_scr, s_scr, *, n_blocks):
    hg = pl.program_id(0)
    qb = pl.program_id(1)
    blk = MOBA_BLOCK
    n_h, _, dh = q_ref.shape
    heads = range(n_h)
    slope = [slopes_ref[N_HEADS_A + hg * n_h + hh] for hh in heads]

    @pl.when(qb == 0)
    def _():
        def body(n, carry):
            for hh in heads:
                kn = k_ref[hh, pl.ds(pl.multiple_of(n * blk, blk), blk), :].astype(F32)
                kmean_scr[hh, pl.ds(n, 1), :] = jnp.mean(kn, axis=0, keepdims=True)
            return carry
        lax.fori_loop(0, n_blocks, body, 0)

    q = [q_ref[hh] for hh in heads]

    blk_id = lax.broadcasted_iota(jnp.int32, (n_blocks, blk), 0)
    blk_id_f = blk_id.astype(F32)
    past = blk_id < qb
    key_i = lax.broadcasted_iota(jnp.int32, (blk, blk), 0)
    qry_i = lax.broadcasted_iota(jnp.int32, (blk, blk), 1)
    own = pl.multiple_of(qb * blk, blk)

    in_block_max = [slope[hh] * (blk - 1) for hh in heads]

    m0 = []
    for hh in heads:
        gate = lax.dot_general(kmean_scr[hh], q[hh].astype(F32), _NT,
                               precision=lax.Precision.HIGHEST, preferred_element_type=F32)
        work = jnp.where(past, gate, NEG_INF)
        sel = jnp.zeros((n_blocks, blk), F32)
        for _ in range(MOBA_TOPK):
            mx = jnp.max(work, axis=0, keepdims=True)
            first = jnp.min(jnp.where(work == mx, blk_id_f, float(n_blocks)), axis=0, keepdims=True)
            hit = blk_id_f == first
            sel = jnp.where(hit, 1.0, sel)
            work = jnp.where(hit, -jnp.inf, work)
        sel_scr[hh] = jnp.where(past, sel, 0.0)

        s = lax.dot_general(k_ref[hh, pl.ds(own, blk), :], q[hh], _NT, preferred_element_type=F32)
        s = jnp.where(key_i <= qry_i, s, NEG_INF)
        m0.append(jnp.max(s, axis=0, keepdims=True) + in_block_max[hh])
        p = jnp.exp2(s - m0[hh])
        acc_scr[hh] = jnp.dot(vt_ref[hh, qb], p.astype(BF16), preferred_element_type=F32)

    width = MOBA_UNROLL
    n_chunks = (qb + width - 1) // width

    def chunk_scores(hh, ci, slot):
        start = pl.multiple_of(ci * (width * blk), width * blk)
        s_scr[slot, hh] = lax.dot_general(k_ref[hh, pl.ds(start, width * blk), :], q[hh], _NT,
                                          preferred_element_type=F32)

    def past_chunk(ci, m_run, src, dst):
        n0 = ci * width
        nxt = jnp.minimum(ci + 1, n_chunks - 1)
        out = []
        for hh in heads:
            chunk_scores(hh, nxt, dst)
            c, chosen = [], []
            m_chunk = jnp.full((1, blk), NEG_INF, F32)
            for a in range(width):
                s = s_scr[src, hh, a * blk:(a + 1) * blk, :]
                c.append(slope[hh] * jnp.full((1, blk), (n0 + a - qb) * blk, jnp.int32).astype(F32))
                chosen.append(sel_scr[hh, pl.ds(n0 + a, 1), :] > 0.5)
                m_blk = jnp.max(s, axis=0, keepdims=True) + (c[a] + in_block_max[hh])
                m_chunk = jnp.maximum(m_chunk, jnp.where(chosen[a], m_blk, NEG_INF))
            pv = jnp.zeros(acc_scr.shape[1:], F32)
            for a in range(width):
                s = s_scr[src, hh, a * blk:(a + 1) * blk, :]
                p = jnp.exp2(s - jnp.where(chosen[a], m_chunk - c[a], -NEG_INF))
                pv = pv + jnp.dot(vt_ref[hh, n0 + a], p.astype(BF16), preferred_element_type=F32)
            m_new = jnp.maximum(m_run[hh], m_chunk)
            alpha = jnp.exp2(m_run[hh] - m_new)
            beta = jnp.exp2(m_chunk - m_new)
            acc_scr[hh] = alpha * acc_scr[hh] + beta * pv
            out.append(m_new)
        return tuple(out)

    for hh in heads:
        chunk_scores(hh, 0, 0)

    def body(ci, m_run):
        return lax.cond(lax.rem(ci, 2) == 0,
                        lambda m: past_chunk(ci, m, 0, 1), lambda m: past_chunk(ci, m, 1, 0), m_run)

    lax.fori_loop(0, n_chunks, body, tuple(m0))
    for hh in heads:
        y = acc_scr[hh, 0:dh, :] / acc_scr[hh, dh:dh + 1, :]
        y_ref[:, hh * dh:(hh + 1) * dh] = y.T.astype(BF16)


def _moba(bqk, bvt, slopes):
    _, t, dh = bqk.shape
    n_blocks = t // MOBA_BLOCK
    assert n_blocks % MOBA_UNROLL == 0, "the padded last chunk must stay inside the key array"
    n_h = MOBA_HEADS_PER_STEP
    n_groups = N_HEADS_B // n_h
    return pl.pallas_call(
        functools.partial(_moba_kernel, n_blocks=n_blocks),
        grid=(n_groups, n_blocks),
        in_specs=[pl.BlockSpec(memory_space=pltpu.SMEM),
                  pl.BlockSpec((n_h, MOBA_BLOCK, dh), lambda h, i: (h, i, 0)),
                  pl.BlockSpec((n_h, t, dh), lambda h, i: (n_groups + h, 0, 0)),
                  pl.BlockSpec((n_h, n_blocks, MOBA_VT_ROWS, MOBA_BLOCK), lambda h, i: (h, 0, 0, 0))],
        out_specs=pl.BlockSpec((MOBA_BLOCK, n_h * dh), lambda h, i: (i, h)),
        out_shape=jax.ShapeDtypeStruct((t, N_HEADS_B * dh), BF16),
        scratch_shapes=[pltpu.VMEM((n_h, n_blocks, dh), F32),
                        pltpu.VMEM((n_h, n_blocks, MOBA_BLOCK), F32),
                        pltpu.VMEM((n_h, MOBA_VT_ROWS, MOBA_BLOCK), F32),
                        pltpu.VMEM((2, n_h, MOBA_UNROLL * MOBA_BLOCK, MOBA_BLOCK), F32)],
        compiler_params=_params("arbitrary", "arbitrary"),
        name="moba",
    )(slopes, bqk, bqk, bvt)


def _merge_kernel(x_ref, ada_ref, ya_ref, yb_ref, gates_a_ref, gates_b_ref, wa_ref, wb_ref, wo_ref, o_ref):
    pa = jnp.dot(ya_ref[...], wa_ref[...], preferred_element_type=F32)
    pb = jnp.dot(yb_ref[...], wb_ref[...], preferred_element_type=F32)
    merged = gates_a_ref[...].astype(F32) * pa + gates_b_ref[...].astype(F32) * pb
    out = jnp.dot(merged.astype(BF16), wo_ref[...], preferred_element_type=F32)
    o_ref[...] = x_ref[...] + ada_ref[5:6, :] * out


def _merge(x, ada, y_a, y_b, gates, w_a, w_b, w_o, *, tm=256):
    t, d = x.shape
    const = lambda i: (0, 0)
    return pl.pallas_call(
        _merge_kernel,
        grid=(t // tm,),
        in_specs=[pl.BlockSpec((tm, d), lambda i: (i, 0)),
                  pl.BlockSpec((N_ADA, d), const),
                  pl.BlockSpec((tm, y_a.shape[1]), lambda i: (i, 0)),
                  pl.BlockSpec((tm, y_b.shape[1]), lambda i: (i, 0)),
                  pl.BlockSpec((tm, d), lambda i: (i, 0)),
                  pl.BlockSpec((tm, d), lambda i: (i, 1)),
                  pl.BlockSpec(w_a.shape, const),
                  pl.BlockSpec(w_b.shape, const),
                  pl.BlockSpec(w_o.shape, const)],
        out_specs=pl.BlockSpec((tm, d), lambda i: (i, 0)),
        out_shape=jax.ShapeDtypeStruct((t, d), F32),
        compiler_params=_params("parallel"),
        name="merge",
    )(x, ada, y_a, y_b, gates, gates, w_a, w_b, w_o)


def _layer(x, c, w_ada, b_ada, g_ffn1, ffn1_w_gate, ffn1_w_up, ffn1_w_down,
           g_mix, w_in, q_norm, k_norm, w_gate, w_branch_a, w_branch_b, w_out,
           g_ffn2, ffn2_w_gate, ffn2_w_up, ffn2_w_down):
    t, d = x.shape
    ada = _ada(c, w_ada, b_ada).reshape(N_ADA, d)
    slopes = jnp.exp2(-8.0 * jnp.arange(1, N_HEADS + 1, dtype=F32) / N_HEADS) * LOG2_E

    x = _ffn(x, ada, g_ffn1, ffn1_w_gate, ffn1_w_up, ffn1_w_down, sub=0)

    scale = HEAD_DIM ** -0.5 * LOG2_E
    n_gate_heads = w_gate.shape[1] // HEAD_DIM
    gains = jnp.concatenate([q_norm[:N_HEADS_A] * scale, k_norm[:N_HEADS_A], jnp.ones((N_HEADS_A, HEAD_DIM), F32),
                             q_norm[N_HEADS_A:] * scale, k_norm[N_HEADS_A:],
                             jnp.ones((N_HEADS_B + n_gate_heads, HEAD_DIM), F32)], axis=0)
    norm_flags = jnp.concatenate([jnp.ones((2 * N_DIL,), F32), jnp.zeros((N_DIL,), F32),
                                  jnp.ones((BLK_BV0 - BLK_BQK0,), F32),
                                  jnp.zeros((BLK_GATE0 - BLK_BV0 + n_gate_heads // HPB,), F32)])
    offsets = jnp.arange(MOBA_BLOCK, dtype=F32)
    key_weights = jnp.exp2(slopes[N_HEADS_A:, None, None] * offsets[None, None, :])
    key_weights = jnp.pad(key_weights, ((0, 0), (0, MOBA_VT_PAD - 1), (0, 0)))
    a0, a1, a2, bqk, bvt, gates = _proj(x, ada, g_mix, w_in.astype(BF16), w_gate.astype(BF16),
                                        gains, norm_flags, key_weights)

    y_a = _dilated((a0, a1, a2), slopes)
    y_b = _moba(bqk, bvt, slopes)

    x = _merge(x, ada, y_a, y_b, gates, w_branch_a.astype(BF16), w_branch_b.astype(BF16), w_out.astype(BF16))
    x = _ffn(x, ada, g_ffn2, ffn2_w_gate, ffn2_w_up, ffn2_w_down, sub=2)
    return x


def kernel(x, c, w_ada, b_ada, g_ffn1, ffn1_w_gate, ffn1_w_up, ffn1_w_down, g_mix, w_in, q_norm, k_norm,
           w_gate, w_branch_a, w_branch_b, w_out, g_ffn2, ffn2_w_gate, ffn2_w_up, ffn2_w_down):
    batch, depth = x.shape[0], w_ada.shape[0]
    outs = []
    for bi in range(batch):
        xb = x[bi]
        for l in range(depth):
            xb = _layer(xb, c[bi], w_ada[l], b_ada[l], g_ffn1[l], ffn1_w_gate[l], ffn1_w_up[l],
                        ffn1_w_down[l], g_mix[l], w_in[l], q_norm[l], k_norm[l], w_gate[l],
                        w_branch_a[l], w_branch_b[l], w_out[l], g_ffn2[l], ffn2_w_gate[l],
                        ffn2_w_up[l], ffn2_w_down[l])
        outs.append(xb)
    return jnp.stack(outs, axis=0)
```

```python
import functools

import jax
import jax.numpy as jnp
from jax import lax
from jax.experimental import pallas as pl
from jax.experimental.pallas import tpu as pltpu

HEAD_DIM = 128
DIL_PATTERNS = ((128, 1), (512, 4), (2048, 16))
N_DIL = len(DIL_PATTERNS)
HEADS_PER_DIL_GROUP = 4
N_HEADS_A = HEADS_PER_DIL_GROUP * N_DIL
N_HEADS_B = 8
N_HEADS = N_HEADS_A + N_HEADS_B
BAND = 128
MOBA_BLOCK = 256
MOBA_TOPK = 3
N_ADA = 9
EPS = 1e-6
NEG_INF = -1e30
LOG2_E = 1.4426950408889634

LANES = 128
VMEM_LIMIT = 60 * 1024 * 1024

BF16 = jnp.bfloat16
F32 = jnp.float32

_NT = (((1,), (1,)), ((), ()))


def _params(*sem):
    return pltpu.CompilerParams(dimension_semantics=sem, vmem_limit_bytes=VMEM_LIMIT)


def _sigmoid(x):
    return 1.0 / (1.0 + jnp.exp(-x))


NORM_ROWS = 128


def _norm_modulate_into(u_ref, row0, x_ref, g, shift, scale):
    gs = g * (1.0 + scale)

    def body(c, carry):
        rows = pl.ds(pl.multiple_of(c * NORM_ROWS, NORM_ROWS), NORM_ROWS)
        x = x_ref[rows, :]
        ms = jnp.mean(x * x, axis=-1, keepdims=True)
        y = (x * lax.rsqrt(ms + EPS)) * gs + shift
        u_ref[pl.ds(pl.multiple_of(row0 + c * NORM_ROWS, NORM_ROWS), NORM_ROWS), :] = y.astype(BF16)
        return carry

    lax.fori_loop(0, x_ref.shape[0] // NORM_ROWS, body, 0)


def _head_rmsnorm(a, gain):
    ms = jnp.mean(a * a, axis=-1, keepdims=True)
    return (a * lax.rsqrt(ms + EPS)) * gain


def _ada_kernel(c_ref, w_ref, b_ref, o_ref):
    c = c_ref[...]
    s = c * _sigmoid(c)
    o_ref[...] = jnp.sum(s * w_ref[...], axis=0, keepdims=True) + b_ref[...]


def _ada(c, w_ada, b_ada, *, tn=1024):
    d, n = w_ada.shape
    return pl.pallas_call(
        _ada_kernel,
        grid=(n // tn,),
        in_specs=[pl.BlockSpec((d, 1), lambda j: (0, 0)),
                  pl.BlockSpec((d, tn), lambda j: (0, j)),
                  pl.BlockSpec((1, tn), lambda j: (0, j))],
        out_specs=pl.BlockSpec((1, tn), lambda j: (0, j)),
        out_shape=jax.ShapeDtypeStruct((1, n), F32),
        compiler_params=_params("arbitrary"),
        name="ada",
    )(c.reshape(d, 1), w_ada, b_ada.reshape(1, n))


DMA_SPLIT = 1


def _row_slab_specs(rows, cols, index_map, **kw):
    def slab(p):
        def imap(*idx):
            r, c = index_map(*idx)
            return r * DMA_SPLIT + p, c
        return pl.BlockSpec((rows // DMA_SPLIT, cols), imap, **kw)
    return [slab(p) for p in range(DMA_SPLIT)]


def _bf16_rows(refs):
    return jnp.concatenate([r[...].astype(BF16) for r in refs], axis=0)


def _ffn_kernel(*refs, sub):
    n = DMA_SPLIT
    x_refs, (ada_ref, g_ref) = refs[0:n], refs[n:n + 2]
    wg_refs, wu_refs, wd_refs = refs[n + 2:2 * n + 2], refs[2 * n + 2:3 * n + 2], refs[3 * n + 2:4 * n + 2]
    o_ref, u_ref = refs[4 * n + 2:]
    f = pl.program_id(1)
    slab = x_refs[0].shape[0]

    @pl.when(f == 0)
    def _():
        for p, x_ref in enumerate(x_refs):
            _norm_modulate_into(u_ref, p * slab, x_ref, g_ref[...],
                                ada_ref[3 * sub:3 * sub + 1, :], ada_ref[3 * sub + 1:3 * sub + 2, :])
        o_ref[...] = jnp.zeros_like(o_ref)

    u = u_ref[...]
    hg = jnp.dot(u, _bf16_rows(wg_refs), preferred_element_type=F32)
    hu = jnp.dot(u, _bf16_rows(wu_refs), preferred_element_type=F32)
    h = (hg * _sigmoid(hg)) * hu
    o_ref[...] += jnp.dot(h.astype(BF16), _bf16_rows(wd_refs), preferred_element_type=F32)

    @pl.when(f == pl.num_programs(1) - 1)
    def _():
        gate = ada_ref[3 * sub + 2:3 * sub + 3, :]
        for p, x_ref in enumerate(x_refs):
            rows = slice(p * slab, (p + 1) * slab)
            o_ref[rows, :] = x_ref[...] + (0.5 * gate) * o_ref[rows, :]


def _ffn(x, ada, g, w_gate, w_up, w_down, *, sub, tm=1024, tf=256):
    t, d = x.shape
    dff = w_gate.shape[1]
    n = DMA_SPLIT
    return pl.pallas_call(
        functools.partial(_ffn_kernel, sub=sub),
        grid=(t // tm, dff // tf),
        in_specs=(_row_slab_specs(tm, d, lambda i, f: (i, 0))
                  + [pl.BlockSpec((N_ADA, d), lambda i, f: (0, 0)),
                     pl.BlockSpec((1, d), lambda i, f: (0, 0))]
                  + _row_slab_specs(d, tf, lambda i, f: (0, f))
                  + _row_slab_specs(d, tf, lambda i, f: (0, f))
                  + _row_slab_specs(tf, d, lambda i, f: (f, 0))),
        out_specs=pl.BlockSpec((tm, d), lambda i, f: (i, 0)),
        out_shape=jax.ShapeDtypeStruct((t, d), F32),
        scratch_shapes=[pltpu.VMEM((tm, d), BF16)],
        compiler_params=_params("parallel", "arbitrary"),
        name=f"ffn{sub}",
    )(*([x] * n), ada, g.reshape(1, d), *([w_gate] * n), *([w_up] * n), *([w_down] * n))


HPB = HEADS_PER_DIL_GROUP
PROJ_TN = HPB * HEAD_DIM
BLK_A_END = 3 * N_DIL
BLK_BQK0 = BLK_A_END
BLK_BV0 = BLK_BQK0 + 2 * (N_HEADS_B // HPB)
BLK_GATE0 = BLK_BV0 + N_HEADS_B // HPB


MOBA_VT_PAD = 16
MOBA_VT_ROWS = HEAD_DIM + MOBA_VT_PAD


PROJ_ROWS = MOBA_BLOCK


def _proj_kernel(*refs):
    n = DMA_SPLIT
    nflag_ref, x_refs, (ada_ref, g_ref) = refs[0], refs[1:n + 1], refs[n + 1:n + 3]
    win_refs, wgate_refs = refs[n + 3:2 * n + 3], refs[2 * n + 3:3 * n + 3]
    (gain_ref, kw_ref, a0_ref, a1_ref, a2_ref, bqk_ref, bvt_ref, gate_ref, u_ref, stg_ref) = refs[3 * n + 3:]
    j = pl.program_id(1)
    slab = x_refs[0].shape[0]
    tm = slab * n
    n_chunks = tm // PROJ_ROWS

    @pl.when(j == 0)
    def _():
        for p, x_ref in enumerate(x_refs):
            _norm_modulate_into(u_ref, p * slab, x_ref, g_ref[...], ada_ref[3:4, :], ada_ref[4:5, :])

    def chunks(w_refs):
        w = _bf16_rows(w_refs)
        for c in range(n_chunks):
            rows = slice(c * PROJ_ROWS, (c + 1) * PROJ_ROWS)
            yield c, jnp.dot(u_ref[rows, :], w, preferred_element_type=F32)

    def head(res, h):
        a = res[:, h * HEAD_DIM:(h + 1) * HEAD_DIM]
        flag = nflag_ref[j]
        ms = jnp.mean(a * a, axis=-1, keepdims=True)
        return (a * (lax.rsqrt(ms + EPS) * flag + (1.0 - flag))) * gain_ref[j, h:h + 1, :]

    def dilated_block(o_ref, r):
        n = PROJ_ROWS // r
        for c, res in chunks(win_refs):
            for h in range(HPB):
                y = head(res, h)
                if r == 1:
                    o_ref[h, c * n:(c + 1) * n, :] = y.astype(BF16)
                else:
                    stg_ref[c % 2, h] = y
                    for rho in range(r):
                        part = stg_ref.at[c % 2, h][pl.ds(rho, n, stride=r), :]
                        o_ref[h, c * n:(c + 1) * n, rho * HEAD_DIM:(rho + 1) * HEAD_DIM] = part.astype(BF16)

    a_refs = (a0_ref, a1_ref, a2_ref)
    for g, (_, r) in enumerate(DIL_PATTERNS):
        pl.when((j < BLK_A_END) & (lax.rem(j, N_DIL) == g))(functools.partial(dilated_block, a_refs[g], r))

    @pl.when((j >= BLK_BQK0) & (j < BLK_BV0))
    def _():
        for c, res in chunks(win_refs):
            for h in range(HPB):
                bqk_ref[h, c * PROJ_ROWS:(c + 1) * PROJ_ROWS, :] = head(res, h).astype(BF16)

    @pl.when((j >= BLK_BV0) & (j < BLK_GATE0))
    def _():
        for c, res in chunks(win_refs):
            for h in range(HPB):
                kw = kw_ref[j - BLK_BV0, h]
                yt = res[:, h * HEAD_DIM:(h + 1) * HEAD_DIM].T
                bvt_ref[h, c, 0:HEAD_DIM, :] = (yt * kw[0:1, :]).astype(BF16)
                bvt_ref[h, c, HEAD_DIM:, :] = kw.astype(BF16)

    @pl.when(j >= BLK_GATE0)
    def _():
        for c, res in chunks(wgate_refs):
            gate_ref[c * PROJ_ROWS:(c + 1) * PROJ_ROWS, :] = _sigmoid(res).astype(BF16)


def _proj(x, ada, g, w_in, w_gate, gains, norm_flags, key_weights, *, tm=1024):
    t, d = x.shape
    n_gate = w_gate.shape[1] // PROJ_TN
    n_blk = BLK_GATE0 + n_gate
    assert w_in.shape[1] == BLK_GATE0 * PROJ_TN
    n_bv = BLK_GATE0 - BLK_BV0

    def a_spec(gi, r):
        return pl.BlockSpec((HPB, tm // r, r * HEAD_DIM),
                            lambda i, j: (jnp.clip((j - gi + N_DIL - 1) // N_DIL, 0, 2), i, 0))

    a_shapes = [jax.ShapeDtypeStruct((3 * HPB, t // r, r * HEAD_DIM), BF16) for _, r in DIL_PATTERNS]
    out_shape = a_shapes + [
        jax.ShapeDtypeStruct((2 * N_HEADS_B, t, HEAD_DIM), BF16),
        jax.ShapeDtypeStruct((N_HEADS_B, t // MOBA_BLOCK, MOBA_VT_ROWS, MOBA_BLOCK), BF16),
        jax.ShapeDtypeStruct((t, n_gate * PROJ_TN), BF16)]
    out_specs = [a_spec(gi, r) for gi, (_, r) in enumerate(DIL_PATTERNS)] + [
        pl.BlockSpec((HPB, tm, HEAD_DIM),
                     lambda i, j: (jnp.clip(j - BLK_BQK0, 0, BLK_BV0 - BLK_BQK0 - 1), i, 0)),
        pl.BlockSpec((HPB, tm // MOBA_BLOCK, MOBA_VT_ROWS, MOBA_BLOCK),
                     lambda i, j: (jnp.clip(j - BLK_BV0, 0, n_bv - 1), i, 0, 0)),
        pl.BlockSpec((tm, PROJ_TN), lambda i, j: (i, jnp.clip(j - BLK_GATE0, 0, n_gate - 1)))]
    return pl.pallas_call(
        _proj_kernel,
        grid=(t // tm, n_blk),
        in_specs=([pl.BlockSpec(memory_space=pltpu.SMEM)]
                  + _row_slab_specs(tm, d, lambda i, j: (i, 0))
                  + [pl.BlockSpec((N_ADA, d), lambda i, j: (0, 0)),
                     pl.BlockSpec((1, d), lambda i, j: (0, 0))]
                  + _row_slab_specs(d, PROJ_TN, lambda i, j: (0, jnp.minimum(j, BLK_GATE0 - 1)))
                  + _row_slab_specs(d, PROJ_TN, lambda i, j: (0, jnp.clip(j - BLK_GATE0, 0, n_gate - 1)))
                  + [pl.BlockSpec((n_blk, HPB, HEAD_DIM), lambda i, j: (0, 0, 0)),
                     pl.BlockSpec((n_bv, HPB, MOBA_VT_PAD, MOBA_BLOCK), lambda i, j: (0, 0, 0, 0))]),
        out_specs=out_specs,
        out_shape=out_shape,
        scratch_shapes=[pltpu.VMEM((tm, d), BF16), pltpu.VMEM((2, HPB, PROJ_ROWS, HEAD_DIM), F32)],
        compiler_params=_params("arbitrary", "arbitrary"),
        name="proj",
    )(norm_flags, *([x] * DMA_SPLIT), ada, g.reshape(1, d), *([w_in] * DMA_SPLIT), *([w_gate] * DMA_SPLIT),
      gains.reshape(n_blk, HPB, HEAD_DIM),
      key_weights.reshape(n_bv, HPB, MOBA_VT_PAD, MOBA_BLOCK))


DIL_TOKENS = 2048


def _band_block(q, k2, v2, bias):
    s = lax.dot_general(q, k2, _NT, preferred_element_type=F32) + bias
    m = jnp.max(s, axis=-1, keepdims=True)
    p = jnp.exp2(s - m)
    denom = jnp.sum(p, axis=-1, keepdims=True)
    o = jnp.dot(p.astype(BF16), v2, preferred_element_type=F32) / denom
    lse = jnp.broadcast_to(m + jnp.log2(denom), (BAND, LANES))
    return o, lse


def _dilated_kernel(slopes_ref,
                    q0, k0, v0, kp0, vp0,
                    q1, k1, v1, kp1, vp1,
                    q2, k2, v2, kp2, vp2,
                    y_ref, o_scr, l_scr):
    j = pl.program_id(0)
    b = pl.program_id(1)

    qi = lax.broadcasted_iota(jnp.int32, (BAND, 2 * BAND), 0)
    ki = lax.broadcasted_iota(jnp.int32, (BAND, 2 * BAND), 1)
    dist = BAND + qi - ki
    in_band = (dist >= 0) & (dist <= BAND)
    dist_f = dist.astype(F32)
    first_ok = (ki + jnp.minimum(b, 1) * BAND) >= BAND

    groups = ((q0, k0, v0, kp0, vp0), (q1, k1, v1, kp1, vp1), (q2, k2, v2, kp2, vp2))
    for g, (_, r) in enumerate(DIL_PATTERNS):
        q_ref, k_ref, v_ref, kp_ref, vp_ref = groups[g]
        slope = slopes_ref[g * HEADS_PER_DIL_GROUP + j]
        bias = jnp.where(in_band, dist_f * (-slope * r), NEG_INF)
        bias_first = jnp.where(first_ok, bias, NEG_INF)
        n_blk = DIL_TOKENS // (r * BAND)
        for rho in range(r):
            cols = slice(rho * HEAD_DIM, (rho + 1) * HEAD_DIM)

            def store(i, o, lse, g=g, r=r, rho=rho):
                rows = pl.ds(i * (BAND * r) + rho, BAND, stride=r) if r > 1 else pl.ds(i * BAND, BAND)
                o_scr.at[g][rows, :] = o
                l_scr.at[g][rows, :] = lse

            kk = jnp.concatenate([kp_ref[:, cols], k_ref[0:BAND, cols]], axis=0)
            vv = jnp.concatenate([vp_ref[:, cols], v_ref[0:BAND, cols]], axis=0)
            o, lse = _band_block(q_ref[0:BAND, cols], kk, vv, bias_first)
            store(0, o, lse)

            for i in range(1, n_blk):
                o, lse = _band_block(q_ref[i * BAND:(i + 1) * BAND, cols],
                                     k_ref[(i - 1) * BAND:(i + 1) * BAND, cols],
                                     v_ref[(i - 1) * BAND:(i + 1) * BAND, cols], bias)
                store(i, o, lse)

    l0, l1, l2 = l_scr[0], l_scr[1], l_scr[2]
    mx = jnp.maximum(jnp.maximum(l0, l1), l2)
    e0, e1, e2 = jnp.exp2(l0 - mx), jnp.exp2(l1 - mx), jnp.exp2(l2 - mx)
    tot = e0 + e1 + e2
    y = (e0 / tot) * o_scr[0] + (e1 / tot) * o_scr[1] + (e2 / tot) * o_scr[2]
    y_ref[...] = y.astype(BF16)


def _dilated(a_views, slopes):
    t = a_views[0].shape[1]
    dh = HEAD_DIM
    nb = t // DIL_TOKENS
    hq, hk, hv = 0, HPB, 2 * HPB
    args, specs = [], []
    for g, (_, r) in enumerate(DIL_PATTERNS):
        rows = DIL_TOKENS // r
        prev_per_blk = rows // BAND
        width = r * dh

        def cur(base, rows=rows, width=width):
            return pl.BlockSpec((None, rows, width), lambda j, b: (base + j, b, 0))

        def prev(base, width=width, ppb=prev_per_blk):
            return pl.BlockSpec((None, BAND, width),
                                lambda j, b: (base + j, jnp.maximum(b * ppb - 1, 0), 0))

        args += [a_views[g]] * 5
        specs += [cur(hq), cur(hk), cur(hv), prev(hk), prev(hv)]

    return pl.pallas_call(
        _dilated_kernel,
        grid=(HEADS_PER_DIL_GROUP, nb),
        in_specs=[pl.BlockSpec(memory_space=pltpu.SMEM)] + specs,
        out_specs=pl.BlockSpec((DIL_TOKENS, dh), lambda j, b: (b, j)),
        out_shape=jax.ShapeDtypeStruct((t, HEADS_PER_DIL_GROUP * dh), BF16),
        scratch_shapes=[pltpu.VMEM((N_DIL, DIL_TOKENS, dh), F32),
                        pltpu.VMEM((N_DIL, DIL_TOKENS, LANES), F32)],
        compiler_params=_params("arbitrary", "arbitrary"),
        name="dilated",
    )(slopes, *args)


MOBA_UNROLL = 4


MOBA_HEADS_PER_STEP = 4


def _moba_kernel(slopes_ref, q_ref, k_ref, vt_ref, y_ref, kmean_scr, sel_scr, acc_scr, s_scr, *, n_blocks):
    hg = pl.program_id(0)
    qb = pl.program_id(1)
    blk = MOBA_BLOCK
    n_h, _, dh = q_ref.shape
    heads = range(n_h)
    slope = [slopes_ref[N_HEADS_A + hg * n_h + hh] for hh in heads]

    @pl.when(qb == 0)
    def _():
        def body(n, carry):
            for hh in heads:
                kn = k_ref[hh, pl.ds(pl.multiple_of(n * blk, blk), blk), :].astype(F32)
                kmean_scr[hh, pl.ds(n, 1), :] = jnp.mean(kn, axis=0, keepdims=True)
            return carry
        lax.fori_loop(0, n_blocks, body, 0)

    q = [q_ref[hh] for hh in heads]

    blk_id = lax.broadcasted_iota(jnp.int32, (n_blocks, blk), 0)
    blk_id_f = blk_id.astype(F32)
    past = blk_id < qb
    key_i = lax.broadcasted_iota(jnp.int32, (blk, blk), 0)
    qry_i = lax.broadcasted_iota(jnp.int32, (blk, blk), 1)
    own = pl.multiple_of(qb * blk, blk)

    in_block_max = [slope[hh] * (blk - 1) for hh in heads]

    m0 = []
    for hh in heads:
        gate = lax.dot_general(kmean_scr[hh], q[hh].astype(F32), _NT,
                               precision=lax.Precision.HIGHEST, preferred_element_type=F32)
        work = jnp.where(past, gate, NEG_INF)
        sel = jnp.zeros((n_blocks, blk), F32)
        for _ in range(MOBA_TOPK):
            mx = jnp.max(work, axis=0, keepdims=True)
            first = jnp.min(jnp.where(work == mx, blk_id_f, float(n_blocks)), axis=0, keepdims=True)
            hit = blk_id_f == first
            sel = jnp.where(hit, 1.0, sel)
            work = jnp.where(hit, -jnp.inf, work)
        sel_scr[hh] = jnp.where(past, sel, 0.0)

        s = lax.dot_general(k_ref[hh, pl.ds(own, blk), :], q[hh], _NT, preferred_element_type=F32)
        s = jnp.where(key_i <= qry_i, s, NEG_INF)
        m0.append(jnp.max(s, axis=0, keepdims=True) + in_block_max[hh])
        p = jnp.exp2(s - m0[hh])
        acc_scr[hh] = jnp.dot(vt_ref[hh, qb], p.astype(BF16), preferred_element_type=F32)

    width = MOBA_UNROLL
    n_chunks = (qb + width - 1) // width

    def chunk_scores(hh, ci, slot):
        start = pl.multiple_of(ci * (width * blk), width * blk)
        s_scr[slot, hh] = lax.dot_general(k_ref[hh, pl.ds(start, width * blk), :], q[hh], _NT,
                                          preferred_element_type=F32)

    def past_chunk(ci, m_run, src, dst):
        n0 = ci * width
        nxt = jnp.minimum(ci + 1, n_chunks - 1)
        out = []
        for hh in heads:
            chunk_scores(hh, nxt, dst)
            c, chosen = [], []
            m_chunk = jnp.full((1, blk), NEG_INF, F32)
            for a in range(width):
                s = s_scr[src, hh, a * blk:(a + 1) * blk, :]
                c.append(slope[hh] * jnp.full((1, blk), (n0 + a - qb) * blk, jnp.int32).astype(F32))
                chosen.append(sel_scr[hh, pl.ds(n0 + a, 1), :] > 0.5)
                m_blk = jnp.max(s, axis=0, keepdims=True) + (c[a] + in_block_max[hh])
                m_chunk = jnp.maximum(m_chunk, jnp.where(chosen[a], m_blk, NEG_INF))
            pv = jnp.zeros(acc_scr.shape[1:], F32)
            for a in range(width):
                s = s_scr[src, hh, a * blk:(a + 1) * blk, :]
                p = jnp.exp2(s - jnp.where(chosen[a], m_chunk - c[a], -NEG_INF))
                pv = pv + jnp.dot(vt_ref[hh, n0 + a], p.astype(BF16), preferred_element_type=F32)
            m_new = jnp.maximum(m_run[hh], m_chunk)
            alpha = jnp.exp2(m_run[hh] - m_new)
            beta = jnp.exp2(m_chunk - m_new)
            acc_scr[hh] = alpha * acc_scr[hh] + beta * pv
            out.append(m_new)
        return tuple(out)

    for hh in heads:
        chunk_scores(hh, 0, 0)

    def body(ci, m_run):
        return lax.cond(lax.rem(ci, 2) == 0,
                        lambda m: past_chunk(ci, m, 0, 1), lambda m: past_chunk(ci, m, 1, 0), m_run)

    lax.fori_loop(0, n_chunks, body, tuple(m0))
    for hh in heads:
        y = acc_scr[hh, 0:dh, :] / acc_scr[hh, dh:dh + 1, :]
        y_ref[:, hh * dh:(hh + 1) * dh] = y.T.astype(BF16)


def _moba(bqk, bvt, slopes):
    _, t, dh = bqk.shape
    n_blocks = t // MOBA_BLOCK
    assert n_blocks % MOBA_UNROLL == 0, "the padded last chunk must stay inside the key array"
    n_h = MOBA_HEADS_PER_STEP
    n_groups = N_HEADS_B // n_h
    return pl.pallas_call(
        functools.partial(_moba_kernel, n_blocks=n_blocks),
        grid=(n_groups, n_blocks),
        in_specs=[pl.BlockSpec(memory_space=pltpu.SMEM),
                  pl.BlockSpec((n_h, MOBA_BLOCK, dh), lambda h, i: (h, i, 0)),
                  pl.BlockSpec((n_h, t, dh), lambda h, i: (n_groups + h, 0, 0)),
                  pl.BlockSpec((n_h, n_blocks, MOBA_VT_ROWS, MOBA_BLOCK), lambda h, i: (h, 0, 0, 0))],
        out_specs=pl.BlockSpec((MOBA_BLOCK, n_h * dh), lambda h, i: (i, h)),
        out_shape=jax.ShapeDtypeStruct((t, N_HEADS_B * dh), BF16),
        scratch_shapes=[pltpu.VMEM((n_h, n_blocks, dh), F32),
                        pltpu.VMEM((n_h, n_blocks, MOBA_BLOCK), F32),
                        pltpu.VMEM((n_h, MOBA_VT_ROWS, MOBA_BLOCK), F32),
                        pltpu.VMEM((2, n_h, MOBA_UNROLL * MOBA_BLOCK, MOBA_BLOCK), F32)],
        compiler_params=_params("arbitrary", "arbitrary"),
        name="moba",
    )(slopes, bqk, bqk, bvt)


def _merge_kernel(x_ref, ada_ref, ya_ref, yb_ref, gates_a_ref, gates_b_ref, wa_ref, wb_ref, wo_ref, o_ref):
    pa = jnp.dot(ya_ref[...], wa_ref[...], preferred_element_type=F32)
    pb = jnp.dot(yb_ref[...], wb_ref[...], preferred_element_type=F32)
    merged = gates_a_ref[...].astype(F32) * pa + gates_b_ref[...].astype(F32) * pb
    out = jnp.dot(merged.astype(BF16), wo_ref[...], preferred_element_type=F32)
    o_ref[...] = x_ref[...] + ada_ref[5:6, :] * out


def _merge(x, ada, y_a, y_b, gates, w_a, w_b, w_o, *, tm=256):
    t, d = x.shape
    const = lambda i: (0, 0)
    return pl.pallas_call(
        _merge_kernel,
        grid=(t // tm,),
        in_specs=[pl.BlockSpec((tm, d), lambda i: (i, 0)),
                  pl.BlockSpec((N_ADA, d), const),
                  pl.BlockSpec((tm, y_a.shape[1]), lambda i: (i, 0)),
                  pl.BlockSpec((tm, y_b.shape[1]), lambda i: (i, 0)),
                  pl.BlockSpec((tm, d), lambda i: (i, 0)),
                  pl.BlockSpec((tm, d), lambda i: (i, 1)),
                  pl.BlockSpec(w_a.shape, const),
                  pl.BlockSpec(w_b.shape, const),
                  pl.BlockSpec(w_o.shape, const)],
        out_specs=pl.BlockSpec((tm, d), lambda i: (i, 0)),
        out_shape=jax.ShapeDtypeStruct((t, d), F32),
        compiler_params=_params("parallel"),
        name="merge",
    )(x, ada, y_a, y_b, gates, gates, w_a, w_b, w_o)


def _layer(x, c, w_ada, b_ada, g_ffn1, ffn1_w_gate, ffn1_w_up, ffn1_w_down,
           g_mix, w_in, q_norm, k_norm, w_gate, w_branch_a, w_branch_b, w_out,
           g_ffn2, ffn2_w_gate, ffn2_w_up, ffn2_w_down):
    t, d = x.shape
    ada = _ada(c, w_ada, b_ada).reshape(N_ADA, d)
    slopes = jnp.exp2(-8.0 * jnp.arange(1, N_HEADS + 1, dtype=F32) / N_HEADS) * LOG2_E

    x = _ffn(x, ada, g_ffn1, ffn1_w_gate, ffn1_w_up, ffn1_w_down, sub=0)

    scale = HEAD_DIM ** -0.5 * LOG2_E
    n_gate_heads = w_gate.shape[1] // HEAD_DIM
    gains = jnp.concatenate([q_norm[:N_HEADS_A] * scale, k_norm[:N_HEADS_A], jnp.ones((N_HEADS_A, HEAD_DIM), F32),
                             q_norm[N_HEADS_A:] * scale, k_norm[N_HEADS_A:],
                             jnp.ones((N_HEADS_B + n_gate_heads, HEAD_DIM), F32)], axis=0)
    norm_flags = jnp.concatenate([jnp.ones((2 * N_DIL,), F32), jnp.zeros((N_DIL,), F32),
                                  jnp.ones((BLK_BV0 - BLK_BQK0,), F32),
                                  jnp.zeros((BLK_GATE0 - BLK_BV0 + n_gate_heads // HPB,), F32)])
    offsets = jnp.arange(MOBA_BLOCK, dtype=F32)
    key_weights = jnp.exp2(slopes[N_HEADS_A:, None, None] * offsets[None, None, :])
    key_weights = jnp.pad(key_weights, ((0, 0), (0, MOBA_VT_PAD - 1), (0, 0)))
    a0, a1, a2, bqk, bvt, gates = _proj(x, ada, g_mix, w_in.astype(BF16), w_gate.astype(BF16),
                                        gains, norm_flags, key_weights)

    y_a = _dilated((a0, a1, a2), slopes)
    y_b = _moba(bqk, bvt, slopes)

    x = _merge(x, ada, y_a, y_b, gates, w_branch_a.astype(BF16), w_branch_b.astype(BF16), w_out.astype(BF16))
    x = _ffn(x, ada, g_ffn2, ffn2_w_gate, ffn2_w_up, ffn2_w_down, sub=2)
    return x


def kernel(x, c, w_ada, b_ada, g_ffn1, ffn1_w_gate, ffn1_w_up, ffn1_w_down, g_mix, w_in, q_norm, k_norm,
           w_gate, w_branch_a, w_branch_b, w_out, g_ffn2, ffn2_w_gate, ffn2_w_up, ffn2_w_down):
    batch, depth = x.shape[0], w_ada.shape[0]
    outs = []
    for bi in range(batch):
        xb = x[bi]
        for l in range(depth):
            xb = _layer(xb, c[bi], w_ada[l], b_ada[l], g_ffn1[l], ffn1_w_gate[l], ffn1_w_up[l],
                        ffn1_w_down[l], g_mix[l], w_in[l], q_norm[l], k_norm[l], w_gate[l],
                        w_branch_a[l], w_branch_b[l], w_out[l], g_ffn2[l], ffn2_w_gate[l],
                        ffn2_w_up[l], ffn2_w_down[l])
        outs.append(xb)
    return jnp.stack(outs, axis=0)
```

```python
import functools

import jax
import jax.numpy as jnp
from jax import lax
from jax.experimental import pallas as pl
from jax.experimental.pallas import tpu as pltpu

HEAD_DIM = 128
DIL_PATTERNS = ((128, 1), (512, 4), (2048, 16))
N_DIL = len(DIL_PATTERNS)
HEADS_PER_DIL_GROUP = 4
N_HEADS_A = HEADS_PER_DIL_GROUP * N_DIL
N_HEADS_B = 8
N_HEADS = N_HEADS_A + N_HEADS_B
BAND = 128
MOBA_BLOCK = 256
MOBA_TOPK = 3
N_ADA = 9
EPS = 1e-6
NEG_INF = -1e30
LOG2_E = 1.4426950408889634

LANES = 128
VMEM_LIMIT = 60 * 1024 * 1024

BF16 = jnp.bfloat16
F32 = jnp.float32

_NT = (((1,), (1,)), ((), ()))


def _params(*sem):
    return pltpu.CompilerParams(dimension_semantics=sem, vmem_limit_bytes=VMEM_LIMIT)


def _sigmoid(x):
    return 1.0 / (1.0 + jnp.exp(-x))


NORM_ROWS = 128


def _norm_modulate_into(u_ref, row0, x_ref, g, shift, scale):
    gs = g * (1.0 + scale)

    def body(c, carry):
        rows = pl.ds(pl.multiple_of(c * NORM_ROWS, NORM_ROWS), NORM_ROWS)
        x = x_ref[rows, :]
        ms = jnp.mean(x * x, axis=-1, keepdims=True)
        y = (x * lax.rsqrt(ms + EPS)) * gs + shift
        u_ref[pl.ds(pl.multiple_of(row0 + c * NORM_ROWS, NORM_ROWS), NORM_ROWS), :] = y.astype(BF16)
        return carry

    lax.fori_loop(0, x_ref.shape[0] // NORM_ROWS, body, 0)


def _head_rmsnorm(a, gain):
    ms = jnp.mean(a * a, axis=-1, keepdims=True)
    return (a * lax.rsqrt(ms + EPS)) * gain


def _ada_kernel(c_ref, w_ref, b_ref, o_ref):
    c = c_ref[...]
    s = c * _sigmoid(c)
    o_ref[...] = jnp.sum(s * w_ref[...], axis=0, keepdims=True) + b_ref[...]


def _ada(c, w_ada, b_ada, *, tn=1024):
    d, n = w_ada.shape
    return pl.pallas_call(
        _ada_kernel,
        grid=(n // tn,),
        in_specs=[pl.BlockSpec((d, 1), lambda j: (0, 0)),
                  pl.BlockSpec((d, tn), lambda j: (0, j)),
                  pl.BlockSpec((1, tn), lambda j: (0, j))],
        out_specs=pl.BlockSpec((1, tn), lambda j: (0, j)),
        out_shape=jax.ShapeDtypeStruct((1, n), F32),
        compiler_params=_params("arbitrary"),
        name="ada",
    )(c.reshape(d, 1), w_ada, b_ada.reshape(1, n))


def _ffn_kernel(x_ref, ada_ref, g_ref, wg_ref, wu_ref, wd_ref, o_ref, u_ref, *, sub):
    f = pl.program_id(1)

    @pl.when(f == 0)
    def _():
        _norm_modulate_into(u_ref, 0, x_ref, g_ref[...],
                            ada_ref[3 * sub:3 * sub + 1, :], ada_ref[3 * sub + 1:3 * sub + 2, :])
        o_ref[...] = jnp.zeros_like(o_ref)

    u = u_ref[...]
    hg = jnp.dot(u, wg_ref[...].astype(BF16), preferred_element_type=F32)
    hu = jnp.dot(u, wu_ref[...].astype(BF16), preferred_element_type=F32)
    h = (hg * _sigmoid(hg)) * hu
    o_ref[...] += jnp.dot(h.astype(BF16), wd_ref[...].astype(BF16), preferred_element_type=F32)

    @pl.when(f == pl.num_programs(1) - 1)
    def _():
        gate = ada_ref[3 * sub + 2:3 * sub + 3, :]
        o_ref[...] = x_ref[...] + (0.5 * gate) * o_ref[...]


def _ffn(x, ada, g, w_gate, w_up, w_down, *, sub, tm=1024, tf=256):
    t, d = x.shape
    dff = w_gate.shape[1]
    return pl.pallas_call(
        functools.partial(_ffn_kernel, sub=sub),
        grid=(t // tm, dff // tf),
        in_specs=[pl.BlockSpec((tm, d), lambda i, f: (i, 0)),
                  pl.BlockSpec((N_ADA, d), lambda i, f: (0, 0)),
                  pl.BlockSpec((1, d), lambda i, f: (0, 0)),
                  pl.BlockSpec((d, tf), lambda i, f: (0, f)),
                  pl.BlockSpec((d, tf), lambda i, f: (0, f)),
                  pl.BlockSpec((tf, d), lambda i, f: (f, 0))],
        out_specs=pl.BlockSpec((tm, d), lambda i, f: (i, 0)),
        out_shape=jax.ShapeDtypeStruct((t, d), F32),
        scratch_shapes=[pltpu.VMEM((tm, d), BF16)],
        compiler_params=_params("parallel", "arbitrary"),
        name=f"ffn{sub}",
    )(x, ada, g.reshape(1, d), w_gate, w_up, w_down)


def _norm_kernel(x_ref, ada_ref, g_ref, u_ref, *, sub):
    _norm_modulate_into(u_ref, 0, x_ref, g_ref[...],
                        ada_ref[3 * sub:3 * sub + 1, :], ada_ref[3 * sub + 1:3 * sub + 2, :])


def _norm(x, ada, g, *, sub, tm=512):
    t, d = x.shape
    return pl.pallas_call(
        functools.partial(_norm_kernel, sub=sub),
        grid=(t // tm,),
        in_specs=[pl.BlockSpec((tm, d), lambda i: (i, 0)),
                  pl.BlockSpec((N_ADA, d), lambda i: (0, 0)),
                  pl.BlockSpec((1, d), lambda i: (0, 0))],
        out_specs=pl.BlockSpec((tm, d), lambda i: (i, 0)),
        out_shape=jax.ShapeDtypeStruct((t, d), BF16),
        compiler_params=_params("parallel"),
        name="norm",
    )(x, ada, g.reshape(1, d))


HPB = HEADS_PER_DIL_GROUP
PROJ_TN = HPB * HEAD_DIM
BLK_A_END = 3 * N_DIL
BLK_BQK0 = BLK_A_END
BLK_BV0 = BLK_BQK0 + 2 * (N_HEADS_B // HPB)
BLK_GATE0 = BLK_BV0 + N_HEADS_B // HPB


MOBA_VT_PAD = 16
MOBA_VT_ROWS = HEAD_DIM + MOBA_VT_PAD


PROJ_ROWS = MOBA_BLOCK


def _proj_kernel(nflag_ref, u_ref, win_ref, wgate_ref, gain_ref, kw_ref,
                 a0_ref, a1_ref, a2_ref, bqk_ref, bvt_ref, gate_ref, stg_ref):
    j = pl.program_id(1)
    n_chunks = u_ref.shape[0] // PROJ_ROWS

    def chunks(w_ref):
        w = w_ref[...]
        for c in range(n_chunks):
            rows = slice(c * PROJ_ROWS, (c + 1) * PROJ_ROWS)
            yield c, jnp.dot(u_ref[rows, :], w, preferred_element_type=F32)

    def head(res, h):
        a = res[:, h * HEAD_DIM:(h + 1) * HEAD_DIM]
        flag = nflag_ref[j]
        ms = jnp.mean(a * a, axis=-1, keepdims=True)
        return (a * (lax.rsqrt(ms + EPS) * flag + (1.0 - flag))) * gain_ref[j, h:h + 1, :]

    def dilated_block(o_ref, r):
        n = PROJ_ROWS // r
        for c, res in chunks(win_ref):
            for h in range(HPB):
                y = head(res, h)
                if r == 1:
                    o_ref[h, c * n:(c + 1) * n, :] = y.astype(BF16)
                else:
                    stg_ref[c % 2, h] = y
                    for rho in range(r):
                        part = stg_ref.at[c % 2, h][pl.ds(rho, n, stride=r), :]
                        o_ref[h, c * n:(c + 1) * n, rho * HEAD_DIM:(rho + 1) * HEAD_DIM] = part.astype(BF16)

    a_refs = (a0_ref, a1_ref, a2_ref)
    for g, (_, r) in enumerate(DIL_PATTERNS):
        pl.when((j < BLK_A_END) & (lax.rem(j, N_DIL) == g))(functools.partial(dilated_block, a_refs[g], r))

    @pl.when((j >= BLK_BQK0) & (j < BLK_BV0))
    def _():
        for c, res in chunks(win_ref):
            for h in range(HPB):
                bqk_ref[h, c * PROJ_ROWS:(c + 1) * PROJ_ROWS, :] = head(res, h).astype(BF16)

    @pl.when((j >= BLK_BV0) & (j < BLK_GATE0))
    def _():
        for c, res in chunks(win_ref):
            for h in range(HPB):
                kw = kw_ref[j - BLK_BV0, h]
                yt = res[:, h * HEAD_DIM:(h + 1) * HEAD_DIM].T
                bvt_ref[h, c, 0:HEAD_DIM, :] = (yt * kw[0:1, :]).astype(BF16)
                bvt_ref[h, c, HEAD_DIM:, :] = kw.astype(BF16)

    @pl.when(j >= BLK_GATE0)
    def _():
        for c, res in chunks(wgate_ref):
            gate_ref[c * PROJ_ROWS:(c + 1) * PROJ_ROWS, :] = _sigmoid(res).astype(BF16)


def _proj(u, w_in, w_gate, gains, norm_flags, key_weights, *, tm=2048):
    t, d = u.shape
    n_gate = w_gate.shape[1] // PROJ_TN
    n_blk = BLK_GATE0 + n_gate
    assert w_in.shape[1] == BLK_GATE0 * PROJ_TN
    n_bv = BLK_GATE0 - BLK_BV0

    def a_spec(gi, r):
        return pl.BlockSpec((HPB, tm // r, r * HEAD_DIM),
                            lambda i, j: (jnp.clip((j - gi + N_DIL - 1) // N_DIL, 0, 2), i, 0))

    a_shapes = [jax.ShapeDtypeStruct((3 * HPB, t // r, r * HEAD_DIM), BF16) for _, r in DIL_PATTERNS]
    out_shape = a_shapes + [
        jax.ShapeDtypeStruct((2 * N_HEADS_B, t, HEAD_DIM), BF16),
        jax.ShapeDtypeStruct((N_HEADS_B, t // MOBA_BLOCK, MOBA_VT_ROWS, MOBA_BLOCK), BF16),
        jax.ShapeDtypeStruct((t, n_gate * PROJ_TN), BF16)]
    out_specs = [a_spec(gi, r) for gi, (_, r) in enumerate(DIL_PATTERNS)] + [
        pl.BlockSpec((HPB, tm, HEAD_DIM),
                     lambda i, j: (jnp.clip(j - BLK_BQK0, 0, BLK_BV0 - BLK_BQK0 - 1), i, 0)),
        pl.BlockSpec((HPB, tm // MOBA_BLOCK, MOBA_VT_ROWS, MOBA_BLOCK),
                     lambda i, j: (jnp.clip(j - BLK_BV0, 0, n_bv - 1), i, 0, 0)),
        pl.BlockSpec((tm, PROJ_TN), lambda i, j: (i, jnp.clip(j - BLK_GATE0, 0, n_gate - 1)))]
    return pl.pallas_call(
        _proj_kernel,
        grid=(t // tm, n_blk),
        in_specs=[pl.BlockSpec(memory_space=pltpu.SMEM),
                  pl.BlockSpec((tm, d), lambda i, j: (i, 0)),
                  pl.BlockSpec((d, PROJ_TN), lambda i, j: (0, jnp.minimum(j, BLK_GATE0 - 1))),
                  pl.BlockSpec((d, PROJ_TN), lambda i, j: (0, jnp.clip(j - BLK_GATE0, 0, n_gate - 1))),
                  pl.BlockSpec((n_blk, HPB, HEAD_DIM), lambda i, j: (0, 0, 0)),
                  pl.BlockSpec((n_bv, HPB, MOBA_VT_PAD, MOBA_BLOCK), lambda i, j: (0, 0, 0, 0))],
        out_specs=out_specs,
        out_shape=out_shape,
        scratch_shapes=[pltpu.VMEM((2, HPB, PROJ_ROWS, HEAD_DIM), F32)],
        compiler_params=_params("arbitrary", "arbitrary"),
        name="proj",
    )(norm_flags, u, w_in, w_gate, gains.reshape(n_blk, HPB, HEAD_DIM),
      key_weights.reshape(n_bv, HPB, MOBA_VT_PAD, MOBA_BLOCK))


DIL_TOKENS = 2048


def _band_block(q, k2, v2, bias):
    s = lax.dot_general(q, k2, _NT, preferred_element_type=F32) + bias
    m = jnp.max(s, axis=-1, keepdims=True)
    p = jnp.exp2(s - m)
    denom = jnp.sum(p, axis=-1, keepdims=True)
    o = jnp.dot(p.astype(BF16), v2, preferred_element_type=F32) / denom
    lse = jnp.broadcast_to(m + jnp.log2(denom), (BAND, LANES))
    return o, lse


def _dilated_kernel(slopes_ref,
                    q0, k0, v0, kp0, vp0,
                    q1, k1, v1, kp1, vp1,
                    q2, k2, v2, kp2, vp2,
                    y_ref, o_scr, l_scr):
    j = pl.program_id(0)
    b = pl.program_id(1)

    qi = lax.broadcasted_iota(jnp.int32, (BAND, 2 * BAND), 0)
    ki = lax.broadcasted_iota(jnp.int32, (BAND, 2 * BAND), 1)
    dist = BAND + qi - ki
    in_band = (dist >= 0) & (dist <= BAND)
    dist_f = dist.astype(F32)
    first_ok = (ki + jnp.minimum(b, 1) * BAND) >= BAND

    groups = ((q0, k0, v0, kp0, vp0), (q1, k1, v1, kp1, vp1), (q2, k2, v2, kp2, vp2))
    for g, (_, r) in enumerate(DIL_PATTERNS):
        q_ref, k_ref, v_ref, kp_ref, vp_ref = groups[g]
        slope = slopes_ref[g * HEADS_PER_DIL_GROUP + j]
        bias = jnp.where(in_band, dist_f * (-slope * r), NEG_INF)
        bias_first = jnp.where(first_ok, bias, NEG_INF)
        n_blk = DIL_TOKENS // (r * BAND)
        for rho in range(r):
            cols = slice(rho * HEAD_DIM, (rho + 1) * HEAD_DIM)

            def store(i, o, lse, g=g, r=r, rho=rho):
                rows = pl.ds(i * (BAND * r) + rho, BAND, stride=r) if r > 1 else pl.ds(i * BAND, BAND)
                o_scr.at[g][rows, :] = o
                l_scr.at[g][rows, :] = lse

            kk = jnp.concatenate([kp_ref[:, cols], k_ref[0:BAND, cols]], axis=0)
            vv = jnp.concatenate([vp_ref[:, cols], v_ref[0:BAND, cols]], axis=0)
            o, lse = _band_block(q_ref[0:BAND, cols], kk, vv, bias_first)
            store(0, o, lse)

            for i in range(1, n_blk):
                o, lse = _band_block(q_ref[i * BAND:(i + 1) * BAND, cols],
                                     k_ref[(i - 1) * BAND:(i + 1) * BAND, cols],
                                     v_ref[(i - 1) * BAND:(i + 1) * BAND, cols], bias)
                store(i, o, lse)

    l0, l1, l2 = l_scr[0], l_scr[1], l_scr[2]
    mx = jnp.maximum(jnp.maximum(l0, l1), l2)
    e0, e1, e2 = jnp.exp2(l0 - mx), jnp.exp2(l1 - mx), jnp.exp2(l2 - mx)
    tot = e0 + e1 + e2
    y = (e0 / tot) * o_scr[0] + (e1 / tot) * o_scr[1] + (e2 / tot) * o_scr[2]
    y_ref[...] = y.astype(BF16)


def _dilated(a_views, slopes):
    t = a_views[0].shape[1]
    dh = HEAD_DIM
    nb = t // DIL_TOKENS
    hq, hk, hv = 0, HPB, 2 * HPB
    args, specs = [], []
    for g, (_, r) in enumerate(DIL_PATTERNS):
        rows = DIL_TOKENS // r
        prev_per_blk = rows // BAND
        width = r * dh

        def cur(base, rows=rows, width=width):
            return pl.BlockSpec((None, rows, width), lambda j, b: (base + j, b, 0))

        def prev(base, width=width, ppb=prev_per_blk):
            return pl.BlockSpec((None, BAND, width),
                                lambda j, b: (base + j, jnp.maximum(b * ppb - 1, 0), 0))

        args += [a_views[g]] * 5
        specs += [cur(hq), cur(hk), cur(hv), prev(hk), prev(hv)]

    return pl.pallas_call(
        _dilated_kernel,
        grid=(HEADS_PER_DIL_GROUP, nb),
        in_specs=[pl.BlockSpec(memory_space=pltpu.SMEM)] + specs,
        out_specs=pl.BlockSpec((DIL_TOKENS, dh), lambda j, b: (b, j)),
        out_shape=jax.ShapeDtypeStruct((t, HEADS_PER_DIL_GROUP * dh), BF16),
        scratch_shapes=[pltpu.VMEM((N_DIL, DIL_TOKENS, dh), F32),
                        pltpu.VMEM((N_DIL, DIL_TOKENS, LANES), F32)],
        compiler_params=_params("arbitrary", "arbitrary"),
        name="dilated",
    )(slopes, *args)


MOBA_UNROLL = 4


MOBA_HEADS_PER_STEP = 4


def _moba_kernel(slopes_ref, q_ref, k_ref, vt_ref, y_ref, kmean_scr, sel_scr, acc_scr, s_scr, *, n_blocks):
    hg = pl.program_id(0)
    qb = pl.program_id(1)
    blk = MOBA_BLOCK
    n_h, _, dh = q_ref.shape
    heads = range(n_h)
    slope = [slopes_ref[N_HEADS_A + hg * n_h + hh] for hh in heads]

    @pl.when(qb == 0)
    def _():
        def body(n, carry):
            for hh in heads:
                kn = k_ref[hh, pl.ds(pl.multiple_of(n * blk, blk), blk), :].astype(F32)
                kmean_scr[hh, pl.ds(n, 1), :] = jnp.mean(kn, axis=0, keepdims=True)
            return carry
        lax.fori_loop(0, n_blocks, body, 0)

    q = [q_ref[hh] for hh in heads]

    blk_id = lax.broadcasted_iota(jnp.int32, (n_blocks, blk), 0)
    blk_id_f = blk_id.astype(F32)
    past = blk_id < qb
    key_i = lax.broadcasted_iota(jnp.int32, (blk, blk), 0)
    qry_i = lax.broadcasted_iota(jnp.int32, (blk, blk), 1)
    own = pl.multiple_of(qb * blk, blk)

    in_block_max = [slope[hh] * (blk - 1) for hh in heads]

    m0 = []
    for hh in heads:
        gate = lax.dot_general(kmean_scr[hh], q[hh].astype(F32), _NT,
                               precision=lax.Precision.HIGHEST, preferred_element_type=F32)
        work = jnp.where(past, gate, NEG_INF)
        sel = jnp.zeros((n_blocks, blk), F32)
        for _ in range(MOBA_TOPK):
            mx = jnp.max(work, axis=0, keepdims=True)
            first = jnp.min(jnp.where(work == mx, blk_id_f, float(n_blocks)), axis=0, keepdims=True)
            hit = blk_id_f == first
            sel = jnp.where(hit, 1.0, sel)
            work = jnp.where(hit, -jnp.inf, work)
        sel_scr[hh] = jnp.where(past, sel, 0.0)

        s = lax.dot_general(k_ref[hh, pl.ds(own, blk), :], q[hh], _NT, preferred_element_type=F32)
        s = jnp.where(key_i <= qry_i, s, NEG_INF)
        m0.append(jnp.max(s, axis=0, keepdims=True) + in_block_max[hh])
        p = jnp.exp2(s - m0[hh])
        acc_scr[hh] = jnp.dot(vt_ref[hh, qb], p.astype(BF16), preferred_element_type=F32)

    width = MOBA_UNROLL
    n_chunks = (qb + width - 1) // width

    def chunk_scores(hh, ci, slot):
        start = pl.multiple_of(ci * (width * blk), width * blk)
        s_scr[slot, hh] = lax.dot_general(k_ref[hh, pl.ds(start, width * blk), :], q[hh], _NT,
                                          preferred_element_type=F32)

    def past_chunk(ci, m_run, src, dst):
        n0 = ci * width
        nxt = jnp.minimum(ci + 1, n_chunks - 1)
        out = []
        for hh in heads:
            chunk_scores(hh, nxt, dst)
            c, chosen = [], []
            m_chunk = jnp.full((1, blk), NEG_INF, F32)
            for a in range(width):
                s = s_scr[src, hh, a * blk:(a + 1) * blk, :]
                c.append(slope[hh] * jnp.full((1, blk), (n0 + a - qb) * blk, jnp.int32).astype(F32))
                chosen.append(sel_scr[hh, pl.ds(n0 + a, 1), :] > 0.5)
                m_blk = jnp.max(s, axis=0, keepdims=True) + (c[a] + in_block_max[hh])
                m_chunk = jnp.maximum(m_chunk, jnp.where(chosen[a], m_blk, NEG_INF))
            pv = jnp.zeros(acc_scr.shape[1:], F32)
            for a in range(width):
                s = s_scr[src, hh, a * blk:(a + 1) * blk, :]
                p = jnp.exp2(s - jnp.where(chosen[a], m_chunk - c[a], -NEG_INF))
                pv = pv + jnp.dot(vt_ref[hh, n0 + a], p.astype(BF16), preferred_element_type=F32)
            m_new = jnp.maximum(m_run[hh], m_chunk)
            alpha = jnp.exp2(m_run[hh] - m_new)
            beta = jnp.exp2(m_chunk - m_new)
            acc_scr[hh] = alpha * acc_scr[hh] + beta * pv
            out.append(m_new)
        return tuple(out)

    for hh in heads:
        chunk_scores(hh, 0, 0)

    def body(ci, m_run):
        return lax.cond(lax.rem(ci, 2) == 0,
                        lambda m: past_chunk(ci, m, 0, 1), lambda m: past_chunk(ci, m, 1, 0), m_run)

    lax.fori_loop(0, n_chunks, body, tuple(m0))
    for hh in heads:
        y = acc_scr[hh, 0:dh, :] / acc_scr[hh, dh:dh + 1, :]
        y_ref[:, hh * dh:(hh + 1) * dh] = y.T.astype(BF16)


def _moba(bqk, bvt, slopes):
    _, t, dh = bqk.shape
    n_blocks = t // MOBA_BLOCK
    assert n_blocks % MOBA_UNROLL == 0, "the padded last chunk must stay inside the key array"
    n_h = MOBA_HEADS_PER_STEP
    n_groups = N_HEADS_B // n_h
    return pl.pallas_call(
        functools.partial(_moba_kernel, n_blocks=n_blocks),
        grid=(n_groups, n_blocks),
        in_specs=[pl.BlockSpec(memory_space=pltpu.SMEM),
                  pl.BlockSpec((n_h, MOBA_BLOCK, dh), lambda h, i: (h, i, 0)),
                  pl.BlockSpec((n_h, t, dh), lambda h, i: (n_groups + h, 0, 0)),
                  pl.BlockSpec((n_h, n_blocks, MOBA_VT_ROWS, MOBA_BLOCK), lambda h, i: (h, 0, 0, 0))],
        out_specs=pl.BlockSpec((MOBA_BLOCK, n_h * dh), lambda h, i: (i, h)),
        out_shape=jax.ShapeDtypeStruct((t, N_HEADS_B * dh), BF16),
        scratch_shapes=[pltpu.VMEM((n_h, n_blocks, dh), F32),
                        pltpu.VMEM((n_h, n_blocks, MOBA_BLOCK), F32),
                        pltpu.VMEM((n_h, MOBA_VT_ROWS, MOBA_BLOCK), F32),
                        pltpu.VMEM((2, n_h, MOBA_UNROLL * MOBA_BLOCK, MOBA_BLOCK), F32)],
        compiler_params=_params("arbitrary", "arbitrary"),
        name="moba",
    )(slopes, bqk, bqk, bvt)


def _merge_kernel(x_ref, ada_ref, ya_ref, yb_ref, gates_a_ref, gates_b_ref, wa_ref, wb_ref, wo_ref, o_ref):
    pa = jnp.dot(ya_ref[...], wa_ref[...], preferred_element_type=F32)
    pb = jnp.dot(yb_ref[...], wb_ref[...], preferred_element_type=F32)
    merged = gates_a_ref[...].astype(F32) * pa + gates_b_ref[...].astype(F32) * pb
    out = jnp.dot(merged.astype(BF16), wo_ref[...], preferred_element_type=F32)
    o_ref[...] = x_ref[...] + ada_ref[5:6, :] * out


def _merge(x, ada, y_a, y_b, gates, w_a, w_b, w_o, *, tm=256):
    t, d = x.shape
    const = lambda i: (0, 0)
    return pl.pallas_call(
        _merge_kernel,
        grid=(t // tm,),
        in_specs=[pl.BlockSpec((tm, d), lambda i: (i, 0)),
                  pl.BlockSpec((N_ADA, d), const),
                  pl.BlockSpec((tm, y_a.shape[1]), lambda i: (i, 0)),
                  pl.BlockSpec((tm, y_b.shape[1]), lambda i: (i, 0)),
                  pl.BlockSpec((tm, d), lambda i: (i, 0)),
                  pl.BlockSpec((tm, d), lambda i: (i, 1)),
                  pl.BlockSpec(w_a.shape, const),
                  pl.BlockSpec(w_b.shape, const),
                  pl.BlockSpec(w_o.shape, const)],
        out_specs=pl.BlockSpec((tm, d), lambda i: (i, 0)),
        out_shape=jax.ShapeDtypeStruct((t, d), F32),
        compiler_params=_params("parallel"),
        name="merge",
    )(x, ada, y_a, y_b, gates, gates, w_a, w_b, w_o)


def _layer(x, c, w_ada, b_ada, g_ffn1, ffn1_w_gate, ffn1_w_up, ffn1_w_down,
           g_mix, w_in, q_norm, k_norm, w_gate, w_branch_a, w_branch_b, w_out,
           g_ffn2, ffn2_w_gate, ffn2_w_up, ffn2_w_down):
    t, d = x.shape
    ada = _ada(c, w_ada, b_ada).reshape(N_ADA, d)
    slopes = jnp.exp2(-8.0 * jnp.arange(1, N_HEADS + 1, dtype=F32) / N_HEADS) * LOG2_E

    x = _ffn(x, ada, g_ffn1, ffn1_w_gate, ffn1_w_up, ffn1_w_down, sub=0)

    scale = HEAD_DIM ** -0.5 * LOG2_E
    n_gate_heads = w_gate.shape[1] // HEAD_DIM
    gains = jnp.concatenate([q_norm[:N_HEADS_A] * scale, k_norm[:N_HEADS_A], jnp.ones((N_HEADS_A, HEAD_DIM), F32),
                             q_norm[N_HEADS_A:] * scale, k_norm[N_HEADS_A:],
                             jnp.ones((N_HEADS_B + n_gate_heads, HEAD_DIM), F32)], axis=0)
    norm_flags = jnp.concatenate([jnp.ones((2 * N_DIL,), F32), jnp.zeros((N_DIL,), F32),
                                  jnp.ones((BLK_BV0 - BLK_BQK0,), F32),
                                  jnp.zeros((BLK_GATE0 - BLK_BV0 + n_gate_heads // HPB,), F32)])
    offsets = jnp.arange(MOBA_BLOCK, dtype=F32)
    key_weights = jnp.exp2(slopes[N_HEADS_A:, None, None] * offsets[None, None, :])
    key_weights = jnp.pad(key_weights, ((0, 0), (0, MOBA_VT_PAD - 1), (0, 0)))
    a0, a1, a2, bqk, bvt, gates = _proj(_norm(x, ada, g_mix, sub=1), w_in.astype(BF16), w_gate.astype(BF16),
                                        gains, norm_flags, key_weights)

    y_a = _dilated((a0, a1, a2), slopes)
    y_b = _moba(bqk, bvt, slopes)

    x = _merge(x, ada, y_a, y_b, gates, w_branch_a.astype(BF16), w_branch_b.astype(BF16), w_out.astype(BF16))
    x = _ffn(x, ada, g_ffn2, ffn2_w_gate, ffn2_w_up, ffn2_w_down, sub=2)
    return x


def kernel(x, c, w_ada, b_ada, g_ffn1, ffn1_w_gate, ffn1_w_up, ffn1_w_down, g_mix, w_in, q_norm, k_norm,
           w_gate, w_branch_a, w_branch_b, w_out, g_ffn2, ffn2_w_gate, ffn2_w_up, ffn2_w_down):
    batch, depth = x.shape[0], w_ada.shape[0]
    outs = []
    for bi in range(batch):
        xb = x[bi]
        for l in range(depth):
            xb = _layer(xb, c[bi], w_ada[l], b_ada[l], g_ffn1[l], ffn1_w_gate[l], ffn1_w_up[l],
                        ffn1_w_down[l], g_mix[l], w_in[l], q_norm[l], k_norm[l], w_gate[l],
                        w_branch_a[l], w_branch_b[l], w_out[l], g_ffn2[l], ffn2_w_gate[l],
                        ffn2_w_up[l], ffn2_w_down[l])
        outs.append(xb)
    return jnp.stack(outs, axis=0)
```

```python
import functools

import jax
import jax.numpy as jnp
from jax import lax
from jax.experimental import pallas as pl
from jax.experimental.pallas import tpu as pltpu

HEAD_DIM = 128
DIL_PATTERNS = ((128, 1), (512, 4), (2048, 16))
N_DIL = len(DIL_PATTERNS)
HEADS_PER_DIL_GROUP = 4
N_HEADS_A = HEADS_PER_DIL_GROUP * N_DIL
N_HEADS_B = 8
N_HEADS = N_HEADS_A + N_HEADS_B
BAND = 128
MOBA_BLOCK = 256
MOBA_TOPK = 3
N_ADA = 9
EPS = 1e-6
NEG_INF = -1e30
LOG2_E = 1.4426950408889634

LANES = 128
VMEM_LIMIT = 60 * 1024 * 1024

BF16 = jnp.bfloat16
F32 = jnp.float32

_NT = (((1,), (1,)), ((), ()))


def _params(*sem):
    return pltpu.CompilerParams(dimension_semantics=sem, vmem_limit_bytes=VMEM_LIMIT)


def _sigmoid(x):
    return 1.0 / (1.0 + jnp.exp(-x))


NORM_ROWS = 128


def _norm_modulate_into(u_ref, row0, x_ref, g, shift, scale):
    gs = g * (1.0 + scale)

    def body(c, carry):
        rows = pl.ds(pl.multiple_of(c * NORM_ROWS, NORM_ROWS), NORM_ROWS)
        x = x_ref[rows, :]
        ms = jnp.mean(x * x, axis=-1, keepdims=True)
        y = (x * lax.rsqrt(ms + EPS)) * gs + shift
        u_ref[pl.ds(pl.multiple_of(row0 + c * NORM_ROWS, NORM_ROWS), NORM_ROWS), :] = y.astype(BF16)
        return carry

    lax.fori_loop(0, x_ref.shape[0] // NORM_ROWS, body, 0)


def _head_rmsnorm(a, gain):
    ms = jnp.mean(a * a, axis=-1, keepdims=True)
    return (a * lax.rsqrt(ms + EPS)) * gain


def _ada_kernel(c_ref, w_ref, b_ref, o_ref):
    c = c_ref[...]
    s = c * _sigmoid(c)
    o_ref[...] = jnp.sum(s * w_ref[...], axis=0, keepdims=True) + b_ref[...]


def _ada(c, w_ada, b_ada, *, tn=1024):
    d, n = w_ada.shape
    return pl.pallas_call(
        _ada_kernel,
        grid=(n // tn,),
        in_specs=[pl.BlockSpec((d, 1), lambda j: (0, 0)),
                  pl.BlockSpec((d, tn), lambda j: (0, j)),
                  pl.BlockSpec((1, tn), lambda j: (0, j))],
        out_specs=pl.BlockSpec((1, tn), lambda j: (0, j)),
        out_shape=jax.ShapeDtypeStruct((1, n), F32),
        compiler_params=_params("arbitrary"),
        name="ada",
    )(c.reshape(d, 1), w_ada, b_ada.reshape(1, n))


def _ffn_kernel(x_ref, ada_ref, g_ref, wg_ref, wu_ref, wd_ref, o_ref, u_ref, *, sub):
    f = pl.program_id(1)

    @pl.when(f == 0)
    def _():
        _norm_modulate_into(u_ref, 0, x_ref, g_ref[...],
                            ada_ref[3 * sub:3 * sub + 1, :], ada_ref[3 * sub + 1:3 * sub + 2, :])
        o_ref[...] = jnp.zeros_like(o_ref)

    u = u_ref[...]
    hg = jnp.dot(u, wg_ref[...].astype(BF16), preferred_element_type=F32)
    hu = jnp.dot(u, wu_ref[...].astype(BF16), preferred_element_type=F32)
    h = (hg * _sigmoid(hg)) * hu
    o_ref[...] += jnp.dot(h.astype(BF16), wd_ref[...].astype(BF16), preferred_element_type=F32)

    @pl.when(f == pl.num_programs(1) - 1)
    def _():
        gate = ada_ref[3 * sub + 2:3 * sub + 3, :]
        o_ref[...] = x_ref[...] + (0.5 * gate) * o_ref[...]


def _ffn(x, ada, g, w_gate, w_up, w_down, *, sub, tm=1024, tf=256):
    t, d = x.shape
    dff = w_gate.shape[1]
    return pl.pallas_call(
        functools.partial(_ffn_kernel, sub=sub),
        grid=(t // tm, dff // tf),
        in_specs=[pl.BlockSpec((tm, d), lambda i, f: (i, 0)),
                  pl.BlockSpec((N_ADA, d), lambda i, f: (0, 0)),
                  pl.BlockSpec((1, d), lambda i, f: (0, 0)),
                  pl.BlockSpec((d, tf), lambda i, f: (0, f)),
                  pl.BlockSpec((d, tf), lambda i, f: (0, f)),
                  pl.BlockSpec((tf, d), lambda i, f: (f, 0))],
        out_specs=pl.BlockSpec((tm, d), lambda i, f: (i, 0)),
        out_shape=jax.ShapeDtypeStruct((t, d), F32),
        scratch_shapes=[pltpu.VMEM((tm, d), BF16)],
        compiler_params=_params("parallel", "arbitrary"),
        name=f"ffn{sub}",
    )(x, ada, g.reshape(1, d), w_gate, w_up, w_down)


def _norm_kernel(x_ref, ada_ref, g_ref, u_ref, *, sub):
    _norm_modulate_into(u_ref, 0, x_ref, g_ref[...],
                        ada_ref[3 * sub:3 * sub + 1, :], ada_ref[3 * sub + 1:3 * sub + 2, :])


def _norm(x, ada, g, *, sub, tm=512):
    t, d = x.shape
    return pl.pallas_call(
        functools.partial(_norm_kernel, sub=sub),
        grid=(t // tm,),
        in_specs=[pl.BlockSpec((tm, d), lambda i: (i, 0)),
                  pl.BlockSpec((N_ADA, d), lambda i: (0, 0)),
                  pl.BlockSpec((1, d), lambda i: (0, 0))],
        out_specs=pl.BlockSpec((tm, d), lambda i: (i, 0)),
        out_shape=jax.ShapeDtypeStruct((t, d), BF16),
        compiler_params=_params("parallel"),
        name="norm",
    )(x, ada, g.reshape(1, d))


HPB = HEADS_PER_DIL_GROUP
PROJ_TN = HPB * HEAD_DIM
BLK_A_END = 3 * N_DIL
BLK_BQK0 = BLK_A_END
BLK_BV0 = BLK_BQK0 + 2 * (N_HEADS_B // HPB)
BLK_GATE0 = BLK_BV0 + N_HEADS_B // HPB


MOBA_VT_PAD = 16
MOBA_VT_ROWS = HEAD_DIM + MOBA_VT_PAD


PROJ_ROWS = MOBA_BLOCK


def _proj_kernel(nflag_ref, u_ref, win_ref, wgate_ref, gain_ref, kw_ref,
                 a0_ref, a1_ref, a2_ref, bqk_ref, bvt_ref, gate_ref, stg_ref, stg2_ref):
    j = pl.program_id(1)
    n_chunks = u_ref.shape[0] // PROJ_ROWS

    def chunks(w_ref):
        w = w_ref[...].astype(BF16)
        for c in range(n_chunks):
            rows = slice(c * PROJ_ROWS, (c + 1) * PROJ_ROWS)
            yield c, jnp.dot(u_ref[rows, :], w, preferred_element_type=F32)

    def head(res, h):
        a = res[:, h * HEAD_DIM:(h + 1) * HEAD_DIM]
        flag = nflag_ref[j]
        ms = jnp.mean(a * a, axis=-1, keepdims=True)
        return (a * (lax.rsqrt(ms + EPS) * flag + (1.0 - flag))) * gain_ref[j, h:h + 1, :]

    def dilated_block(o_ref, r):
        n = PROJ_ROWS // r
        for c, res in chunks(win_ref):
            for h in range(HPB):
                y = head(res, h)
                if r == 1:
                    o_ref[h, c * n:(c + 1) * n, :] = y.astype(BF16)
                    continue
                stg = stg_ref.at[c % 2, h]
                stg[...] = y
                if r == 16:
                    stg2 = stg2_ref.at[c % 2, h]
                    m = PROJ_ROWS // 4
                    for a in range(4):
                        stg2[a * m:(a + 1) * m, :] = stg[pl.ds(a, m, stride=4), :]
                    parts = {a + 4 * b: stg2[pl.ds(a * m + b, n, stride=4), :] for a in range(4) for b in range(4)}
                else:
                    parts = {rho: stg[pl.ds(rho, n, stride=r), :] for rho in range(r)}
                for rho in range(r):
                    o_ref[h, c * n:(c + 1) * n, rho * HEAD_DIM:(rho + 1) * HEAD_DIM] = parts[rho].astype(BF16)

    a_refs = (a0_ref, a1_ref, a2_ref)
    for g, (_, r) in enumerate(DIL_PATTERNS):
        pl.when((j < BLK_A_END) & (lax.rem(j, N_DIL) == g))(functools.partial(dilated_block, a_refs[g], r))

    @pl.when((j >= BLK_BQK0) & (j < BLK_BV0))
    def _():
        for c, res in chunks(win_ref):
            for h in range(HPB):
                bqk_ref[h, c * PROJ_ROWS:(c + 1) * PROJ_ROWS, :] = head(res, h).astype(BF16)

    @pl.when((j >= BLK_BV0) & (j < BLK_GATE0))
    def _():
        for c, res in chunks(win_ref):
            for h in range(HPB):
                kw = kw_ref[j - BLK_BV0, h]
                yt = res[:, h * HEAD_DIM:(h + 1) * HEAD_DIM].T
                bvt_ref[h, c, 0:HEAD_DIM, :] = (yt * kw[0:1, :]).astype(BF16)
                bvt_ref[h, c, HEAD_DIM:, :] = kw.astype(BF16)

    @pl.when(j >= BLK_GATE0)
    def _():
        for c, res in chunks(wgate_ref):
            gate_ref[c * PROJ_ROWS:(c + 1) * PROJ_ROWS, :] = _sigmoid(res).astype(BF16)


def _proj(u, w_in, w_gate, gains, norm_flags, key_weights, *, tm=2048):
    t, d = u.shape
    n_gate = w_gate.shape[1] // PROJ_TN
    n_blk = BLK_GATE0 + n_gate
    assert w_in.shape[1] == BLK_GATE0 * PROJ_TN
    n_bv = BLK_GATE0 - BLK_BV0

    def a_spec(gi, r):
        return pl.BlockSpec((HPB, tm // r, r * HEAD_DIM),
                            lambda i, j: (jnp.clip((j - gi + N_DIL - 1) // N_DIL, 0, 2), i, 0))

    a_shapes = [jax.ShapeDtypeStruct((3 * HPB, t // r, r * HEAD_DIM), BF16) for _, r in DIL_PATTERNS]
    out_shape = a_shapes + [
        jax.ShapeDtypeStruct((2 * N_HEADS_B, t, HEAD_DIM), BF16),
        jax.ShapeDtypeStruct((N_HEADS_B, t // MOBA_BLOCK, MOBA_VT_ROWS, MOBA_BLOCK), BF16),
        jax.ShapeDtypeStruct((t, n_gate * PROJ_TN), BF16)]
    out_specs = [a_spec(gi, r) for gi, (_, r) in enumerate(DIL_PATTERNS)] + [
        pl.BlockSpec((HPB, tm, HEAD_DIM),
                     lambda i, j: (jnp.clip(j - BLK_BQK0, 0, BLK_BV0 - BLK_BQK0 - 1), i, 0)),
        pl.BlockSpec((HPB, tm // MOBA_BLOCK, MOBA_VT_ROWS, MOBA_BLOCK),
                     lambda i, j: (jnp.clip(j - BLK_BV0, 0, n_bv - 1), i, 0, 0)),
        pl.BlockSpec((tm, PROJ_TN), lambda i, j: (i, jnp.clip(j - BLK_GATE0, 0, n_gate - 1)))]
    return pl.pallas_call(
        _proj_kernel,
        grid=(t // tm, n_blk),
        in_specs=[pl.BlockSpec(memory_space=pltpu.SMEM),
                  pl.BlockSpec((tm, d), lambda i, j: (i, 0), pipeline_mode=pl.Buffered(1)),
                  pl.BlockSpec((d, PROJ_TN), lambda i, j: (0, jnp.minimum(j, BLK_GATE0 - 1))),
                  pl.BlockSpec((d, PROJ_TN), lambda i, j: (0, jnp.clip(j - BLK_GATE0, 0, n_gate - 1))),
                  pl.BlockSpec((n_blk, HPB, HEAD_DIM), lambda i, j: (0, 0, 0)),
                  pl.BlockSpec((n_bv, HPB, MOBA_VT_PAD, MOBA_BLOCK), lambda i, j: (0, 0, 0, 0))],
        out_specs=out_specs,
        out_shape=out_shape,
        scratch_shapes=[pltpu.VMEM((2, HPB, PROJ_ROWS, HEAD_DIM), F32)] * 2,
        compiler_params=_params("arbitrary", "arbitrary"),
        name="proj",
    )(norm_flags, u, w_in, w_gate, gains.reshape(n_blk, HPB, HEAD_DIM),
      key_weights.reshape(n_bv, HPB, MOBA_VT_PAD, MOBA_BLOCK))


DIL_TOKENS = 2048


def _band_block(q, k2, v2, bias):
    s = lax.dot_general(q, k2, _NT, preferred_element_type=F32) + bias
    m = jnp.max(s, axis=-1, keepdims=True)
    p = jnp.exp2(s - m)
    denom = jnp.sum(p, axis=-1, keepdims=True)
    o = jnp.dot(p.astype(BF16), v2, preferred_element_type=F32) / denom
    lse = jnp.broadcast_to(m + jnp.log2(denom), (BAND, LANES))
    return o, lse


def _dilated_kernel(slopes_ref,
                    q0, k0, v0, kp0, vp0,
                    q1, k1, v1, kp1, vp1,
                    q2, k2, v2, kp2, vp2,
                    y_ref, o_scr, l_scr):
    j = pl.program_id(0)
    b = pl.program_id(1)

    qi = lax.broadcasted_iota(jnp.int32, (BAND, 2 * BAND), 0)
    ki = lax.broadcasted_iota(jnp.int32, (BAND, 2 * BAND), 1)
    dist = BAND + qi - ki
    in_band = (dist >= 0) & (dist <= BAND)
    dist_f = dist.astype(F32)
    first_ok = (ki + jnp.minimum(b, 1) * BAND) >= BAND

    groups = ((q0, k0, v0, kp0, vp0), (q1, k1, v1, kp1, vp1), (q2, k2, v2, kp2, vp2))
    for g, (_, r) in enumerate(DIL_PATTERNS):
        q_ref, k_ref, v_ref, kp_ref, vp_ref = groups[g]
        slope = slopes_ref[g * HEADS_PER_DIL_GROUP + j]
        bias = jnp.where(in_band, dist_f * (-slope * r), NEG_INF)
        bias_first = jnp.where(first_ok, bias, NEG_INF)
        n_blk = DIL_TOKENS // (r * BAND)
        for rho in range(r):
            cols = slice(rho * HEAD_DIM, (rho + 1) * HEAD_DIM)

            def store(i, o, lse, g=g, r=r, rho=rho):
                rows = pl.ds(i * (BAND * r) + rho, BAND, stride=r) if r > 1 else pl.ds(i * BAND, BAND)
                o_scr.at[g][rows, :] = o
                l_scr.at[g][rows, :] = lse

            kk = jnp.concatenate([kp_ref[:, cols], k_ref[0:BAND, cols]], axis=0)
            vv = jnp.concatenate([vp_ref[:, cols], v_ref[0:BAND, cols]], axis=0)
            o, lse = _band_block(q_ref[0:BAND, cols], kk, vv, bias_first)
            store(0, o, lse)

            for i in range(1, n_blk):
                o, lse = _band_block(q_ref[i * BAND:(i + 1) * BAND, cols],
                                     k_ref[(i - 1) * BAND:(i + 1) * BAND, cols],
                                     v_ref[(i - 1) * BAND:(i + 1) * BAND, cols], bias)
                store(i, o, lse)

    l0, l1, l2 = l_scr[0], l_scr[1], l_scr[2]
    mx = jnp.maximum(jnp.maximum(l0, l1), l2)
    e0, e1, e2 = jnp.exp2(l0 - mx), jnp.exp2(l1 - mx), jnp.exp2(l2 - mx)
    tot = e0 + e1 + e2
    y = (e0 / tot) * o_scr[0] + (e1 / tot) * o_scr[1] + (e2 / tot) * o_scr[2]
    y_ref[...] = y.astype(BF16)


def _dilated(a_views, slopes):
    t = a_views[0].shape[1]
    dh = HEAD_DIM
    nb = t // DIL_TOKENS
    hq, hk, hv = 0, HPB, 2 * HPB
    args, specs = [], []
    for g, (_, r) in enumerate(DIL_PATTERNS):
        rows = DIL_TOKENS // r
        prev_per_blk = rows // BAND
        width = r * dh

        def cur(base, rows=rows, width=width):
            return pl.BlockSpec((None, rows, width), lambda j, b: (base + j, b, 0))

        def prev(base, width=width, ppb=prev_per_blk):
            return pl.BlockSpec((None, BAND, width),
                                lambda j, b: (base + j, jnp.maximum(b * ppb - 1, 0), 0))

        args += [a_views[g]] * 5
        specs += [cur(hq), cur(hk), cur(hv), prev(hk), prev(hv)]

    return pl.pallas_call(
        _dilated_kernel,
        grid=(HEADS_PER_DIL_GROUP, nb),
        in_specs=[pl.BlockSpec(memory_space=pltpu.SMEM)] + specs,
        out_specs=pl.BlockSpec((DIL_TOKENS, dh), lambda j, b: (b, j)),
        out_shape=jax.ShapeDtypeStruct((t, HEADS_PER_DIL_GROUP * dh), BF16),
        scratch_shapes=[pltpu.VMEM((N_DIL, DIL_TOKENS, dh), F32),
                        pltpu.VMEM((N_DIL, DIL_TOKENS, LANES), F32)],
        compiler_params=_params("arbitrary", "arbitrary"),
        name="dilated",
    )(slopes, *args)


MOBA_UNROLL = 4


MOBA_HEADS_PER_STEP = 4


def _moba_kernel(slopes_ref, q_ref, k_ref, vt_ref, y_ref, kmean_scr, sel_scr, acc_scr, s_scr, *, n_blocks):
    hg = pl.program_id(0)
    qb = pl.program_id(1)
    blk = MOBA_BLOCK
    n_h, _, dh = q_ref.shape
    heads = range(n_h)
    slope = [slopes_ref[N_HEADS_A + hg * n_h + hh] for hh in heads]

    @pl.when(qb == 0)
    def _():
        def body(n, carry):
            for hh in heads:
                kn = k_ref[hh, pl.ds(pl.multiple_of(n * blk, blk), blk), :].astype(F32)
                kmean_scr[hh, pl.ds(n, 1), :] = jnp.mean(kn, axis=0, keepdims=True)
            return carry
        lax.fori_loop(0, n_blocks, body, 0)

    q = [q_ref[hh] for hh in heads]

    blk_id = lax.broadcasted_iota(jnp.int32, (n_blocks, blk), 0)
    blk_id_f = blk_id.astype(F32)
    past = blk_id < qb
    key_i = lax.broadcasted_iota(jnp.int32, (blk, blk), 0)
    qry_i = lax.broadcasted_iota(jnp.int32, (blk, blk), 1)
    own = pl.multiple_of(qb * blk, blk)

    in_block_max = [slope[hh] * (blk - 1) for hh in heads]

    width = MOBA_UNROLL
    n_chunks = (qb + width - 1) // width

    def chunk_scores(hh, ci, slot):
        start = pl.multiple_of(ci * (width * blk), width * blk)
        s_scr[slot, hh] = lax.dot_general(k_ref[hh, pl.ds(start, width * blk), :], q[hh], _NT,
                                          preferred_element_type=F32)

    gates = [lax.dot_general(kmean_scr[hh], q[hh].astype(F32), _NT,
                             precision=lax.Precision.HIGHEST, preferred_element_type=F32) for hh in heads]
    s_own = [lax.dot_general(k_ref[hh, pl.ds(own, blk), :], q[hh], _NT, preferred_element_type=F32)
             for hh in heads]
    for hh in heads:
        chunk_scores(hh, 0, 0)

    m0, p_own = [], []
    for hh in heads:
        work = jnp.where(past, gates[hh], NEG_INF)
        sel = jnp.zeros((n_blocks, blk), F32)
        for _ in range(MOBA_TOPK):
            mx = jnp.max(work, axis=0, keepdims=True)
            first = jnp.min(jnp.where(work == mx, blk_id_f, float(n_blocks)), axis=0, keepdims=True)
            hit = blk_id_f == first
            sel = jnp.where(hit, 1.0, sel)
            work = jnp.where(hit, -jnp.inf, work)
        sel_scr[hh] = jnp.where(past, sel, 0.0)

        s = jnp.where(key_i <= qry_i, s_own[hh], NEG_INF)
        m0.append(jnp.max(s, axis=0, keepdims=True) + in_block_max[hh])
        p_own.append(jnp.exp2(s - m0[hh]).astype(BF16))
    for hh in heads:
        acc_scr[hh] = jnp.dot(vt_ref[hh, qb], p_own[hh], preferred_element_type=F32)

    def past_chunk(ci, m_run, src, dst):
        n0 = ci * width
        nxt = jnp.minimum(ci + 1, n_chunks - 1)
        out = []
        for hh in heads:
            chunk_scores(hh, nxt, dst)
            c, chosen = [], []
            m_chunk = jnp.full((1, blk), NEG_INF, F32)
            for a in range(width):
                s = s_scr[src, hh, a * blk:(a + 1) * blk, :]
                c.append(slope[hh] * jnp.full((1, blk), (n0 + a - qb) * blk, jnp.int32).astype(F32))
                chosen.append(sel_scr[hh, pl.ds(n0 + a, 1), :] > 0.5)
                m_blk = jnp.max(s, axis=0, keepdims=True) + (c[a] + in_block_max[hh])
                m_chunk = jnp.maximum(m_chunk, jnp.where(chosen[a], m_blk, NEG_INF))
            pv = jnp.zeros(acc_scr.shape[1:], F32)
            for a in range(width):
                s = s_scr[src, hh, a * blk:(a + 1) * blk, :]
                p = jnp.exp2(s - jnp.where(chosen[a], m_chunk - c[a], -NEG_INF))
                pv = pv + jnp.dot(vt_ref[hh, n0 + a], p.astype(BF16), preferred_element_type=F32)
            m_new = jnp.maximum(m_run[hh], m_chunk)
            alpha = jnp.exp2(m_run[hh] - m_new)
            beta = jnp.exp2(m_chunk - m_new)
            acc_scr[hh] = alpha * acc_scr[hh] + beta * pv
            out.append(m_new)
        return tuple(out)

    def body(ci, m_run):
        return lax.cond(lax.rem(ci, 2) == 0,
                        lambda m: past_chunk(ci, m, 0, 1), lambda m: past_chunk(ci, m, 1, 0), m_run)

    lax.fori_loop(0, n_chunks, body, tuple(m0))
    for hh in heads:
        y = acc_scr[hh, 0:dh, :] / acc_scr[hh, dh:dh + 1, :]
        y_ref[:, hh * dh:(hh + 1) * dh] = y.T.astype(BF16)


def _moba(bqk, bvt, slopes):
    _, t, dh = bqk.shape
    n_blocks = t // MOBA_BLOCK
    assert n_blocks % MOBA_UNROLL == 0, "the padded last chunk must stay inside the key array"
    n_h = MOBA_HEADS_PER_STEP
    n_groups = N_HEADS_B // n_h
    return pl.pallas_call(
        functools.partial(_moba_kernel, n_blocks=n_blocks),
        grid=(n_groups, n_blocks),
        in_specs=[pl.BlockSpec(memory_space=pltpu.SMEM),
                  pl.BlockSpec((n_h, MOBA_BLOCK, dh), lambda h, i: (h, i, 0)),
                  pl.BlockSpec((n_h, t, dh), lambda h, i: (n_groups + h, 0, 0)),
                  pl.BlockSpec((n_h, n_blocks, MOBA_VT_ROWS, MOBA_BLOCK), lambda h, i: (h, 0, 0, 0))],
        out_specs=pl.BlockSpec((MOBA_BLOCK, n_h * dh), lambda h, i: (i, h)),
        out_shape=jax.ShapeDtypeStruct((t, N_HEADS_B * dh), BF16),
        scratch_shapes=[pltpu.VMEM((n_h, n_blocks, dh), F32),
                        pltpu.VMEM((n_h, n_blocks, MOBA_BLOCK), F32),
                        pltpu.VMEM((n_h, MOBA_VT_ROWS, MOBA_BLOCK), F32),
                        pltpu.VMEM((2, n_h, MOBA_UNROLL * MOBA_BLOCK, MOBA_BLOCK), F32)],
        compiler_params=_params("arbitrary", "arbitrary"),
        name="moba",
    )(slopes, bqk, bqk, bvt)


def _merge_kernel(x_ref, ada_ref, ya_ref, yb_ref, gates_a_ref, gates_b_ref, wa_ref, wb_ref, wo_ref, o_ref):
    pa = jnp.dot(ya_ref[...], wa_ref[...], preferred_element_type=F32)
    pb = jnp.dot(yb_ref[...], wb_ref[...], preferred_element_type=F32)
    merged = gates_a_ref[...].astype(F32) * pa + gates_b_ref[...].astype(F32) * pb
    out = jnp.dot(merged.astype(BF16), wo_ref[...], preferred_element_type=F32)
    o_ref[...] = x_ref[...] + ada_ref[5:6, :] * out


def _merge(x, ada, y_a, y_b, gates, w_a, w_b, w_o, *, tm=256):
    t, d = x.shape
    const = lambda i: (0, 0)
    return pl.pallas_call(
        _merge_kernel,
        grid=(t // tm,),
        in_specs=[pl.BlockSpec((tm, d), lambda i: (i, 0)),
                  pl.BlockSpec((N_ADA, d), const),
                  pl.BlockSpec((tm, y_a.shape[1]), lambda i: (i, 0)),
                  pl.BlockSpec((tm, y_b.shape[1]), lambda i: (i, 0)),
                  pl.BlockSpec((tm, d), lambda i: (i, 0)),
                  pl.BlockSpec((tm, d), lambda i: (i, 1)),
                  pl.BlockSpec(w_a.shape, const),
                  pl.BlockSpec(w_b.shape, const),
                  pl.BlockSpec(w_o.shape, const)],
        out_specs=pl.BlockSpec((tm, d), lambda i: (i, 0)),
        out_shape=jax.ShapeDtypeStruct((t, d), F32),
        compiler_params=_params("parallel"),
        name="merge",
    )(x, ada, y_a, y_b, gates, gates, w_a, w_b, w_o)


def _layer(x, c, w_ada, b_ada, g_ffn1, ffn1_w_gate, ffn1_w_up, ffn1_w_down,
           g_mix, w_in, q_norm, k_norm, w_gate, w_branch_a, w_branch_b, w_out,
           g_ffn2, ffn2_w_gate, ffn2_w_up, ffn2_w_down):
    t, d = x.shape
    ada = _ada(c, w_ada, b_ada).reshape(N_ADA, d)
    slopes = jnp.exp2(-8.0 * jnp.arange(1, N_HEADS + 1, dtype=F32) / N_HEADS) * LOG2_E

    x = _ffn(x, ada, g_ffn1, ffn1_w_gate, ffn1_w_up, ffn1_w_down, sub=0)

    scale = HEAD_DIM ** -0.5 * LOG2_E
    n_gate_heads = w_gate.shape[1] // HEAD_DIM
    gains = jnp.concatenate([q_norm[:N_HEADS_A] * scale, k_norm[:N_HEADS_A], jnp.ones((N_HEADS_A, HEAD_DIM), F32),
                             q_norm[N_HEADS_A:] * scale, k_norm[N_HEADS_A:],
                             jnp.ones((N_HEADS_B + n_gate_heads, HEAD_DIM), F32)], axis=0)
    norm_flags = jnp.concatenate([jnp.ones((2 * N_DIL,), F32), jnp.zeros((N_DIL,), F32),
                                  jnp.ones((BLK_BV0 - BLK_BQK0,), F32),
                                  jnp.zeros((BLK_GATE0 - BLK_BV0 + n_gate_heads // HPB,), F32)])
    offsets = jnp.arange(MOBA_BLOCK, dtype=F32)
    key_weights = jnp.exp2(slopes[N_HEADS_A:, None, None] * offsets[None, None, :])
    key_weights = jnp.pad(key_weights, ((0, 0), (0, MOBA_VT_PAD - 1), (0, 0)))
    a0, a1, a2, bqk, bvt, gates = _proj(_norm(x, ada, g_mix, sub=1), w_in, w_gate,
                                        gains, norm_flags, key_weights)

    y_a = _dilated((a0, a1, a2), slopes)
    y_b = _moba(bqk, bvt, slopes)

    x = _merge(x, ada, y_a, y_b, gates, w_branch_a.astype(BF16), w_branch_b.astype(BF16), w_out.astype(BF16))
    x = _ffn(x, ada, g_ffn2, ffn2_w_gate, ffn2_w_up, ffn2_w_down, sub=2)
    return x


def kernel(x, c, w_ada, b_ada, g_ffn1, ffn1_w_gate, ffn1_w_up, ffn1_w_down, g_mix, w_in, q_norm, k_norm,
           w_gate, w_branch_a, w_branch_b, w_out, g_ffn2, ffn2_w_gate, ffn2_w_up, ffn2_w_down):
    batch, depth = x.shape[0], w_ada.shape[0]
    outs = []
    for bi in range(batch):
        xb = x[bi]
        for l in range(depth):
            xb = _layer(xb, c[bi], w_ada[l], b_ada[l], g_ffn1[l], ffn1_w_gate[l], ffn1_w_up[l],
                        ffn1_w_down[l], g_mix[l], w_in[l], q_norm[l], k_norm[l], w_gate[l],
                        w_branch_a[l], w_branch_b[l], w_out[l], g_ffn2[l], ffn2_w_gate[l],
                        ffn2_w_up[l], ffn2_w_down[l])
        outs.append(xb)
    return jnp.stack(outs, axis=0)
```

```python
import functools

import jax
import jax.numpy as jnp
from jax import lax
from jax.experimental import pallas as pl
from jax.experimental.pallas import tpu as pltpu

HEAD_DIM = 128
DIL_PATTERNS = ((128, 1), (512, 4), (2048, 16))
N_DIL = len(DIL_PATTERNS)
HEADS_PER_DIL_GROUP = 4
N_HEADS_A = HEADS_PER_DIL_GROUP * N_DIL
N_HEADS_B = 8
N_HEADS = N_HEADS_A + N_HEADS_B
BAND = 128
MOBA_BLOCK = 256
MOBA_TOPK = 3
N_ADA = 9
EPS = 1e-6
NEG_INF = -1e30
LOG2_E = 1.4426950408889634

LANES = 128
VMEM_LIMIT = 60 * 1024 * 1024

BF16 = jnp.bfloat16
F32 = jnp.float32

_NT = (((1,), (1,)), ((), ()))


def _params(*sem):
    return pltpu.CompilerParams(dimension_semantics=sem, vmem_limit_bytes=VMEM_LIMIT)


def _sigmoid(x):
    return 1.0 / (1.0 + jnp.exp(-x))


NORM_ROWS = 128


def _norm_modulate_into(u_ref, row0, x_ref, g, shift, scale):
    gs = g * (1.0 + scale)

    def body(c, carry):
        rows = pl.ds(pl.multiple_of(c * NORM_ROWS, NORM_ROWS), NORM_ROWS)
        x = x_ref[rows, :]
        ms = jnp.mean(x * x, axis=-1, keepdims=True)
        y = (x * lax.rsqrt(ms + EPS)) * gs + shift
        u_ref[pl.ds(pl.multiple_of(row0 + c * NORM_ROWS, NORM_ROWS), NORM_ROWS), :] = y.astype(BF16)
        return carry

    lax.fori_loop(0, x_ref.shape[0] // NORM_ROWS, body, 0)


def _head_rmsnorm(a, gain):
    ms = jnp.mean(a * a, axis=-1, keepdims=True)
    return (a * lax.rsqrt(ms + EPS)) * gain


def _ada_kernel(c_ref, w_ref, b_ref, o_ref):
    c = c_ref[...]
    s = c * _sigmoid(c)
    o_ref[...] = jnp.sum(s * w_ref[...], axis=0, keepdims=True) + b_ref[...]


def _ada(c, w_ada, b_ada, *, tn=1024):
    d, n = w_ada.shape
    return pl.pallas_call(
        _ada_kernel,
        grid=(n // tn,),
        in_specs=[pl.BlockSpec((d, 1), lambda j: (0, 0)),
                  pl.BlockSpec((d, tn), lambda j: (0, j)),
                  pl.BlockSpec((1, tn), lambda j: (0, j))],
        out_specs=pl.BlockSpec((1, tn), lambda j: (0, j)),
        out_shape=jax.ShapeDtypeStruct((1, n), F32),
        compiler_params=_params("arbitrary"),
        name="ada",
    )(c.reshape(d, 1), w_ada, b_ada.reshape(1, n))


def _ffn_kernel(x_ref, ada_ref, g_ref, wg_ref, wu_ref, wd_ref, o_ref, u_ref, *, sub):
    f = pl.program_id(1)

    @pl.when(f == 0)
    def _():
        _norm_modulate_into(u_ref, 0, x_ref, g_ref[...],
                            ada_ref[3 * sub:3 * sub + 1, :], ada_ref[3 * sub + 1:3 * sub + 2, :])
        o_ref[...] = jnp.zeros_like(o_ref)

    u = u_ref[...]
    hg = jnp.dot(u, wg_ref[...].astype(BF16), preferred_element_type=F32)
    hu = jnp.dot(u, wu_ref[...].astype(BF16), preferred_element_type=F32)
    h = (hg * _sigmoid(hg)) * hu
    o_ref[...] += jnp.dot(h.astype(BF16), wd_ref[...].astype(BF16), preferred_element_type=F32)

    @pl.when(f == pl.num_programs(1) - 1)
    def _():
        gate = ada_ref[3 * sub + 2:3 * sub + 3, :]
        o_ref[...] = x_ref[...] + (0.5 * gate) * o_ref[...]


def _ffn(x, ada, g, w_gate, w_up, w_down, *, sub, tm=1024, tf=256):
    t, d = x.shape
    dff = w_gate.shape[1]
    return pl.pallas_call(
        functools.partial(_ffn_kernel, sub=sub),
        grid=(t // tm, dff // tf),
        in_specs=[pl.BlockSpec((tm, d), lambda i, f: (i, 0)),
                  pl.BlockSpec((N_ADA, d), lambda i, f: (0, 0)),
                  pl.BlockSpec((1, d), lambda i, f: (0, 0)),
                  pl.BlockSpec((d, tf), lambda i, f: (0, f)),
                  pl.BlockSpec((d, tf), lambda i, f: (0, f)),
                  pl.BlockSpec((tf, d), lambda i, f: (f, 0))],
        out_specs=pl.BlockSpec((tm, d), lambda i, f: (i, 0)),
        out_shape=jax.ShapeDtypeStruct((t, d), F32),
        scratch_shapes=[pltpu.VMEM((tm, d), BF16)],
        compiler_params=_params("parallel", "arbitrary"),
        name=f"ffn{sub}",
    )(x, ada, g.reshape(1, d), w_gate, w_up, w_down)


def _norm_kernel(x_ref, ada_ref, g_ref, u_ref, *, sub):
    _norm_modulate_into(u_ref, 0, x_ref, g_ref[...],
                        ada_ref[3 * sub:3 * sub + 1, :], ada_ref[3 * sub + 1:3 * sub + 2, :])


def _norm(x, ada, g, *, sub, tm=512):
    t, d = x.shape
    return pl.pallas_call(
        functools.partial(_norm_kernel, sub=sub),
        grid=(t // tm,),
        in_specs=[pl.BlockSpec((tm, d), lambda i: (i, 0)),
                  pl.BlockSpec((N_ADA, d), lambda i: (0, 0)),
                  pl.BlockSpec((1, d), lambda i: (0, 0))],
        out_specs=pl.BlockSpec((tm, d), lambda i: (i, 0)),
        out_shape=jax.ShapeDtypeStruct((t, d), BF16),
        compiler_params=_params("parallel"),
        name="norm",
    )(x, ada, g.reshape(1, d))


HPB = HEADS_PER_DIL_GROUP
PROJ_TN = HPB * HEAD_DIM
BLK_A_END = 3 * N_DIL
BLK_BQK0 = BLK_A_END
BLK_BV0 = BLK_BQK0 + 2 * (N_HEADS_B // HPB)
BLK_GATE0 = BLK_BV0 + N_HEADS_B // HPB


MOBA_VT_PAD = 16
MOBA_VT_ROWS = HEAD_DIM + MOBA_VT_PAD


PROJ_ROWS = MOBA_BLOCK


def _proj_kernel(nflag_ref, u_ref, win_ref, wgate_ref, gain_ref, kw_ref,
                 a0_ref, a1_ref, a2_ref, bqk_ref, bvt_ref, gate_ref, stg_ref, stg2_ref):
    j = pl.program_id(1)
    n_chunks = u_ref.shape[0] // PROJ_ROWS

    def chunks(w_ref):
        w = w_ref[...].astype(BF16)
        for c in range(n_chunks):
            rows = slice(c * PROJ_ROWS, (c + 1) * PROJ_ROWS)
            yield c, jnp.dot(u_ref[rows, :], w, preferred_element_type=F32)

    def head(res, h):
        a = res[:, h * HEAD_DIM:(h + 1) * HEAD_DIM]
        flag = nflag_ref[j]
        ms = jnp.mean(a * a, axis=-1, keepdims=True)
        return (a * (lax.rsqrt(ms + EPS) * flag + (1.0 - flag))) * gain_ref[j, h:h + 1, :]

    def dilated_block(o_ref, r):
        n = PROJ_ROWS // r
        for c, res in chunks(win_ref):
            for h in range(HPB):
                y = head(res, h)
                if r == 1:
                    o_ref[h, c * n:(c + 1) * n, :] = y.astype(BF16)
                    continue
                stg = stg_ref.at[c % 2, h]
                stg[...] = y
                if r == 16:
                    stg2 = stg2_ref.at[c % 2, h]
                    m = PROJ_ROWS // 4
                    for a in range(4):
                        stg2[a * m:(a + 1) * m, :] = stg[pl.ds(a, m, stride=4), :]
                    parts = {a + 4 * b: stg2[pl.ds(a * m + b, n, stride=4), :] for a in range(4) for b in range(4)}
                else:
                    parts = {rho: stg[pl.ds(rho, n, stride=r), :] for rho in range(r)}
                for rho in range(r):
                    o_ref[h, c * n:(c + 1) * n, rho * HEAD_DIM:(rho + 1) * HEAD_DIM] = parts[rho].astype(BF16)

    a_refs = (a0_ref, a1_ref, a2_ref)
    for g, (_, r) in enumerate(DIL_PATTERNS):
        pl.when((j < BLK_A_END) & (lax.rem(j, N_DIL) == g))(functools.partial(dilated_block, a_refs[g], r))

    @pl.when((j >= BLK_BQK0) & (j < BLK_BV0))
    def _():
        for c, res in chunks(win_ref):
            for h in range(HPB):
                bqk_ref[h, c * PROJ_ROWS:(c + 1) * PROJ_ROWS, :] = head(res, h).astype(BF16)

    @pl.when((j >= BLK_BV0) & (j < BLK_GATE0))
    def _():
        for c, res in chunks(win_ref):
            for h in range(HPB):
                kw = kw_ref[j - BLK_BV0, h]
                yt = res[:, h * HEAD_DIM:(h + 1) * HEAD_DIM].T
                bvt_ref[h, c, 0:HEAD_DIM, :] = (yt * kw[0:1, :]).astype(BF16)
                bvt_ref[h, c, HEAD_DIM:, :] = kw.astype(BF16)

    @pl.when(j >= BLK_GATE0)
    def _():
        for c, res in chunks(wgate_ref):
            gate_ref[c * PROJ_ROWS:(c + 1) * PROJ_ROWS, :] = _sigmoid(res).astype(BF16)


def _proj(u, w_in, w_gate, gains, norm_flags, key_weights, *, tm=2048):
    t, d = u.shape
    n_gate = w_gate.shape[1] // PROJ_TN
    n_blk = BLK_GATE0 + n_gate
    assert w_in.shape[1] == BLK_GATE0 * PROJ_TN
    n_bv = BLK_GATE0 - BLK_BV0

    def a_spec(gi, r):
        return pl.BlockSpec((HPB, tm // r, r * HEAD_DIM),
                            lambda i, j: (jnp.clip((j - gi + N_DIL - 1) // N_DIL, 0, 2), i, 0))

    a_shapes = [jax.ShapeDtypeStruct((3 * HPB, t // r, r * HEAD_DIM), BF16) for _, r in DIL_PATTERNS]
    out_shape = a_shapes + [
        jax.ShapeDtypeStruct((2 * N_HEADS_B, t, HEAD_DIM), BF16),
        jax.ShapeDtypeStruct((N_HEADS_B, t // MOBA_BLOCK, MOBA_VT_ROWS, MOBA_BLOCK), BF16),
        jax.ShapeDtypeStruct((t, n_gate * PROJ_TN), BF16)]
    out_specs = [a_spec(gi, r) for gi, (_, r) in enumerate(DIL_PATTERNS)] + [
        pl.BlockSpec((HPB, tm, HEAD_DIM),
                     lambda i, j: (jnp.clip(j - BLK_BQK0, 0, BLK_BV0 - BLK_BQK0 - 1), i, 0)),
        pl.BlockSpec((HPB, tm // MOBA_BLOCK, MOBA_VT_ROWS, MOBA_BLOCK),
                     lambda i, j: (jnp.clip(j - BLK_BV0, 0, n_bv - 1), i, 0, 0)),
        pl.BlockSpec((tm, PROJ_TN), lambda i, j: (i, jnp.clip(j - BLK_GATE0, 0, n_gate - 1)))]
    return pl.pallas_call(
        _proj_kernel,
        grid=(t // tm, n_blk),
        in_specs=[pl.BlockSpec(memory_space=pltpu.SMEM),
                  pl.BlockSpec((tm, d), lambda i, j: (i, 0), pipeline_mode=pl.Buffered(1)),
                  pl.BlockSpec((d, PROJ_TN), lambda i, j: (0, jnp.minimum(j, BLK_GATE0 - 1))),
                  pl.BlockSpec((d, PROJ_TN), lambda i, j: (0, jnp.clip(j - BLK_GATE0, 0, n_gate - 1))),
                  pl.BlockSpec((n_blk, HPB, HEAD_DIM), lambda i, j: (0, 0, 0)),
                  pl.BlockSpec((n_bv, HPB, MOBA_VT_PAD, MOBA_BLOCK), lambda i, j: (0, 0, 0, 0))],
        out_specs=out_specs,
        out_shape=out_shape,
        scratch_shapes=[pltpu.VMEM((2, HPB, PROJ_ROWS, HEAD_DIM), F32)] * 2,
        compiler_params=_params("arbitrary", "arbitrary"),
        name="proj",
    )(norm_flags, u, w_in, w_gate, gains.reshape(n_blk, HPB, HEAD_DIM),
      key_weights.reshape(n_bv, HPB, MOBA_VT_PAD, MOBA_BLOCK))


DIL_TOKENS = 2048


def _band_block(q, k2, v2, bias):
    s = lax.dot_general(q, k2, _NT, preferred_element_type=F32) + bias
    m = jnp.max(s, axis=-1, keepdims=True)
    p = jnp.exp2(s - m)
    denom = jnp.sum(p, axis=-1, keepdims=True)
    o = jnp.dot(p.astype(BF16), v2, preferred_element_type=F32) / denom
    lse = jnp.broadcast_to(m + jnp.log2(denom), (BAND, LANES))
    return o, lse


def _dilated_kernel(slopes_ref,
                    q0, k0, v0, kp0, vp0,
                    q1, k1, v1, kp1, vp1,
                    q2, k2, v2, kp2, vp2,
                    y_ref, o_scr, l_scr):
    j = pl.program_id(0)
    b = pl.program_id(1)

    qi = lax.broadcasted_iota(jnp.int32, (BAND, 2 * BAND), 0)
    ki = lax.broadcasted_iota(jnp.int32, (BAND, 2 * BAND), 1)
    dist = BAND + qi - ki
    in_band = (dist >= 0) & (dist <= BAND)
    dist_f = dist.astype(F32)
    first_ok = (ki + jnp.minimum(b, 1) * BAND) >= BAND

    groups = ((q0, k0, v0, kp0, vp0), (q1, k1, v1, kp1, vp1), (q2, k2, v2, kp2, vp2))
    for g, (_, r) in enumerate(DIL_PATTERNS):
        q_ref, k_ref, v_ref, kp_ref, vp_ref = groups[g]
        slope = slopes_ref[g * HEADS_PER_DIL_GROUP + j]
        bias = jnp.where(in_band, dist_f * (-slope * r), NEG_INF)
        bias_first = jnp.where(first_ok, bias, NEG_INF)
        n_blk = DIL_TOKENS // (r * BAND)
        for rho in range(r):
            cols = slice(rho * HEAD_DIM, (rho + 1) * HEAD_DIM)

            def store(i, o, lse, g=g, r=r, rho=rho):
                rows = pl.ds(i * (BAND * r) + rho, BAND, stride=r) if r > 1 else pl.ds(i * BAND, BAND)
                o_scr.at[g][rows, :] = o
                l_scr.at[g][rows, :] = lse

            kk = jnp.concatenate([kp_ref[:, cols], k_ref[0:BAND, cols]], axis=0)
            vv = jnp.concatenate([vp_ref[:, cols], v_ref[0:BAND, cols]], axis=0)
            o, lse = _band_block(q_ref[0:BAND, cols], kk, vv, bias_first)
            store(0, o, lse)

            for i in range(1, n_blk):
                o, lse = _band_block(q_ref[i * BAND:(i + 1) * BAND, cols],
                                     k_ref[(i - 1) * BAND:(i + 1) * BAND, cols],
                                     v_ref[(i - 1) * BAND:(i + 1) * BAND, cols], bias)
                store(i, o, lse)

    l0, l1, l2 = l_scr[0], l_scr[1], l_scr[2]
    mx = jnp.maximum(jnp.maximum(l0, l1), l2)
    e0, e1, e2 = jnp.exp2(l0 - mx), jnp.exp2(l1 - mx), jnp.exp2(l2 - mx)
    tot = e0 + e1 + e2
    y = (e0 / tot) * o_scr[0] + (e1 / tot) * o_scr[1] + (e2 / tot) * o_scr[2]
    y_ref[...] = y.astype(BF16)


def _dilated(a_views, slopes):
    t = a_views[0].shape[1]
    dh = HEAD_DIM
    nb = t // DIL_TOKENS
    hq, hk, hv = 0, HPB, 2 * HPB
    args, specs = [], []
    for g, (_, r) in enumerate(DIL_PATTERNS):
        rows = DIL_TOKENS // r
        prev_per_blk = rows // BAND
        width = r * dh

        def cur(base, rows=rows, width=width):
            return pl.BlockSpec((None, rows, width), lambda j, b: (base + j, b, 0))

        def prev(base, width=width, ppb=prev_per_blk):
            return pl.BlockSpec((None, BAND, width),
                                lambda j, b: (base + j, jnp.maximum(b * ppb - 1, 0), 0))

        args += [a_views[g]] * 5
        specs += [cur(hq), cur(hk), cur(hv), prev(hk), prev(hv)]

    return pl.pallas_call(
        _dilated_kernel,
        grid=(HEADS_PER_DIL_GROUP, nb),
        in_specs=[pl.BlockSpec(memory_space=pltpu.SMEM)] + specs,
        out_specs=pl.BlockSpec((DIL_TOKENS, dh), lambda j, b: (b, j)),
        out_shape=jax.ShapeDtypeStruct((t, HEADS_PER_DIL_GROUP * dh), BF16),
        scratch_shapes=[pltpu.VMEM((N_DIL, DIL_TOKENS, dh), F32),
                        pltpu.VMEM((N_DIL, DIL_TOKENS, LANES), F32)],
        compiler_params=_params("arbitrary", "arbitrary"),
        name="dilated",
    )(slopes, *args)


MOBA_UNROLL = 4


MOBA_HEADS_PER_STEP = 8


def _moba_kernel(slopes_ref, q_ref, k_ref, vt_ref, y_ref, kmean_scr, sel_scr, acc_scr, s_scr, *, n_blocks):
    hg = pl.program_id(0)
    qb = pl.program_id(1)
    blk = MOBA_BLOCK
    n_h, _, dh = q_ref.shape
    heads = range(n_h)
    slope = [slopes_ref[N_HEADS_A + hg * n_h + hh] for hh in heads]

    @pl.when(qb == 0)
    def _():
        def body(n, carry):
            for hh in heads:
                kn = k_ref[hh, pl.ds(pl.multiple_of(n * blk, blk), blk), :].astype(F32)
                kmean_scr[hh, pl.ds(n, 1), :] = jnp.mean(kn, axis=0, keepdims=True)
            return carry
        lax.fori_loop(0, n_blocks, body, 0)

    q = [q_ref[hh] for hh in heads]

    blk_id = lax.broadcasted_iota(jnp.int32, (n_blocks, blk), 0)
    blk_id_f = blk_id.astype(F32)
    past = blk_id < qb
    key_i = lax.broadcasted_iota(jnp.int32, (blk, blk), 0)
    qry_i = lax.broadcasted_iota(jnp.int32, (blk, blk), 1)
    own = pl.multiple_of(qb * blk, blk)

    in_block_max = [slope[hh] * (blk - 1) for hh in heads]

    width = MOBA_UNROLL
    n_chunks = (qb + width - 1) // width

    def chunk_scores(hh, ci, slot):
        start = pl.multiple_of(ci * (width * blk), width * blk)
        s_scr[slot, hh] = lax.dot_general(k_ref[hh, pl.ds(start, width * blk), :], q[hh], _NT,
                                          preferred_element_type=F32)

    gates = [lax.dot_general(kmean_scr[hh], q[hh].astype(F32), _NT,
                             precision=lax.Precision.HIGHEST, preferred_element_type=F32) for hh in heads]
    s_own = [lax.dot_general(k_ref[hh, pl.ds(own, blk), :], q[hh], _NT, preferred_element_type=F32)
             for hh in heads]
    for hh in heads:
        chunk_scores(hh, 0, 0)

    m0, p_own = [], []
    for hh in heads:
        work = jnp.where(past, gates[hh], NEG_INF)
        sel = jnp.zeros((n_blocks, blk), F32)
        for _ in range(MOBA_TOPK):
            mx = jnp.max(work, axis=0, keepdims=True)
            first = jnp.min(jnp.where(work == mx, blk_id_f, float(n_blocks)), axis=0, keepdims=True)
            hit = blk_id_f == first
            sel = jnp.where(hit, 1.0, sel)
            work = jnp.where(hit, -jnp.inf, work)
        sel_scr[hh] = jnp.where(past, sel, 0.0)

        s = jnp.where(key_i <= qry_i, s_own[hh], NEG_INF)
        m0.append(jnp.max(s, axis=0, keepdims=True) + in_block_max[hh])
        p_own.append(jnp.exp2(s - m0[hh]).astype(BF16))
    for hh in heads:
        acc_scr[hh] = jnp.dot(vt_ref[hh, qb], p_own[hh], preferred_element_type=F32)

    def past_chunk(ci, m_run, src, dst):
        n0 = ci * width
        nxt = jnp.minimum(ci + 1, n_chunks - 1)
        out = []
        for hh in heads:
            chunk_scores(hh, nxt, dst)
            c, chosen = [], []
            m_chunk = jnp.full((1, blk), NEG_INF, F32)
            for a in range(width):
                s = s_scr[src, hh, a * blk:(a + 1) * blk, :]
                c.append(slope[hh] * jnp.full((1, blk), (n0 + a - qb) * blk, jnp.int32).astype(F32))
                chosen.append(sel_scr[hh, pl.ds(n0 + a, 1), :] > 0.5)
                m_blk = jnp.max(s, axis=0, keepdims=True) + (c[a] + in_block_max[hh])
                m_chunk = jnp.maximum(m_chunk, jnp.where(chosen[a], m_blk, NEG_INF))
            pv = jnp.zeros(acc_scr.shape[1:], F32)
            for a in range(width):
                s = s_scr[src, hh, a * blk:(a + 1) * blk, :]
                p = jnp.exp2(s - jnp.where(chosen[a], m_chunk - c[a], -NEG_INF))
                pv = pv + jnp.dot(vt_ref[hh, n0 + a], p.astype(BF16), preferred_element_type=F32)
            m_new = jnp.maximum(m_run[hh], m_chunk)
            alpha = jnp.exp2(m_run[hh] - m_new)
            beta = jnp.exp2(m_chunk - m_new)
            acc_scr[hh] = alpha * acc_scr[hh] + beta * pv
            out.append(m_new)
        return tuple(out)

    def body(ci, m_run):
        return lax.cond(lax.rem(ci, 2) == 0,
                        lambda m: past_chunk(ci, m, 0, 1), lambda m: past_chunk(ci, m, 1, 0), m_run)

    lax.fori_loop(0, n_chunks, body, tuple(m0))
    for hh in heads:
        y = acc_scr[hh, 0:dh, :] / acc_scr[hh, dh:dh + 1, :]
        y_ref[:, hh * dh:(hh + 1) * dh] = y.T.astype(BF16)


def _moba(bqk, bvt, slopes):
    _, t, dh = bqk.shape
    n_blocks = t // MOBA_BLOCK
    assert n_blocks % MOBA_UNROLL == 0, "the padded last chunk must stay inside the key array"
    n_h = MOBA_HEADS_PER_STEP
    n_groups = N_HEADS_B // n_h
    return pl.pallas_call(
        functools.partial(_moba_kernel, n_blocks=n_blocks),
        grid=(n_groups, n_blocks),
        in_specs=[pl.BlockSpec(memory_space=pltpu.SMEM),
                  pl.BlockSpec((n_h, MOBA_BLOCK, dh), lambda h, i: (h, i, 0)),
                  pl.BlockSpec((n_h, t, dh), lambda h, i: (n_groups + h, 0, 0), pipeline_mode=pl.Buffered(1)),
                  pl.BlockSpec((n_h, n_blocks, MOBA_VT_ROWS, MOBA_BLOCK), lambda h, i: (h, 0, 0, 0),
                               pipeline_mode=pl.Buffered(1))],
        out_specs=pl.BlockSpec((MOBA_BLOCK, n_h * dh), lambda h, i: (i, h)),
        out_shape=jax.ShapeDtypeStruct((t, N_HEADS_B * dh), BF16),
        scratch_shapes=[pltpu.VMEM((n_h, n_blocks, dh), F32),
                        pltpu.VMEM((n_h, n_blocks, MOBA_BLOCK), F32),
                        pltpu.VMEM((n_h, MOBA_VT_ROWS, MOBA_BLOCK), F32),
                        pltpu.VMEM((2, n_h, MOBA_UNROLL * MOBA_BLOCK, MOBA_BLOCK), F32)],
        compiler_params=_params("arbitrary", "arbitrary"),
        name="moba",
    )(slopes, bqk, bqk, bvt)


def _merge_kernel(x_ref, ada_ref, ya_ref, yb_ref, gates_a_ref, gates_b_ref, wa_ref, wb_ref, wo_ref, o_ref):
    pa = jnp.dot(ya_ref[...], wa_ref[...], preferred_element_type=F32)
    pb = jnp.dot(yb_ref[...], wb_ref[...], preferred_element_type=F32)
    merged = gates_a_ref[...].astype(F32) * pa + gates_b_ref[...].astype(F32) * pb
    out = jnp.dot(merged.astype(BF16), wo_ref[...], preferred_element_type=F32)
    o_ref[...] = x_ref[...] + ada_ref[5:6, :] * out


def _merge(x, ada, y_a, y_b, gates, w_a, w_b, w_o, *, tm=256):
    t, d = x.shape
    const = lambda i: (0, 0)
    return pl.pallas_call(
        _merge_kernel,
        grid=(t // tm,),
        in_specs=[pl.BlockSpec((tm, d), lambda i: (i, 0)),
                  pl.BlockSpec((N_ADA, d), const),
                  pl.BlockSpec((tm, y_a.shape[1]), lambda i: (i, 0)),
                  pl.BlockSpec((tm, y_b.shape[1]), lambda i: (i, 0)),
                  pl.BlockSpec((tm, d), lambda i: (i, 0)),
                  pl.BlockSpec((tm, d), lambda i: (i, 1)),
                  pl.BlockSpec(w_a.shape, const),
                  pl.BlockSpec(w_b.shape, const),
                  pl.BlockSpec(w_o.shape, const)],
        out_specs=pl.BlockSpec((tm, d), lambda i: (i, 0)),
        out_shape=jax.ShapeDtypeStruct((t, d), F32),
        compiler_params=_params("parallel"),
        name="merge",
    )(x, ada, y_a, y_b, gates, gates, w_a, w_b, w_o)


def _layer(x, c, w_ada, b_ada, g_ffn1, ffn1_w_gate, ffn1_w_up, ffn1_w_down,
           g_mix, w_in, q_norm, k_norm, w_gate, w_branch_a, w_branch_b, w_out,
           g_ffn2, ffn2_w_gate, ffn2_w_up, ffn2_w_down):
    t, d = x.shape
    ada = _ada(c, w_ada, b_ada).reshape(N_ADA, d)
    slopes = jnp.exp2(-8.0 * jnp.arange(1, N_HEADS + 1, dtype=F32) / N_HEADS) * LOG2_E

    x = _ffn(x, ada, g_ffn1, ffn1_w_gate, ffn1_w_up, ffn1_w_down, sub=0)

    scale = HEAD_DIM ** -0.5 * LOG2_E
    n_gate_heads = w_gate.shape[1] // HEAD_DIM
    gains = jnp.concatenate([q_norm[:N_HEADS_A] * scale, k_norm[:N_HEADS_A], jnp.ones((N_HEADS_A, HEAD_DIM), F32),
                             q_norm[N_HEADS_A:] * scale, k_norm[N_HEADS_A:],
                             jnp.ones((N_HEADS_B + n_gate_heads, HEAD_DIM), F32)], axis=0)
    norm_flags = jnp.concatenate([jnp.ones((2 * N_DIL,), F32), jnp.zeros((N_DIL,), F32),
                                  jnp.ones((BLK_BV0 - BLK_BQK0,), F32),
                                  jnp.zeros((BLK_GATE0 - BLK_BV0 + n_gate_heads // HPB,), F32)])
    offsets = jnp.arange(MOBA_BLOCK, dtype=F32)
    key_weights = jnp.exp2(slopes[N_HEADS_A:, None, None] * offsets[None, None, :])
    key_weights = jnp.pad(key_weights, ((0, 0), (0, MOBA_VT_PAD - 1), (0, 0)))
    a0, a1, a2, bqk, bvt, gates = _proj(_norm(x, ada, g_mix, sub=1), w_in, w_gate,
                                        gains, norm_flags, key_weights)

    y_a = _dilated((a0, a1, a2), slopes)
    y_b = _moba(bqk, bvt, slopes)

    x = _merge(x, ada, y_a, y_b, gates, w_branch_a.astype(BF16), w_branch_b.astype(BF16), w_out.astype(BF16))
    x = _ffn(x, ada, g_ffn2, ffn2_w_gate, ffn2_w_up, ffn2_w_down, sub=2)
    return x


def kernel(x, c, w_ada, b_ada, g_ffn1, ffn1_w_gate, ffn1_w_up, ffn1_w_down, g_mix, w_in, q_norm, k_norm,
           w_gate, w_branch_a, w_branch_b, w_out, g_ffn2, ffn2_w_gate, ffn2_w_up, ffn2_w_down):
    batch, depth = x.shape[0], w_ada.shape[0]
    outs = []
    for bi in range(batch):
        xb = x[bi]
        for l in range(depth):
            xb = _layer(xb, c[bi], w_ada[l], b_ada[l], g_ffn1[l], ffn1_w_gate[l], ffn1_w_up[l],
                        ffn1_w_down[l], g_mix[l], w_in[l], q_norm[l], k_norm[l], w_gate[l],
                        w_branch_a[l], w_branch_b[l], w_out[l], g_ffn2[l], ffn2_w_gate[l],
                        ffn2_w_up[l], ffn2_w_down[l])
        outs.append(xb)
    return jnp.stack(outs, axis=0)
```

```python
import functools

import jax
import jax.numpy as jnp
from jax import lax
from jax.experimental import pallas as pl
from jax.experimental.pallas import tpu as pltpu

HEAD_DIM = 128
DIL_PATTERNS = ((128, 1), (512, 4), (2048, 16))
N_DIL = len(DIL_PATTERNS)
HEADS_PER_DIL_GROUP = 4
N_HEADS_A = HEADS_PER_DIL_GROUP * N_DIL
N_HEADS_B = 8
N_HEADS = N_HEADS_A + N_HEADS_B
BAND = 128
MOBA_BLOCK = 256
MOBA_TOPK = 3
N_ADA = 9
EPS = 1e-6
NEG_INF = -1e30
LOG2_E = 1.4426950408889634

LANES = 128
VMEM_LIMIT = 60 * 1024 * 1024

BF16 = jnp.bfloat16
F32 = jnp.float32

_NT = (((1,), (1,)), ((), ()))


def _params(*sem):
    return pltpu.CompilerParams(dimension_semantics=sem, vmem_limit_bytes=VMEM_LIMIT)


def _sigmoid(x):
    return 1.0 / (1.0 + jnp.exp(-x))


NORM_ROWS = 128


def _norm_modulate_into(u_ref, row0, x_ref, g, shift, scale):
    gs = g * (1.0 + scale)

    def body(c, carry):
        rows = pl.ds(pl.multiple_of(c * NORM_ROWS, NORM_ROWS), NORM_ROWS)
        x = x_ref[rows, :]
        ms = jnp.mean(x * x, axis=-1, keepdims=True)
        y = (x * lax.rsqrt(ms + EPS)) * gs + shift
        u_ref[pl.ds(pl.multiple_of(row0 + c * NORM_ROWS, NORM_ROWS), NORM_ROWS), :] = y.astype(BF16)
        return carry

    lax.fori_loop(0, x_ref.shape[0] // NORM_ROWS, body, 0)


def _head_rmsnorm(a, gain):
    ms = jnp.mean(a * a, axis=-1, keepdims=True)
    return (a * lax.rsqrt(ms + EPS)) * gain


def _ada_kernel(c_ref, w_ref, b_ref, o_ref):
    c = c_ref[...]
    s = c * _sigmoid(c)
    o_ref[...] = jnp.sum(s * w_ref[...], axis=0, keepdims=True) + b_ref[...]


def _ada(c, w_ada, b_ada, *, tn=1024):
    d, n = w_ada.shape
    return pl.pallas_call(
        _ada_kernel,
        grid=(n // tn,),
        in_specs=[pl.BlockSpec((d, 1), lambda j: (0, 0)),
                  pl.BlockSpec((d, tn), lambda j: (0, j)),
                  pl.BlockSpec((1, tn), lambda j: (0, j))],
        out_specs=pl.BlockSpec((1, tn), lambda j: (0, j)),
        out_shape=jax.ShapeDtypeStruct((1, n), F32),
        compiler_params=_params("arbitrary"),
        name="ada",
    )(c.reshape(d, 1), w_ada, b_ada.reshape(1, n))


FFN_FIRST_ROWS = 256


def _ffn_kernel(x_ref, ada_ref, g_ref, wg_ref, wu_ref, wd_ref, o_ref, u_ref, *, sub):
    f = pl.program_id(1)

    def swiglu_tile(u, wg, wu, wd):
        hg = jnp.dot(u, wg, preferred_element_type=F32)
        hu = jnp.dot(u, wu, preferred_element_type=F32)
        h = (hg * _sigmoid(hg)) * hu
        return jnp.dot(h.astype(BF16), wd, preferred_element_type=F32)

    @pl.when(f == 0)
    def _():
        wg, wu, wd = wg_ref[...].astype(BF16), wu_ref[...].astype(BF16), wd_ref[...].astype(BF16)
        gs = g_ref[...] * (1.0 + ada_ref[3 * sub + 1:3 * sub + 2, :])
        shift = ada_ref[3 * sub:3 * sub + 1, :]
        for c in range(x_ref.shape[0] // FFN_FIRST_ROWS):
            rows = slice(c * FFN_FIRST_ROWS, (c + 1) * FFN_FIRST_ROWS)
            x = x_ref[rows, :]
            ms = jnp.mean(x * x, axis=-1, keepdims=True)
            u = ((x * lax.rsqrt(ms + EPS)) * gs + shift).astype(BF16)
            u_ref[rows, :] = u
            o_ref[rows, :] = swiglu_tile(u, wg, wu, wd)

    @pl.when(f > 0)
    def _():
        o_ref[...] += swiglu_tile(u_ref[...], wg_ref[...].astype(BF16), wu_ref[...].astype(BF16),
                                  wd_ref[...].astype(BF16))

    @pl.when(f == pl.num_programs(1) - 1)
    def _():
        gate = ada_ref[3 * sub + 2:3 * sub + 3, :]
        o_ref[...] = x_ref[...] + (0.5 * gate) * o_ref[...]


def _ffn(x, ada, g, w_gate, w_up, w_down, *, sub, tm=1024, tf=256):
    t, d = x.shape
    dff = w_gate.shape[1]
    return pl.pallas_call(
        functools.partial(_ffn_kernel, sub=sub),
        grid=(t // tm, dff // tf),
        in_specs=[pl.BlockSpec((tm, d), lambda i, f: (i, 0)),
                  pl.BlockSpec((N_ADA, d), lambda i, f: (0, 0)),
                  pl.BlockSpec((1, d), lambda i, f: (0, 0)),
                  pl.BlockSpec((d, tf), lambda i, f: (0, f)),
                  pl.BlockSpec((d, tf), lambda i, f: (0, f)),
                  pl.BlockSpec((tf, d), lambda i, f: (f, 0))],
        out_specs=pl.BlockSpec((tm, d), lambda i, f: (i, 0)),
        out_shape=jax.ShapeDtypeStruct((t, d), F32),
        scratch_shapes=[pltpu.VMEM((tm, d), BF16)],
        compiler_params=_params("parallel", "arbitrary"),
        name=f"ffn{sub}",
    )(x, ada, g.reshape(1, d), w_gate, w_up, w_down)


def _norm_kernel(x_ref, ada_ref, g_ref, u_ref, *, sub):
    _norm_modulate_into(u_ref, 0, x_ref, g_ref[...],
                        ada_ref[3 * sub:3 * sub + 1, :], ada_ref[3 * sub + 1:3 * sub + 2, :])


def _norm(x, ada, g, *, sub, tm=512):
    t, d = x.shape
    return pl.pallas_call(
        functools.partial(_norm_kernel, sub=sub),
        grid=(t // tm,),
        in_specs=[pl.BlockSpec((tm, d), lambda i: (i, 0)),
                  pl.BlockSpec((N_ADA, d), lambda i: (0, 0)),
                  pl.BlockSpec((1, d), lambda i: (0, 0))],
        out_specs=pl.BlockSpec((tm, d), lambda i: (i, 0)),
        out_shape=jax.ShapeDtypeStruct((t, d), BF16),
        compiler_params=_params("parallel"),
        name="norm",
    )(x, ada, g.reshape(1, d))


HPB = HEADS_PER_DIL_GROUP
PROJ_TN = HPB * HEAD_DIM
BLK_A_END = 3 * N_DIL
BLK_BQK0 = BLK_A_END
BLK_BV0 = BLK_BQK0 + 2 * (N_HEADS_B // HPB)
BLK_GATE0 = BLK_BV0 + N_HEADS_B // HPB


MOBA_VT_PAD = 16
MOBA_VT_ROWS = HEAD_DIM + MOBA_VT_PAD


PROJ_ROWS = MOBA_BLOCK


def _proj_kernel(nflag_ref, u_ref, win_ref, wgate_ref, gain_ref, kw_ref,
                 a0_ref, a1_ref, a2_ref, bqk_ref, bvt_ref, gate_ref, stg_ref, stg2_ref):
    j = pl.program_id(1)
    n_chunks = u_ref.shape[0] // PROJ_ROWS

    def chunks(w_ref):
        w = w_ref[...].astype(BF16)
        for c in range(n_chunks):
            rows = slice(c * PROJ_ROWS, (c + 1) * PROJ_ROWS)
            yield c, jnp.dot(u_ref[rows, :], w, preferred_element_type=F32)

    def head(res, h):
        a = res[:, h * HEAD_DIM:(h + 1) * HEAD_DIM]
        flag = nflag_ref[j]
        ms = jnp.mean(a * a, axis=-1, keepdims=True)
        return (a * (lax.rsqrt(ms + EPS) * flag + (1.0 - flag))) * gain_ref[j, h:h + 1, :]

    def dilated_block(o_ref, r):
        n = PROJ_ROWS // r
        for c, res in chunks(win_ref):
            for h in range(HPB):
                y = head(res, h)
                if r == 1:
                    o_ref[h, c * n:(c + 1) * n, :] = y.astype(BF16)
                    continue
                stg = stg_ref.at[c % 2, h]
                stg[...] = y
                if r == 16:
                    stg2 = stg2_ref.at[c % 2, h]
                    m = PROJ_ROWS // 4
                    for a in range(4):
                        stg2[a * m:(a + 1) * m, :] = stg[pl.ds(a, m, stride=4), :]
                    parts = {a + 4 * b: stg2[pl.ds(a * m + b, n, stride=4), :] for a in range(4) for b in range(4)}
                else:
                    parts = {rho: stg[pl.ds(rho, n, stride=r), :] for rho in range(r)}
                for rho in range(r):
                    o_ref[h, c * n:(c + 1) * n, rho * HEAD_DIM:(rho + 1) * HEAD_DIM] = parts[rho].astype(BF16)

    a_refs = (a0_ref, a1_ref, a2_ref)
    for g, (_, r) in enumerate(DIL_PATTERNS):
        pl.when((j < BLK_A_END) & (lax.rem(j, N_DIL) == g))(functools.partial(dilated_block, a_refs[g], r))

    @pl.when((j >= BLK_BQK0) & (j < BLK_BV0))
    def _():
        for c, res in chunks(win_ref):
            for h in range(HPB):
                bqk_ref[h, c * PROJ_ROWS:(c + 1) * PROJ_ROWS, :] = head(res, h).astype(BF16)

    @pl.when((j >= BLK_BV0) & (j < BLK_GATE0))
    def _():
        for c, res in chunks(win_ref):
            for h in range(HPB):
                kw = kw_ref[j - BLK_BV0, h]
                yt = res[:, h * HEAD_DIM:(h + 1) * HEAD_DIM].T
                bvt_ref[h, c, 0:HEAD_DIM, :] = (yt * kw[0:1, :]).astype(BF16)
                bvt_ref[h, c, HEAD_DIM:, :] = kw.astype(BF16)

    @pl.when(j >= BLK_GATE0)
    def _():
        for c, res in chunks(wgate_ref):
            gate_ref[c * PROJ_ROWS:(c + 1) * PROJ_ROWS, :] = _sigmoid(res).astype(BF16)


def _proj(u, w_in, w_gate, gains, norm_flags, key_weights, *, tm=2048):
    t, d = u.shape
    n_gate = w_gate.shape[1] // PROJ_TN
    n_blk = BLK_GATE0 + n_gate
    assert w_in.shape[1] == BLK_GATE0 * PROJ_TN
    n_bv = BLK_GATE0 - BLK_BV0

    def a_spec(gi, r):
        return pl.BlockSpec((HPB, tm // r, r * HEAD_DIM),
                            lambda i, j: (jnp.clip((j - gi + N_DIL - 1) // N_DIL, 0, 2), i, 0))

    a_shapes = [jax.ShapeDtypeStruct((3 * HPB, t // r, r * HEAD_DIM), BF16) for _, r in DIL_PATTERNS]
    out_shape = a_shapes + [
        jax.ShapeDtypeStruct((2 * N_HEADS_B, t, HEAD_DIM), BF16),
        jax.ShapeDtypeStruct((N_HEADS_B, t // MOBA_BLOCK, MOBA_VT_ROWS, MOBA_BLOCK), BF16),
        jax.ShapeDtypeStruct((t, n_gate * PROJ_TN), BF16)]
    out_specs = [a_spec(gi, r) for gi, (_, r) in enumerate(DIL_PATTERNS)] + [
        pl.BlockSpec((HPB, tm, HEAD_DIM),
                     lambda i, j: (jnp.clip(j - BLK_BQK0, 0, BLK_BV0 - BLK_BQK0 - 1), i, 0)),
        pl.BlockSpec((HPB, tm // MOBA_BLOCK, MOBA_VT_ROWS, MOBA_BLOCK),
                     lambda i, j: (jnp.clip(j - BLK_BV0, 0, n_bv - 1), i, 0, 0)),
        pl.BlockSpec((tm, PROJ_TN), lambda i, j: (i, jnp.clip(j - BLK_GATE0, 0, n_gate - 1)))]
    return pl.pallas_call(
        _proj_kernel,
        grid=(t // tm, n_blk),
        in_specs=[pl.BlockSpec(memory_space=pltpu.SMEM),
                  pl.BlockSpec((tm, d), lambda i, j: (i, 0), pipeline_mode=pl.Buffered(1)),
                  pl.BlockSpec((d, PROJ_TN), lambda i, j: (0, jnp.minimum(j, BLK_GATE0 - 1))),
                  pl.BlockSpec((d, PROJ_TN), lambda i, j: (0, jnp.clip(j - BLK_GATE0, 0, n_gate - 1))),
                  pl.BlockSpec((n_blk, HPB, HEAD_DIM), lambda i, j: (0, 0, 0)),
                  pl.BlockSpec((n_bv, HPB, MOBA_VT_PAD, MOBA_BLOCK), lambda i, j: (0, 0, 0, 0))],
        out_specs=out_specs,
        out_shape=out_shape,
        scratch_shapes=[pltpu.VMEM((2, HPB, PROJ_ROWS, HEAD_DIM), F32)] * 2,
        compiler_params=_params("arbitrary", "arbitrary"),
        name="proj",
    )(norm_flags, u, w_in, w_gate, gains.reshape(n_blk, HPB, HEAD_DIM),
      key_weights.reshape(n_bv, HPB, MOBA_VT_PAD, MOBA_BLOCK))


DIL_TOKENS = 2048


def _band_blocks(blocks, store):
    scores = [lax.dot_general(q, k2, _NT, preferred_element_type=F32) + bias
              for q, k2, _, bias, _ in blocks]
    soft = []
    for s in scores:
        m = jnp.max(s, axis=-1, keepdims=True)
        p = jnp.exp2(s - m)
        denom = jnp.sum(p, axis=-1, keepdims=True)
        soft.append((p.astype(BF16), denom, jnp.broadcast_to(m + jnp.log2(denom), (BAND, LANES))))
    for (p, denom, lse), (_, _, v2, _, args) in zip(soft, blocks):
        o = jnp.dot(p, v2, preferred_element_type=F32) / denom
        store(*args, o, lse)


def _dilated_kernel(slopes_ref,
                    q0, k0, v0, kp0, vp0,
                    q1, k1, v1, kp1, vp1,
                    q2, k2, v2, kp2, vp2,
                    y_ref, o_scr, l_scr):
    j = pl.program_id(0)
    b = pl.program_id(1)

    qi = lax.broadcasted_iota(jnp.int32, (BAND, 2 * BAND), 0)
    ki = lax.broadcasted_iota(jnp.int32, (BAND, 2 * BAND), 1)
    dist = BAND + qi - ki
    in_band = (dist >= 0) & (dist <= BAND)
    dist_f = dist.astype(F32)
    first_ok = (ki + jnp.minimum(b, 1) * BAND) >= BAND

    groups = ((q0, k0, v0, kp0, vp0), (q1, k1, v1, kp1, vp1), (q2, k2, v2, kp2, vp2))
    for g, (_, r) in enumerate(DIL_PATTERNS):
        q_ref, k_ref, v_ref, kp_ref, vp_ref = groups[g]
        slope = slopes_ref[g * HEADS_PER_DIL_GROUP + j]
        bias = jnp.where(in_band, dist_f * (-slope * r), NEG_INF)
        bias_first = jnp.where(first_ok, bias, NEG_INF)
        n_blk = DIL_TOKENS // (r * BAND)

        def store(i, rho, o, lse, g=g, r=r):
            rows = pl.ds(i * (BAND * r) + rho, BAND, stride=r) if r > 1 else pl.ds(i * BAND, BAND)
            o_scr.at[g][rows, :] = o
            l_scr.at[g][rows, :] = lse

        blocks = []
        for rho in range(r):
            cols = slice(rho * HEAD_DIM, (rho + 1) * HEAD_DIM)
            kk = jnp.concatenate([kp_ref[:, cols], k_ref[0:BAND, cols]], axis=0)
            vv = jnp.concatenate([vp_ref[:, cols], v_ref[0:BAND, cols]], axis=0)
            blocks.append((q_ref[0:BAND, cols], kk, vv, bias_first, (0, rho)))
            for i in range(1, n_blk):
                blocks.append((q_ref[i * BAND:(i + 1) * BAND, cols],
                               k_ref[(i - 1) * BAND:(i + 1) * BAND, cols],
                               v_ref[(i - 1) * BAND:(i + 1) * BAND, cols], bias, (i, rho)))
        _band_blocks(blocks, store)

    l0, l1, l2 = l_scr[0], l_scr[1], l_scr[2]
    mx = jnp.maximum(jnp.maximum(l0, l1), l2)
    e0, e1, e2 = jnp.exp2(l0 - mx), jnp.exp2(l1 - mx), jnp.exp2(l2 - mx)
    tot = e0 + e1 + e2
    y = (e0 / tot) * o_scr[0] + (e1 / tot) * o_scr[1] + (e2 / tot) * o_scr[2]
    y_ref[...] = y.astype(BF16)


def _dilated(a_views, slopes):
    t = a_views[0].shape[1]
    dh = HEAD_DIM
    nb = t // DIL_TOKENS
    hq, hk, hv = 0, HPB, 2 * HPB
    args, specs = [], []
    for g, (_, r) in enumerate(DIL_PATTERNS):
        rows = DIL_TOKENS // r
        prev_per_blk = rows // BAND
        width = r * dh

        def cur(base, rows=rows, width=width):
            return pl.BlockSpec((None, rows, width), lambda j, b: (base + j, b, 0))

        def prev(base, width=width, ppb=prev_per_blk):
            return pl.BlockSpec((None, BAND, width),
                                lambda j, b: (base + j, jnp.maximum(b * ppb - 1, 0), 0))

        args += [a_views[g]] * 5
        specs += [cur(hq), cur(hk), cur(hv), prev(hk), prev(hv)]

    return pl.pallas_call(
        _dilated_kernel,
        grid=(HEADS_PER_DIL_GROUP, nb),
        in_specs=[pl.BlockSpec(memory_space=pltpu.SMEM)] + specs,
        out_specs=pl.BlockSpec((DIL_TOKENS, dh), lambda j, b: (b, j)),
        out_shape=jax.ShapeDtypeStruct((t, HEADS_PER_DIL_GROUP * dh), BF16),
        scratch_shapes=[pltpu.VMEM((N_DIL, DIL_TOKENS, dh), F32),
                        pltpu.VMEM((N_DIL, DIL_TOKENS, LANES), F32)],
        compiler_params=_params("arbitrary", "arbitrary"),
        name="dilated",
    )(slopes, *args)


MOBA_UNROLL = 4


MOBA_HEADS_PER_STEP = 8


def _moba_kernel(slopes_ref, q_ref, k_ref, vt_ref, y_ref, kmean_scr, sel_scr, acc_scr, s_scr, *, n_blocks):
    hg = pl.program_id(0)
    qb = pl.program_id(1)
    blk = MOBA_BLOCK
    n_h, _, dh = q_ref.shape
    heads = range(n_h)
    slope = [slopes_ref[N_HEADS_A + hg * n_h + hh] for hh in heads]

    @pl.when(qb == 0)
    def _():
        def body(n, carry):
            for hh in heads:
                kn = k_ref[hh, pl.ds(pl.multiple_of(n * blk, blk), blk), :].astype(F32)
                kmean_scr[hh, pl.ds(n, 1), :] = jnp.mean(kn, axis=0, keepdims=True)
            return carry
        lax.fori_loop(0, n_blocks, body, 0)

    q = [q_ref[hh] for hh in heads]

    blk_id = lax.broadcasted_iota(jnp.int32, (n_blocks, blk), 0)
    blk_id_f = blk_id.astype(F32)
    past = blk_id < qb
    key_i = lax.broadcasted_iota(jnp.int32, (blk, blk), 0)
    qry_i = lax.broadcasted_iota(jnp.int32, (blk, blk), 1)
    own = pl.multiple_of(qb * blk, blk)

    in_block_max = [slope[hh] * (blk - 1) for hh in heads]

    width = MOBA_UNROLL
    n_chunks = (qb + width - 1) // width

    def chunk_scores(hh, ci, slot):
        start = pl.multiple_of(ci * (width * blk), width * blk)
        s_scr[slot, hh] = lax.dot_general(k_ref[hh, pl.ds(start, width * blk), :], q[hh], _NT,
                                          preferred_element_type=F32)

    gates = [lax.dot_general(kmean_scr[hh], q[hh].astype(F32), _NT,
                             precision=lax.Precision.HIGHEST, preferred_element_type=F32) for hh in heads]
    s_own = [lax.dot_general(k_ref[hh, pl.ds(own, blk), :], q[hh], _NT, preferred_element_type=F32)
             for hh in heads]
    for hh in heads:
        chunk_scores(hh, 0, 0)

    m0, p_own = [], []
    for hh in heads:
        work = jnp.where(past, gates[hh], NEG_INF)
        sel = jnp.zeros((n_blocks, blk), F32)
        for _ in range(MOBA_TOPK):
            mx = jnp.max(work, axis=0, keepdims=True)
            first = jnp.min(jnp.where(work == mx, blk_id_f, float(n_blocks)), axis=0, keepdims=True)
            hit = blk_id_f == first
            sel = jnp.where(hit, 1.0, sel)
            work = jnp.where(hit, -jnp.inf, work)
        sel_scr[hh] = jnp.where(past, sel, 0.0)

        s = jnp.where(key_i <= qry_i, s_own[hh], NEG_INF)
        m0.append(jnp.max(s, axis=0, keepdims=True) + in_block_max[hh])
        p_own.append(jnp.exp2(s - m0[hh]).astype(BF16))
    for hh in heads:
        acc_scr[hh] = jnp.dot(vt_ref[hh, qb], p_own[hh], preferred_element_type=F32)

    def past_chunk(ci, m_run, src, dst):
        n0 = ci * width
        nxt = jnp.minimum(ci + 1, n_chunks - 1)
        out = []
        for hh in heads:
            chunk_scores(hh, nxt, dst)
            c, chosen = [], []
            m_chunk = jnp.full((1, blk), NEG_INF, F32)
            for a in range(width):
                s = s_scr[src, hh, a * blk:(a + 1) * blk, :]
                c.append(slope[hh] * jnp.full((1, blk), (n0 + a - qb) * blk, jnp.int32).astype(F32))
                chosen.append(sel_scr[hh, pl.ds(n0 + a, 1), :] > 0.5)
                m_blk = jnp.max(s, axis=0, keepdims=True) + (c[a] + in_block_max[hh])
                m_chunk = jnp.maximum(m_chunk, jnp.where(chosen[a], m_blk, NEG_INF))
            pv = jnp.zeros(acc_scr.shape[1:], F32)
            for a in range(width):
                s = s_scr[src, hh, a * blk:(a + 1) * blk, :]
                p = jnp.exp2(s - jnp.where(chosen[a], m_chunk - c[a], -NEG_INF))
                pv = pv + jnp.dot(vt_ref[hh, n0 + a], p.astype(BF16), preferred_element_type=F32)
            m_new = jnp.maximum(m_run[hh], m_chunk)
            alpha = jnp.exp2(m_run[hh] - m_new)
            beta = jnp.exp2(m_chunk - m_new)
            acc_scr[hh] = alpha * acc_scr[hh] + beta * pv
            out.append(m_new)
        return tuple(out)

    def body(ci, m_run):
        return lax.cond(lax.rem(ci, 2) == 0,
                        lambda m: past_chunk(ci, m, 0, 1), lambda m: past_chunk(ci, m, 1, 0), m_run)

    lax.fori_loop(0, n_chunks, body, tuple(m0))
    for hh in heads:
        y = acc_scr[hh, 0:dh, :] / acc_scr[hh, dh:dh + 1, :]
        y_ref[:, hh * dh:(hh + 1) * dh] = y.T.astype(BF16)


def _moba(bqk, bvt, slopes):
    _, t, dh = bqk.shape
    n_blocks = t // MOBA_BLOCK
    assert n_blocks % MOBA_UNROLL == 0, "the padded last chunk must stay inside the key array"
    n_h = MOBA_HEADS_PER_STEP
    n_groups = N_HEADS_B // n_h
    return pl.pallas_call(
        functools.partial(_moba_kernel, n_blocks=n_blocks),
        grid=(n_groups, n_blocks),
        in_specs=[pl.BlockSpec(memory_space=pltpu.SMEM),
                  pl.BlockSpec((n_h, MOBA_BLOCK, dh), lambda h, i: (h, i, 0)),
                  pl.BlockSpec((n_h, t, dh), lambda h, i: (n_groups + h, 0, 0), pipeline_mode=pl.Buffered(1)),
                  pl.BlockSpec((n_h, n_blocks, MOBA_VT_ROWS, MOBA_BLOCK), lambda h, i: (h, 0, 0, 0),
                               pipeline_mode=pl.Buffered(1))],
        out_specs=pl.BlockSpec((MOBA_BLOCK, n_h * dh), lambda h, i: (i, h)),
        out_shape=jax.ShapeDtypeStruct((t, N_HEADS_B * dh), BF16),
        scratch_shapes=[pltpu.VMEM((n_h, n_blocks, dh), F32),
                        pltpu.VMEM((n_h, n_blocks, MOBA_BLOCK), F32),
                        pltpu.VMEM((n_h, MOBA_VT_ROWS, MOBA_BLOCK), F32),
                        pltpu.VMEM((2, n_h, MOBA_UNROLL * MOBA_BLOCK, MOBA_BLOCK), F32)],
        compiler_params=_params("arbitrary", "arbitrary"),
        name="moba",
    )(slopes, bqk, bqk, bvt)


def _merge_kernel(x_ref, ada_ref, ya_ref, yb_ref, gates_a_ref, gates_b_ref, wa_ref, wb_ref, wo_ref, o_ref):
    pa = jnp.dot(ya_ref[...], wa_ref[...], preferred_element_type=F32)
    pb = jnp.dot(yb_ref[...], wb_ref[...], preferred_element_type=F32)
    merged = gates_a_ref[...].astype(F32) * pa + gates_b_ref[...].astype(F32) * pb
    out = jnp.dot(merged.astype(BF16), wo_ref[...], preferred_element_type=F32)
    o_ref[...] = x_ref[...] + ada_ref[5:6, :] * out


def _merge(x, ada, y_a, y_b, gates, w_a, w_b, w_o, *, tm=256):
    t, d = x.shape
    const = lambda i: (0, 0)
    return pl.pallas_call(
        _merge_kernel,
        grid=(t // tm,),
        in_specs=[pl.BlockSpec((tm, d), lambda i: (i, 0)),
                  pl.BlockSpec((N_ADA, d), const),
                  pl.BlockSpec((tm, y_a.shape[1]), lambda i: (i, 0)),
                  pl.BlockSpec((tm, y_b.shape[1]), lambda i: (i, 0)),
                  pl.BlockSpec((tm, d), lambda i: (i, 0)),
                  pl.BlockSpec((tm, d), lambda i: (i, 1)),
                  pl.BlockSpec(w_a.shape, const),
                  pl.BlockSpec(w_b.shape, const),
                  pl.BlockSpec(w_o.shape, const)],
        out_specs=pl.BlockSpec((tm, d), lambda i: (i, 0)),
        out_shape=jax.ShapeDtypeStruct((t, d), F32),
        compiler_params=_params("parallel"),
        name="merge",
    )(x, ada, y_a, y_b, gates, gates, w_a, w_b, w_o)


def _layer(x, c, w_ada, b_ada, g_ffn1, ffn1_w_gate, ffn1_w_up, ffn1_w_down,
           g_mix, w_in, q_norm, k_norm, w_gate, w_branch_a, w_branch_b, w_out,
           g_ffn2, ffn2_w_gate, ffn2_w_up, ffn2_w_down):
    t, d = x.shape
    ada = _ada(c, w_ada, b_ada).reshape(N_ADA, d)
    slopes = jnp.exp2(-8.0 * jnp.arange(1, N_HEADS + 1, dtype=F32) / N_HEADS) * LOG2_E

    x = _ffn(x, ada, g_ffn1, ffn1_w_gate, ffn1_w_up, ffn1_w_down, sub=0)

    scale = HEAD_DIM ** -0.5 * LOG2_E
    n_gate_heads = w_gate.shape[1] // HEAD_DIM
    gains = jnp.concatenate([q_norm[:N_HEADS_A] * scale, k_norm[:N_HEADS_A], jnp.ones((N_HEADS_A, HEAD_DIM), F32),
                             q_norm[N_HEADS_A:] * scale, k_norm[N_HEADS_A:],
                             jnp.ones((N_HEADS_B + n_gate_heads, HEAD_DIM), F32)], axis=0)
    norm_flags = jnp.concatenate([jnp.ones((2 * N_DIL,), F32), jnp.zeros((N_DIL,), F32),
                                  jnp.ones((BLK_BV0 - BLK_BQK0,), F32),
                                  jnp.zeros((BLK_GATE0 - BLK_BV0 + n_gate_heads // HPB,), F32)])
    offsets = jnp.arange(MOBA_BLOCK, dtype=F32)
    key_weights = jnp.exp2(slopes[N_HEADS_A:, None, None] * offsets[None, None, :])
    key_weights = jnp.pad(key_weights, ((0, 0), (0, MOBA_VT_PAD - 1), (0, 0)))
    a0, a1, a2, bqk, bvt, gates = _proj(_norm(x, ada, g_mix, sub=1), w_in, w_gate,
                                        gains, norm_flags, key_weights)

    y_a = _dilated((a0, a1, a2), slopes)
    y_b = _moba(bqk, bvt, slopes)

    x = _merge(x, ada, y_a, y_b, gates, w_branch_a.astype(BF16), w_branch_b.astype(BF16), w_out.astype(BF16))
    x = _ffn(x, ada, g_ffn2, ffn2_w_gate, ffn2_w_up, ffn2_w_down, sub=2)
    return x


def kernel(x, c, w_ada, b_ada, g_ffn1, ffn1_w_gate, ffn1_w_up, ffn1_w_down, g_mix, w_in, q_norm, k_norm,
           w_gate, w_branch_a, w_branch_b, w_out, g_ffn2, ffn2_w_gate, ffn2_w_up, ffn2_w_down):
    batch, depth = x.shape[0], w_ada.shape[0]
    outs = []
    for bi in range(batch):
        xb = x[bi]
        for l in range(depth):
            xb = _layer(xb, c[bi], w_ada[l], b_ada[l], g_ffn1[l], ffn1_w_gate[l], ffn1_w_up[l],
                        ffn1_w_down[l], g_mix[l], w_in[l], q_norm[l], k_norm[l], w_gate[l],
                        w_branch_a[l], w_branch_b[l], w_out[l], g_ffn2[l], ffn2_w_gate[l],
                        ffn2_w_up[l], ffn2_w_down[l])
        outs.append(xb)
    return jnp.stack(outs, axis=0)
```

```python
import functools

import jax
import jax.numpy as jnp
from jax import lax
from jax.experimental import pallas as pl
from jax.experimental.pallas import tpu as pltpu

HEAD_DIM = 128
DIL_PATTERNS = ((128, 1), (512, 4), (2048, 16))
N_DIL = len(DIL_PATTERNS)
HEADS_PER_DIL_GROUP = 4
N_HEADS_A = HEADS_PER_DIL_GROUP * N_DIL
N_HEADS_B = 8
N_HEADS = N_HEADS_A + N_HEADS_B
BAND = 128
MOBA_BLOCK = 256
MOBA_TOPK = 3
N_ADA = 9
EPS = 1e-6
NEG_INF = -1e30
LOG2_E = 1.4426950408889634

LANES = 128
VMEM_LIMIT = 60 * 1024 * 1024

BF16 = jnp.bfloat16
F32 = jnp.float32

_NT = (((1,), (1,)), ((), ()))


def _params(*sem):
    return pltpu.CompilerParams(dimension_semantics=sem, vmem_limit_bytes=VMEM_LIMIT)


def _sigmoid(x):
    return 1.0 / (1.0 + jnp.exp(-x))


def _ada_rows(ada_ref, sub):
    return tuple(ada_ref[3 * sub + k:3 * sub + k + 1, :] for k in range(3))


def _norm_modulate(x, g, shift, scale):
    ms = jnp.mean(x * x, axis=-1, keepdims=True)
    return ((x * lax.rsqrt(ms + EPS)) * (g * (1.0 + scale)) + shift).astype(BF16)


def _ada_kernel(c_ref, w_ref, b_ref, o_ref):
    c = c_ref[...]
    s = c * _sigmoid(c)
    o_ref[...] = jnp.sum(s * w_ref[...], axis=0, keepdims=True) + b_ref[...]


def _ada(c, w_ada, b_ada, *, tn=1024):
    d, n = w_ada.shape
    return pl.pallas_call(
        _ada_kernel,
        grid=(n // tn,),
        in_specs=[pl.BlockSpec((d, 1), lambda j: (0, 0)),
                  pl.BlockSpec((d, tn), lambda j: (0, j)),
                  pl.BlockSpec((1, tn), lambda j: (0, j))],
        out_specs=pl.BlockSpec((1, tn), lambda j: (0, j)),
        out_shape=jax.ShapeDtypeStruct((1, n), F32),
        compiler_params=_params("arbitrary"),
        name="ada",
    )(c.reshape(d, 1), w_ada, b_ada.reshape(1, n))


FFN_FIRST_ROWS = 256


def _ffn_kernel(x_ref, ada_ref, g_ref, wg_ref, wu_ref, wd_ref, o_ref, u_ref, *, sub):
    f = pl.program_id(1)

    def swiglu_tile(u, wg, wu, wd):
        hg = jnp.dot(u, wg, preferred_element_type=F32)
        hu = jnp.dot(u, wu, preferred_element_type=F32)
        h = (hg * _sigmoid(hg)) * hu
        return jnp.dot(h.astype(BF16), wd, preferred_element_type=F32)

    @pl.when(f == 0)
    def _():
        wg, wu, wd = wg_ref[...].astype(BF16), wu_ref[...].astype(BF16), wd_ref[...].astype(BF16)
        shift, scale, _ = _ada_rows(ada_ref, sub)
        for c in range(x_ref.shape[0] // FFN_FIRST_ROWS):
            rows = slice(c * FFN_FIRST_ROWS, (c + 1) * FFN_FIRST_ROWS)
            u = _norm_modulate(x_ref[rows, :], g_ref[...], shift, scale)
            u_ref[rows, :] = u
            o_ref[rows, :] = swiglu_tile(u, wg, wu, wd)

    @pl.when(f > 0)
    def _():
        o_ref[...] += swiglu_tile(u_ref[...], wg_ref[...].astype(BF16), wu_ref[...].astype(BF16),
                                  wd_ref[...].astype(BF16))

    @pl.when(f == pl.num_programs(1) - 1)
    def _():
        gate = _ada_rows(ada_ref, sub)[2]
        o_ref[...] = x_ref[...] + (0.5 * gate) * o_ref[...]


def _ffn(x, ada, g, w_gate, w_up, w_down, *, sub, tm=1024, tf=256):
    t, d = x.shape
    dff = w_gate.shape[1]
    return pl.pallas_call(
        functools.partial(_ffn_kernel, sub=sub),
        grid=(t // tm, dff // tf),
        in_specs=[pl.BlockSpec((tm, d), lambda i, f: (i, 0)),
                  pl.BlockSpec((N_ADA, d), lambda i, f: (0, 0)),
                  pl.BlockSpec((1, d), lambda i, f: (0, 0)),
                  pl.BlockSpec((d, tf), lambda i, f: (0, f)),
                  pl.BlockSpec((d, tf), lambda i, f: (0, f)),
                  pl.BlockSpec((tf, d), lambda i, f: (f, 0))],
        out_specs=pl.BlockSpec((tm, d), lambda i, f: (i, 0)),
        out_shape=jax.ShapeDtypeStruct((t, d), F32),
        scratch_shapes=[pltpu.VMEM((tm, d), BF16)],
        compiler_params=_params("parallel", "arbitrary"),
        name=f"ffn{sub}",
    )(x, ada, g.reshape(1, d), w_gate, w_up, w_down)


NORM_ROWS = 128


def _norm_kernel(x_ref, ada_ref, g_ref, u_ref, *, sub):
    shift, scale, _ = _ada_rows(ada_ref, sub)

    def body(c, carry):
        rows = pl.ds(pl.multiple_of(c * NORM_ROWS, NORM_ROWS), NORM_ROWS)
        u_ref[rows, :] = _norm_modulate(x_ref[rows, :], g_ref[...], shift, scale)
        return carry

    lax.fori_loop(0, x_ref.shape[0] // NORM_ROWS, body, 0)


def _norm(x, ada, g, *, sub, tm=512):
    t, d = x.shape
    return pl.pallas_call(
        functools.partial(_norm_kernel, sub=sub),
        grid=(t // tm,),
        in_specs=[pl.BlockSpec((tm, d), lambda i: (i, 0)),
                  pl.BlockSpec((N_ADA, d), lambda i: (0, 0)),
                  pl.BlockSpec((1, d), lambda i: (0, 0))],
        out_specs=pl.BlockSpec((tm, d), lambda i: (i, 0)),
        out_shape=jax.ShapeDtypeStruct((t, d), BF16),
        compiler_params=_params("parallel"),
        name="norm",
    )(x, ada, g.reshape(1, d))


HPB = HEADS_PER_DIL_GROUP
PROJ_TN = HPB * HEAD_DIM
BLK_A_END = 3 * N_DIL
BLK_BQK0 = BLK_A_END
BLK_BV0 = BLK_BQK0 + 2 * (N_HEADS_B // HPB)
BLK_GATE0 = BLK_BV0 + N_HEADS_B // HPB


MOBA_VT_PAD = 16
MOBA_VT_ROWS = HEAD_DIM + MOBA_VT_PAD


PROJ_ROWS = MOBA_BLOCK


def _proj_kernel(nflag_ref, u_ref, win_ref, wgate_ref, gain_ref, kw_ref,
                 a0_ref, a1_ref, a2_ref, bqk_ref, bvt_ref, gate_ref, stg_ref, stg2_ref):
    j = pl.program_id(1)
    n_chunks = u_ref.shape[0] // PROJ_ROWS

    def chunks(w_ref):
        w = w_ref[...].astype(BF16)
        for c in range(n_chunks):
            rows = slice(c * PROJ_ROWS, (c + 1) * PROJ_ROWS)
            yield c, jnp.dot(u_ref[rows, :], w, preferred_element_type=F32)

    def head(res, h):
        a = res[:, h * HEAD_DIM:(h + 1) * HEAD_DIM]
        flag = nflag_ref[j]
        ms = jnp.mean(a * a, axis=-1, keepdims=True)
        return (a * (lax.rsqrt(ms + EPS) * flag + (1.0 - flag))) * gain_ref[j, h:h + 1, :]

    def dilated_block(o_ref, r):
        n = PROJ_ROWS // r
        for c, res in chunks(win_ref):
            for h in range(HPB):
                y = head(res, h)
                if r == 1:
                    o_ref[h, c * n:(c + 1) * n, :] = y.astype(BF16)
                    continue
                stg = stg_ref.at[c % 2, h]
                stg[...] = y
                if r == 16:
                    stg2 = stg2_ref.at[c % 2, h]
                    m = PROJ_ROWS // 4
                    for a in range(4):
                        stg2[a * m:(a + 1) * m, :] = stg[pl.ds(a, m, stride=4), :]
                    parts = {a + 4 * b: stg2[pl.ds(a * m + b, n, stride=4), :] for a in range(4) for b in range(4)}
                else:
                    parts = {rho: stg[pl.ds(rho, n, stride=r), :] for rho in range(r)}
                for rho in range(r):
                    o_ref[h, c * n:(c + 1) * n, rho * HEAD_DIM:(rho + 1) * HEAD_DIM] = parts[rho].astype(BF16)

    a_refs = (a0_ref, a1_ref, a2_ref)
    for g, (_, r) in enumerate(DIL_PATTERNS):
        pl.when((j < BLK_A_END) & (lax.rem(j, N_DIL) == g))(functools.partial(dilated_block, a_refs[g], r))

    @pl.when((j >= BLK_BQK0) & (j < BLK_BV0))
    def _():
        for c, res in chunks(win_ref):
            for h in range(HPB):
                bqk_ref[h, c * PROJ_ROWS:(c + 1) * PROJ_ROWS, :] = head(res, h).astype(BF16)

    @pl.when((j >= BLK_BV0) & (j < BLK_GATE0))
    def _():
        for c, res in chunks(win_ref):
            for h in range(HPB):
                kw = kw_ref[j - BLK_BV0, h]
                yt = res[:, h * HEAD_DIM:(h + 1) * HEAD_DIM].T
                bvt_ref[h, c, 0:HEAD_DIM, :] = (yt * kw[0:1, :]).astype(BF16)
                bvt_ref[h, c, HEAD_DIM:, :] = kw.astype(BF16)

    @pl.when(j >= BLK_GATE0)
    def _():
        for c, res in chunks(wgate_ref):
            gate_ref[c * PROJ_ROWS:(c + 1) * PROJ_ROWS, :] = _sigmoid(res).astype(BF16)


def _proj(u, w_in, w_gate, gains, norm_flags, key_weights, *, tm=2048):
    t, d = u.shape
    n_gate = w_gate.shape[1] // PROJ_TN
    n_blk = BLK_GATE0 + n_gate
    assert w_in.shape[1] == BLK_GATE0 * PROJ_TN
    n_bv = BLK_GATE0 - BLK_BV0

    def a_spec(gi, r):
        return pl.BlockSpec((HPB, tm // r, r * HEAD_DIM),
                            lambda i, j: (jnp.clip((j - gi + N_DIL - 1) // N_DIL, 0, 2), i, 0))

    a_shapes = [jax.ShapeDtypeStruct((3 * HPB, t // r, r * HEAD_DIM), BF16) for _, r in DIL_PATTERNS]
    out_shape = a_shapes + [
        jax.ShapeDtypeStruct((2 * N_HEADS_B, t, HEAD_DIM), BF16),
        jax.ShapeDtypeStruct((N_HEADS_B, t // MOBA_BLOCK, MOBA_VT_ROWS, MOBA_BLOCK), BF16),
        jax.ShapeDtypeStruct((t, n_gate * PROJ_TN), BF16)]
    out_specs = [a_spec(gi, r) for gi, (_, r) in enumerate(DIL_PATTERNS)] + [
        pl.BlockSpec((HPB, tm, HEAD_DIM),
                     lambda i, j: (jnp.clip(j - BLK_BQK0, 0, BLK_BV0 - BLK_BQK0 - 1), i, 0)),
        pl.BlockSpec((HPB, tm // MOBA_BLOCK, MOBA_VT_ROWS, MOBA_BLOCK),
                     lambda i, j: (jnp.clip(j - BLK_BV0, 0, n_bv - 1), i, 0, 0)),
        pl.BlockSpec((tm, PROJ_TN), lambda i, j: (i, jnp.clip(j - BLK_GATE0, 0, n_gate - 1)))]
    return pl.pallas_call(
        _proj_kernel,
        grid=(t // tm, n_blk),
        in_specs=[pl.BlockSpec(memory_space=pltpu.SMEM),
                  pl.BlockSpec((tm, d), lambda i, j: (i, 0), pipeline_mode=pl.Buffered(1)),
                  pl.BlockSpec((d, PROJ_TN), lambda i, j: (0, jnp.minimum(j, BLK_GATE0 - 1))),
                  pl.BlockSpec((d, PROJ_TN), lambda i, j: (0, jnp.clip(j - BLK_GATE0, 0, n_gate - 1))),
                  pl.BlockSpec((n_blk, HPB, HEAD_DIM), lambda i, j: (0, 0, 0)),
                  pl.BlockSpec((n_bv, HPB, MOBA_VT_PAD, MOBA_BLOCK), lambda i, j: (0, 0, 0, 0))],
        out_specs=out_specs,
        out_shape=out_shape,
        scratch_shapes=[pltpu.VMEM((2, HPB, PROJ_ROWS, HEAD_DIM), F32)] * 2,
        compiler_params=_params("arbitrary", "arbitrary"),
        name="proj",
    )(norm_flags, u, w_in, w_gate, gains.reshape(n_blk, HPB, HEAD_DIM),
      key_weights.reshape(n_bv, HPB, MOBA_VT_PAD, MOBA_BLOCK))


DIL_TOKENS = BAND * max(r for _, r in DIL_PATTERNS)


def _band_blocks(blocks, store):
    scores = [lax.dot_general(q, k2, _NT, preferred_element_type=F32) + bias
              for q, k2, _, bias, _ in blocks]
    soft = []
    for s in scores:
        m = jnp.max(s, axis=-1, keepdims=True)
        p = jnp.exp2(s - m)
        denom = jnp.sum(p, axis=-1, keepdims=True)
        soft.append((p.astype(BF16), denom, jnp.broadcast_to(m + jnp.log2(denom), (BAND, LANES))))
    for (p, denom, lse), (_, _, v2, _, args) in zip(soft, blocks):
        o = jnp.dot(p, v2, preferred_element_type=F32) / denom
        store(*args, o, lse)


def _dilated_kernel(slopes_ref, *refs):
    groups = [refs[5 * g:5 * g + 5] for g in range(N_DIL)]
    y_ref, o_scr, l_scr = refs[5 * N_DIL:]
    j = pl.program_id(0)
    b = pl.program_id(1)

    qi = lax.broadcasted_iota(jnp.int32, (BAND, 2 * BAND), 0)
    ki = lax.broadcasted_iota(jnp.int32, (BAND, 2 * BAND), 1)
    dist = BAND + qi - ki
    in_band = (dist >= 0) & (dist <= BAND)
    dist_f = dist.astype(F32)
    first_ok = (ki + jnp.minimum(b, 1) * BAND) >= BAND

    for g, (_, r) in enumerate(DIL_PATTERNS):
        q_ref, k_ref, v_ref, kp_ref, vp_ref = groups[g]
        slope = slopes_ref[g * HEADS_PER_DIL_GROUP + j]
        bias = jnp.where(in_band, dist_f * (-slope * r), NEG_INF)
        bias_first = jnp.where(first_ok, bias, NEG_INF)
        n_blk = DIL_TOKENS // (r * BAND)

        def store(i, rho, o, lse, g=g, r=r):
            rows = pl.ds(i * (BAND * r) + rho, BAND, stride=r) if r > 1 else pl.ds(i * BAND, BAND)
            o_scr.at[g][rows, :] = o
            l_scr.at[g][rows, :] = lse

        blocks = []
        for rho in range(r):
            cols = slice(rho * HEAD_DIM, (rho + 1) * HEAD_DIM)
            kk = jnp.concatenate([kp_ref[:, cols], k_ref[0:BAND, cols]], axis=0)
            vv = jnp.concatenate([vp_ref[:, cols], v_ref[0:BAND, cols]], axis=0)
            blocks.append((q_ref[0:BAND, cols], kk, vv, bias_first, (0, rho)))
            for i in range(1, n_blk):
                blocks.append((q_ref[i * BAND:(i + 1) * BAND, cols],
                               k_ref[(i - 1) * BAND:(i + 1) * BAND, cols],
                               v_ref[(i - 1) * BAND:(i + 1) * BAND, cols], bias, (i, rho)))
        _band_blocks(blocks, store)

    lses = [l_scr[g] for g in range(N_DIL)]
    mx = functools.reduce(jnp.maximum, lses)
    es = [jnp.exp2(lse - mx) for lse in lses]
    tot = functools.reduce(jnp.add, es)
    y = functools.reduce(jnp.add, [(e / tot) * o_scr[g] for g, e in enumerate(es)])
    y_ref[...] = y.astype(BF16)


def _dilated(a_views, slopes):
    t = a_views[0].shape[1]
    dh = HEAD_DIM
    nb = t // DIL_TOKENS
    hq, hk, hv = 0, HPB, 2 * HPB
    args, specs = [], []
    for g, (_, r) in enumerate(DIL_PATTERNS):
        rows = DIL_TOKENS // r
        prev_per_blk = rows // BAND
        width = r * dh

        def cur(base, rows=rows, width=width):
            return pl.BlockSpec((None, rows, width), lambda j, b: (base + j, b, 0))

        def prev(base, width=width, ppb=prev_per_blk):
            return pl.BlockSpec((None, BAND, width),
                                lambda j, b: (base + j, jnp.maximum(b * ppb - 1, 0), 0))

        args += [a_views[g]] * 5
        specs += [cur(hq), cur(hk), cur(hv), prev(hk), prev(hv)]

    return pl.pallas_call(
        _dilated_kernel,
        grid=(HEADS_PER_DIL_GROUP, nb),
        in_specs=[pl.BlockSpec(memory_space=pltpu.SMEM)] + specs,
        out_specs=pl.BlockSpec((DIL_TOKENS, dh), lambda j, b: (b, j)),
        out_shape=jax.ShapeDtypeStruct((t, HEADS_PER_DIL_GROUP * dh), BF16),
        scratch_shapes=[pltpu.VMEM((N_DIL, DIL_TOKENS, dh), F32),
                        pltpu.VMEM((N_DIL, DIL_TOKENS, LANES), F32)],
        compiler_params=_params("arbitrary", "arbitrary"),
        name="dilated",
    )(slopes, *args)


MOBA_UNROLL = 4


MOBA_HEADS_PER_STEP = 8


def _moba_kernel(slopes_ref, q_ref, k_ref, vt_ref, y_ref, kmean_scr, sel_scr, acc_scr, s_scr, *, n_blocks):
    hg = pl.program_id(0)
    qb = pl.program_id(1)
    blk = MOBA_BLOCK
    n_h, _, dh = q_ref.shape
    heads = range(n_h)
    slope = [slopes_ref[N_HEADS_A + hg * n_h + hh] for hh in heads]

    @pl.when(qb == 0)
    def _():
        def body(n, carry):
            for hh in heads:
                kn = k_ref[hh, pl.ds(pl.multiple_of(n * blk, blk), blk), :].astype(F32)
                kmean_scr[hh, pl.ds(n, 1), :] = jnp.mean(kn, axis=0, keepdims=True)
            return carry
        lax.fori_loop(0, n_blocks, body, 0)

    q = [q_ref[hh] for hh in heads]

    blk_id = lax.broadcasted_iota(jnp.int32, (n_blocks, blk), 0)
    blk_id_f = blk_id.astype(F32)
    past = blk_id < qb
    key_i = lax.broadcasted_iota(jnp.int32, (blk, blk), 0)
    qry_i = lax.broadcasted_iota(jnp.int32, (blk, blk), 1)
    own = pl.multiple_of(qb * blk, blk)

    in_block_max = [slope[hh] * (blk - 1) for hh in heads]

    width = MOBA_UNROLL
    n_chunks = (qb + width - 1) // width

    def chunk_scores(hh, ci, slot):
        start = pl.multiple_of(ci * (width * blk), width * blk)
        s_scr[slot, hh] = lax.dot_general(k_ref[hh, pl.ds(start, width * blk), :], q[hh], _NT,
                                          preferred_element_type=F32)

    gates = [lax.dot_general(kmean_scr[hh], q[hh].astype(F32), _NT,
                             precision=lax.Precision.HIGHEST, preferred_element_type=F32) for hh in heads]
    s_own = [lax.dot_general(k_ref[hh, pl.ds(own, blk), :], q[hh], _NT, preferred_element_type=F32)
             for hh in heads]
    for hh in heads:
        chunk_scores(hh, 0, 0)

    m0, p_own = [], []
    for hh in heads:
        work = jnp.where(past, gates[hh], NEG_INF)
        sel = jnp.zeros((n_blocks, blk), F32)
        for _ in range(MOBA_TOPK):
            mx = jnp.max(work, axis=0, keepdims=True)
            first = jnp.min(jnp.where(work == mx, blk_id_f, float(n_blocks)), axis=0, keepdims=True)
            hit = blk_id_f == first
            sel = jnp.where(hit, 1.0, sel)
            work = jnp.where(hit, -jnp.inf, work)
        sel_scr[hh] = jnp.where(past, sel, 0.0)

        s = jnp.where(key_i <= qry_i, s_own[hh], NEG_INF)
        m0.append(jnp.max(s, axis=0, keepdims=True) + in_block_max[hh])
        p_own.append(jnp.exp2(s - m0[hh]).astype(BF16))
    for hh in heads:
        acc_scr[hh] = jnp.dot(vt_ref[hh, qb], p_own[hh], preferred_element_type=F32)

    def past_chunk(ci, m_run, src, dst):
        n0 = ci * width
        nxt = jnp.minimum(ci + 1, n_chunks - 1)
        out = []
        for hh in heads:
            chunk_scores(hh, nxt, dst)
            c, chosen = [], []
            m_chunk = jnp.full((1, blk), NEG_INF, F32)
            for a in range(width):
                s = s_scr[src, hh, a * blk:(a + 1) * blk, :]
                c.append(slope[hh] * jnp.full((1, blk), (n0 + a - qb) * blk, jnp.int32).astype(F32))
                chosen.append(sel_scr[hh, pl.ds(n0 + a, 1), :] > 0.5)
                m_blk = jnp.max(s, axis=0, keepdims=True) + (c[a] + in_block_max[hh])
                m_chunk = jnp.maximum(m_chunk, jnp.where(chosen[a], m_blk, NEG_INF))
            pv = jnp.zeros(acc_scr.shape[1:], F32)
            for a in range(width):
                s = s_scr[src, hh, a * blk:(a + 1) * blk, :]
                p = jnp.exp2(s - jnp.where(chosen[a], m_chunk - c[a], -NEG_INF))
                pv = pv + jnp.dot(vt_ref[hh, n0 + a], p.astype(BF16), preferred_element_type=F32)
            m_new = jnp.maximum(m_run[hh], m_chunk)
            alpha = jnp.exp2(m_run[hh] - m_new)
            beta = jnp.exp2(m_chunk - m_new)
            acc_scr[hh] = alpha * acc_scr[hh] + beta * pv
            out.append(m_new)
        return tuple(out)

    def body(ci, m_run):
        return lax.cond(lax.rem(ci, 2) == 0,
                        lambda m: past_chunk(ci, m, 0, 1), lambda m: past_chunk(ci, m, 1, 0), m_run)

    lax.fori_loop(0, n_chunks, body, tuple(m0))
    for hh in heads:
        y = acc_scr[hh, 0:dh, :] / acc_scr[hh, dh:dh + 1, :]
        y_ref[:, hh * dh:(hh + 1) * dh] = y.T.astype(BF16)


def _moba(bqk, bvt, slopes):
    _, t, dh = bqk.shape
    n_blocks = t // MOBA_BLOCK
    assert n_blocks % MOBA_UNROLL == 0, "the padded last chunk must stay inside the key array"
    n_h = MOBA_HEADS_PER_STEP
    n_groups = N_HEADS_B // n_h
    return pl.pallas_call(
        functools.partial(_moba_kernel, n_blocks=n_blocks),
        grid=(n_groups, n_blocks),
        in_specs=[pl.BlockSpec(memory_space=pltpu.SMEM),
                  pl.BlockSpec((n_h, MOBA_BLOCK, dh), lambda h, i: (h, i, 0)),
                  pl.BlockSpec((n_h, t, dh), lambda h, i: (n_groups + h, 0, 0), pipeline_mode=pl.Buffered(1)),
                  pl.BlockSpec((n_h, n_blocks, MOBA_VT_ROWS, MOBA_BLOCK), lambda h, i: (h, 0, 0, 0),
                               pipeline_mode=pl.Buffered(1))],
        out_specs=pl.BlockSpec((MOBA_BLOCK, n_h * dh), lambda h, i: (i, h)),
        out_shape=jax.ShapeDtypeStruct((t, N_HEADS_B * dh), BF16),
        scratch_shapes=[pltpu.VMEM((n_h, n_blocks, dh), F32),
                        pltpu.VMEM((n_h, n_blocks, MOBA_BLOCK), F32),
                        pltpu.VMEM((n_h, MOBA_VT_ROWS, MOBA_BLOCK), F32),
                        pltpu.VMEM((2, n_h, MOBA_UNROLL * MOBA_BLOCK, MOBA_BLOCK), F32)],
        compiler_params=_params("arbitrary", "arbitrary"),
        name="moba",
    )(slopes, bqk, bqk, bvt)


def _merge_kernel(x_ref, ada_ref, ya_ref, yb_ref, gates_a_ref, gates_b_ref, wa_ref, wb_ref, wo_ref, o_ref):
    pa = jnp.dot(ya_ref[...], wa_ref[...], preferred_element_type=F32)
    pb = jnp.dot(yb_ref[...], wb_ref[...], preferred_element_type=F32)
    merged = gates_a_ref[...].astype(F32) * pa + gates_b_ref[...].astype(F32) * pb
    out = jnp.dot(merged.astype(BF16), wo_ref[...], preferred_element_type=F32)
    o_ref[...] = x_ref[...] + _ada_rows(ada_ref, 1)[2] * out


def _merge(x, ada, y_a, y_b, gates, w_a, w_b, w_o, *, tm=256):
    t, d = x.shape
    const = lambda i: (0, 0)
    return pl.pallas_call(
        _merge_kernel,
        grid=(t // tm,),
        in_specs=[pl.BlockSpec((tm, d), lambda i: (i, 0)),
                  pl.BlockSpec((N_ADA, d), const),
                  pl.BlockSpec((tm, y_a.shape[1]), lambda i: (i, 0)),
                  pl.BlockSpec((tm, y_b.shape[1]), lambda i: (i, 0)),
                  pl.BlockSpec((tm, d), lambda i: (i, 0)),
                  pl.BlockSpec((tm, d), lambda i: (i, 1)),
                  pl.BlockSpec(w_a.shape, const),
                  pl.BlockSpec(w_b.shape, const),
                  pl.BlockSpec(w_o.shape, const)],
        out_specs=pl.BlockSpec((tm, d), lambda i: (i, 0)),
        out_shape=jax.ShapeDtypeStruct((t, d), F32),
        compiler_params=_params("parallel"),
        name="merge",
    )(x, ada, y_a, y_b, gates, gates, w_a, w_b, w_o)


def _layer(x, c, w_ada, b_ada, g_ffn1, ffn1_w_gate, ffn1_w_up, ffn1_w_down,
           g_mix, w_in, q_norm, k_norm, w_gate, w_branch_a, w_branch_b, w_out,
           g_ffn2, ffn2_w_gate, ffn2_w_up, ffn2_w_down):
    t, d = x.shape
    ada = _ada(c, w_ada, b_ada).reshape(N_ADA, d)
    slopes = jnp.exp2(-8.0 * jnp.arange(1, N_HEADS + 1, dtype=F32) / N_HEADS) * LOG2_E

    x = _ffn(x, ada, g_ffn1, ffn1_w_gate, ffn1_w_up, ffn1_w_down, sub=0)

    scale = HEAD_DIM ** -0.5 * LOG2_E
    n_gate_heads = w_gate.shape[1] // HEAD_DIM
    gains = jnp.concatenate([q_norm[:N_HEADS_A] * scale, k_norm[:N_HEADS_A], jnp.ones((N_HEADS_A, HEAD_DIM), F32),
                             q_norm[N_HEADS_A:] * scale, k_norm[N_HEADS_A:],
                             jnp.ones((N_HEADS_B + n_gate_heads, HEAD_DIM), F32)], axis=0)
    norm_flags = jnp.concatenate([jnp.ones((2 * N_DIL,), F32), jnp.zeros((N_DIL,), F32),
                                  jnp.ones((BLK_BV0 - BLK_BQK0,), F32),
                                  jnp.zeros((BLK_GATE0 - BLK_BV0 + n_gate_heads // HPB,), F32)])
    offsets = jnp.arange(MOBA_BLOCK, dtype=F32)
    key_weights = jnp.exp2(slopes[N_HEADS_A:, None, None] * offsets[None, None, :])
    key_weights = jnp.pad(key_weights, ((0, 0), (0, MOBA_VT_PAD - 1), (0, 0)))
    a0, a1, a2, bqk, bvt, gates = _proj(_norm(x, ada, g_mix, sub=1), w_in, w_gate,
                                        gains, norm_flags, key_weights)

    y_a = _dilated((a0, a1, a2), slopes)
    y_b = _moba(bqk, bvt, slopes)

    x = _merge(x, ada, y_a, y_b, gates, w_branch_a.astype(BF16), w_branch_b.astype(BF16), w_out.astype(BF16))
    x = _ffn(x, ada, g_ffn2, ffn2_w_gate, ffn2_w_up, ffn2_w_down, sub=2)
    return x


def kernel(x, c, w_ada, b_ada, g_ffn1, ffn1_w_gate, ffn1_w_up, ffn1_w_down, g_mix, w_in, q_norm, k_norm,
           w_gate, w_branch_a, w_branch_b, w_out, g_ffn2, ffn2_w_gate, ffn2_w_up, ffn2_w_down):
    batch, depth = x.shape[0], w_ada.shape[0]
    outs = []
    for bi in range(batch):
        xb = x[bi]
        for l in range(depth):
            xb = _layer(xb, c[bi], w_ada[l], b_ada[l], g_ffn1[l], ffn1_w_gate[l], ffn1_w_up[l],
                        ffn1_w_down[l], g_mix[l], w_in[l], q_norm[l], k_norm[l], w_gate[l],
                        w_branch_a[l], w_branch_b[l], w_out[l], g_ffn2[l], ffn2_w_gate[l],
                        ffn2_w_up[l], ffn2_w_down[l])
        outs.append(xb)
    return jnp.stack(outs, axis=0)
```

```python
import functools

import jax
import jax.numpy as jnp
from jax import lax
from jax.experimental import pallas as pl
from jax.experimental.pallas import tpu as pltpu

HEAD_DIM = 128
DIL_PATTERNS = ((128, 1), (512, 4), (2048, 16))
N_DIL = len(DIL_PATTERNS)
HEADS_PER_DIL_GROUP = 4
N_HEADS_A = HEADS_PER_DIL_GROUP * N_DIL
N_HEADS_B = 8
N_HEADS = N_HEADS_A + N_HEADS_B
BAND = 128
MOBA_BLOCK = 256
MOBA_TOPK = 3
N_ADA = 9
EPS = 1e-6
NEG_INF = -1e30
LOG2_E = 1.4426950408889634

LANES = 128
VMEM_LIMIT = 60 * 1024 * 1024

BF16 = jnp.bfloat16
F32 = jnp.float32

_NT = (((1,), (1,)), ((), ()))


def _params(*sem):
    return pltpu.CompilerParams(dimension_semantics=sem, vmem_limit_bytes=VMEM_LIMIT)


def _sigmoid(x):
    return 1.0 / (1.0 + jnp.exp(-x))


def _ada_rows(ada_ref, sub):
    return tuple(ada_ref[3 * sub + k:3 * sub + k + 1, :] for k in range(3))


def _norm_modulate(x, g, shift, scale):
    ms = jnp.mean(x * x, axis=-1, keepdims=True)
    return ((x * lax.rsqrt(ms + EPS)) * (g * (1.0 + scale)) + shift).astype(BF16)


def _ada_kernel(c_ref, w_ref, b_ref, o_ref):
    c = c_ref[...]
    s = c * _sigmoid(c)
    o_ref[...] = jnp.sum(s * w_ref[...], axis=0, keepdims=True) + b_ref[...]


def _ada(c, w_ada, b_ada, *, tn=1024):
    d, n = w_ada.shape
    return pl.pallas_call(
        _ada_kernel,
        grid=(n // tn,),
        in_specs=[pl.BlockSpec((d, 1), lambda j: (0, 0)),
                  pl.BlockSpec((d, tn), lambda j: (0, j)),
                  pl.BlockSpec((1, tn), lambda j: (0, j))],
        out_specs=pl.BlockSpec((1, tn), lambda j: (0, j)),
        out_shape=jax.ShapeDtypeStruct((1, n), F32),
        compiler_params=_params("arbitrary"),
        name="ada",
    )(c.reshape(d, 1), w_ada, b_ada.reshape(1, n))


FFN_FIRST_ROWS = 256


def _ffn_kernel(x_ref, ada_ref, g_ref, wg_ref, wu_ref, wd_ref, o_ref, u_ref, *, sub):
    f = pl.program_id(1)

    def swiglu_tile(u, wg, wu, wd):
        hg = jnp.dot(u, wg, preferred_element_type=F32)
        hu = jnp.dot(u, wu, preferred_element_type=F32)
        h = (hg * _sigmoid(hg)) * hu
        return jnp.dot(h.astype(BF16), wd, preferred_element_type=F32)

    @pl.when(f == 0)
    def _():
        wg, wu, wd = wg_ref[...].astype(BF16), wu_ref[...].astype(BF16), wd_ref[...].astype(BF16)
        shift, scale, _ = _ada_rows(ada_ref, sub)
        for c in range(x_ref.shape[0] // FFN_FIRST_ROWS):
            rows = slice(c * FFN_FIRST_ROWS, (c + 1) * FFN_FIRST_ROWS)
            u = _norm_modulate(x_ref[rows, :], g_ref[...], shift, scale)
            u_ref[rows, :] = u
            o_ref[rows, :] = swiglu_tile(u, wg, wu, wd)

    @pl.when(f > 0)
    def _():
        o_ref[...] += swiglu_tile(u_ref[...], wg_ref[...].astype(BF16), wu_ref[...].astype(BF16),
                                  wd_ref[...].astype(BF16))

    @pl.when(f == pl.num_programs(1) - 1)
    def _():
        gate = _ada_rows(ada_ref, sub)[2]
        o_ref[...] = x_ref[...] + (0.5 * gate) * o_ref[...]


def _ffn(x, ada, g, w_gate, w_up, w_down, *, sub, tm=1024, tf=256):
    t, d = x.shape
    dff = w_gate.shape[1]
    return pl.pallas_call(
        functools.partial(_ffn_kernel, sub=sub),
        grid=(t // tm, dff // tf),
        in_specs=[pl.BlockSpec((tm, d), lambda i, f: (i, 0)),
                  pl.BlockSpec((N_ADA, d), lambda i, f: (0, 0)),
                  pl.BlockSpec((1, d), lambda i, f: (0, 0)),
                  pl.BlockSpec((d, tf), lambda i, f: (0, f)),
                  pl.BlockSpec((d, tf), lambda i, f: (0, f)),
                  pl.BlockSpec((tf, d), lambda i, f: (f, 0))],
        out_specs=pl.BlockSpec((tm, d), lambda i, f: (i, 0)),
        out_shape=jax.ShapeDtypeStruct((t, d), F32),
        scratch_shapes=[pltpu.VMEM((tm, d), BF16)],
        compiler_params=_params("parallel", "arbitrary"),
        name=f"ffn{sub}",
    )(x, ada, g.reshape(1, d), w_gate, w_up, w_down)


NORM_ROWS = 128


def _norm_kernel(x_ref, ada_ref, g_ref, u_ref, *, sub):
    shift, scale, _ = _ada_rows(ada_ref, sub)

    def body(c, carry):
        rows = pl.ds(pl.multiple_of(c * NORM_ROWS, NORM_ROWS), NORM_ROWS)
        u_ref[rows, :] = _norm_modulate(x_ref[rows, :], g_ref[...], shift, scale)
        return carry

    lax.fori_loop(0, x_ref.shape[0] // NORM_ROWS, body, 0)


def _norm(x, ada, g, *, sub, tm=512):
    t, d = x.shape
    return pl.pallas_call(
        functools.partial(_norm_kernel, sub=sub),
        grid=(t // tm,),
        in_specs=[pl.BlockSpec((tm, d), lambda i: (i, 0)),
                  pl.BlockSpec((N_ADA, d), lambda i: (0, 0)),
                  pl.BlockSpec((1, d), lambda i: (0, 0))],
        out_specs=pl.BlockSpec((tm, d), lambda i: (i, 0)),
        out_shape=jax.ShapeDtypeStruct((t, d), BF16),
        compiler_params=_params("parallel"),
        name="norm",
    )(x, ada, g.reshape(1, d))


HPB = HEADS_PER_DIL_GROUP
PROJ_TN = HPB * HEAD_DIM
BLK_A_END = 3 * N_DIL
BLK_BQK0 = BLK_A_END
BLK_BV0 = BLK_BQK0 + 2 * (N_HEADS_B // HPB)
BLK_GATE0 = BLK_BV0 + N_HEADS_B // HPB


MOBA_VT_PAD = 16
MOBA_VT_ROWS = HEAD_DIM + MOBA_VT_PAD


PROJ_ROWS = MOBA_BLOCK


def _proj_kernel(nflag_ref, u_ref, win_ref, wgate_ref, gain_ref, kw_ref,
                 a0_ref, a1_ref, a2_ref, bqk_ref, bvt_ref, gate_ref, stg_ref, stg2_ref):
    j = pl.program_id(1)
    n_chunks = u_ref.shape[0] // PROJ_ROWS

    def chunks(w_ref):
        w = w_ref[...].astype(BF16)
        for c in range(n_chunks):
            rows = slice(c * PROJ_ROWS, (c + 1) * PROJ_ROWS)
            yield c, jnp.dot(u_ref[rows, :], w, preferred_element_type=F32)

    def head(res, h):
        a = res[:, h * HEAD_DIM:(h + 1) * HEAD_DIM]
        flag = nflag_ref[j]
        ms = jnp.mean(a * a, axis=-1, keepdims=True)
        return (a * (lax.rsqrt(ms + EPS) * flag + (1.0 - flag))) * gain_ref[j, h:h + 1, :]

    def dilated_block(o_ref, r):
        n = PROJ_ROWS // r
        for c, res in chunks(win_ref):
            for h in range(HPB):
                y = head(res, h)
                if r == 1:
                    o_ref[h, c * n:(c + 1) * n, :] = y.astype(BF16)
                    continue
                stg = stg_ref.at[h]
                stg[...] = y
                if r == 16:
                    stg2 = stg2_ref
                    m = PROJ_ROWS // 4
                    for a in range(4):
                        stg2[a * m:(a + 1) * m, :] = stg[pl.ds(a, m, stride=4), :]
                    parts = {a + 4 * b: stg2[pl.ds(a * m + b, n, stride=4), :] for a in range(4) for b in range(4)}
                else:
                    parts = {rho: stg[pl.ds(rho, n, stride=r), :] for rho in range(r)}
                for rho in range(r):
                    o_ref[h, c * n:(c + 1) * n, rho * HEAD_DIM:(rho + 1) * HEAD_DIM] = parts[rho].astype(BF16)

    a_refs = (a0_ref, a1_ref, a2_ref)
    for g, (_, r) in enumerate(DIL_PATTERNS):
        pl.when((j < BLK_A_END) & (lax.rem(j, N_DIL) == g))(functools.partial(dilated_block, a_refs[g], r))

    @pl.when((j >= BLK_BQK0) & (j < BLK_BV0))
    def _():
        for c, res in chunks(win_ref):
            for h in range(HPB):
                bqk_ref[h, c * PROJ_ROWS:(c + 1) * PROJ_ROWS, :] = head(res, h).astype(BF16)

    @pl.when((j >= BLK_BV0) & (j < BLK_GATE0))
    def _():
        for c, res in chunks(win_ref):
            for h in range(HPB):
                kw = kw_ref[j - BLK_BV0, h]
                yt = res[:, h * HEAD_DIM:(h + 1) * HEAD_DIM].T
                bvt_ref[h, c, 0:HEAD_DIM, :] = (yt * kw[0:1, :]).astype(BF16)
                bvt_ref[h, c, HEAD_DIM:, :] = kw.astype(BF16)

    @pl.when(j >= BLK_GATE0)
    def _():
        for c, res in chunks(wgate_ref):
            gate_ref[c * PROJ_ROWS:(c + 1) * PROJ_ROWS, :] = _sigmoid(res).astype(BF16)


def _proj(u, w_in, w_gate, gains, norm_flags, key_weights, *, tm=2048):
    t, d = u.shape
    n_gate = w_gate.shape[1] // PROJ_TN
    n_blk = BLK_GATE0 + n_gate
    assert w_in.shape[1] == BLK_GATE0 * PROJ_TN
    n_bv = BLK_GATE0 - BLK_BV0

    def a_spec(gi, r):
        return pl.BlockSpec((HPB, tm // r, r * HEAD_DIM),
                            lambda i, j: (jnp.clip((j - gi + N_DIL - 1) // N_DIL, 0, 2), i, 0))

    a_shapes = [jax.ShapeDtypeStruct((3 * HPB, t // r, r * HEAD_DIM), BF16) for _, r in DIL_PATTERNS]
    out_shape = a_shapes + [
        jax.ShapeDtypeStruct((2 * N_HEADS_B, t, HEAD_DIM), BF16),
        jax.ShapeDtypeStruct((N_HEADS_B, t // MOBA_BLOCK, MOBA_VT_ROWS, MOBA_BLOCK), BF16),
        jax.ShapeDtypeStruct((t, n_gate * PROJ_TN), BF16)]
    out_specs = [a_spec(gi, r) for gi, (_, r) in enumerate(DIL_PATTERNS)] + [
        pl.BlockSpec((HPB, tm, HEAD_DIM),
                     lambda i, j: (jnp.clip(j - BLK_BQK0, 0, BLK_BV0 - BLK_BQK0 - 1), i, 0)),
        pl.BlockSpec((HPB, tm // MOBA_BLOCK, MOBA_VT_ROWS, MOBA_BLOCK),
                     lambda i, j: (jnp.clip(j - BLK_BV0, 0, n_bv - 1), i, 0, 0)),
        pl.BlockSpec((tm, PROJ_TN), lambda i, j: (i, jnp.clip(j - BLK_GATE0, 0, n_gate - 1)))]
    return pl.pallas_call(
        _proj_kernel,
        grid=(t // tm, n_blk),
        in_specs=[pl.BlockSpec(memory_space=pltpu.SMEM),
                  pl.BlockSpec((tm, d), lambda i, j: (i, 0)),
                  pl.BlockSpec((d, PROJ_TN), lambda i, j: (0, jnp.minimum(j, BLK_GATE0 - 1))),
                  pl.BlockSpec((d, PROJ_TN), lambda i, j: (0, jnp.clip(j - BLK_GATE0, 0, n_gate - 1))),
                  pl.BlockSpec((n_blk, HPB, HEAD_DIM), lambda i, j: (0, 0, 0)),
                  pl.BlockSpec((n_bv, HPB, MOBA_VT_PAD, MOBA_BLOCK), lambda i, j: (0, 0, 0, 0))],
        out_specs=out_specs,
        out_shape=out_shape,
        scratch_shapes=[pltpu.VMEM((HPB, PROJ_ROWS, HEAD_DIM), F32), pltpu.VMEM((PROJ_ROWS, HEAD_DIM), F32)],
        compiler_params=_params("arbitrary", "arbitrary"),
        name="proj",
    )(norm_flags, u, w_in, w_gate, gains.reshape(n_blk, HPB, HEAD_DIM),
      key_weights.reshape(n_bv, HPB, MOBA_VT_PAD, MOBA_BLOCK))


DIL_TOKENS = BAND * max(r for _, r in DIL_PATTERNS)


def _band_blocks(blocks, store):
    scores = [lax.dot_general(q, k2, _NT, preferred_element_type=F32) + bias
              for q, k2, _, bias, _ in blocks]
    soft = []
    for s in scores:
        m = jnp.max(s, axis=-1, keepdims=True)
        p = jnp.exp2(s - m)
        denom = jnp.sum(p, axis=-1, keepdims=True)
        soft.append((p.astype(BF16), denom, jnp.broadcast_to(m + jnp.log2(denom), (BAND, LANES))))
    for (p, denom, lse), (_, _, v2, _, args) in zip(soft, blocks):
        o = jnp.dot(p, v2, preferred_element_type=F32) / denom
        store(*args, o, lse)


def _dilated_kernel(slopes_ref, *refs):
    groups = [refs[5 * g:5 * g + 5] for g in range(N_DIL)]
    y_ref, o_scr, l_scr = refs[5 * N_DIL:]
    j = pl.program_id(0)
    b = pl.program_id(1)

    qi = lax.broadcasted_iota(jnp.int32, (BAND, 2 * BAND), 0)
    ki = lax.broadcasted_iota(jnp.int32, (BAND, 2 * BAND), 1)
    dist = BAND + qi - ki
    in_band = (dist >= 0) & (dist <= BAND)
    dist_f = dist.astype(F32)
    first_ok = (ki + jnp.minimum(b, 1) * BAND) >= BAND

    for g, (_, r) in enumerate(DIL_PATTERNS):
        q_ref, k_ref, v_ref, kp_ref, vp_ref = groups[g]
        slope = slopes_ref[g * HEADS_PER_DIL_GROUP + j]
        bias = jnp.where(in_band, dist_f * (-slope * r), NEG_INF)
        bias_first = jnp.where(first_ok, bias, NEG_INF)
        n_blk = DIL_TOKENS // (r * BAND)

        def store(i, rho, o, lse, g=g, r=r):
            rows = pl.ds(i * (BAND * r) + rho, BAND, stride=r) if r > 1 else pl.ds(i * BAND, BAND)
            o_scr.at[g][rows, :] = o
            l_scr.at[g][rows, :] = lse

        blocks = []
        for rho in range(r):
            cols = slice(rho * HEAD_DIM, (rho + 1) * HEAD_DIM)
            kk = jnp.concatenate([kp_ref[:, cols], k_ref[0:BAND, cols]], axis=0)
            vv = jnp.concatenate([vp_ref[:, cols], v_ref[0:BAND, cols]], axis=0)
            blocks.append((q_ref[0:BAND, cols], kk, vv, bias_first, (0, rho)))
            for i in range(1, n_blk):
                blocks.append((q_ref[i * BAND:(i + 1) * BAND, cols],
                               k_ref[(i - 1) * BAND:(i + 1) * BAND, cols],
                               v_ref[(i - 1) * BAND:(i + 1) * BAND, cols], bias, (i, rho)))
        _band_blocks(blocks, store)

    lses = [l_scr[g] for g in range(N_DIL)]
    mx = functools.reduce(jnp.maximum, lses)
    es = [jnp.exp2(lse - mx) for lse in lses]
    tot = functools.reduce(jnp.add, es)
    y = functools.reduce(jnp.add, [(e / tot) * o_scr[g] for g, e in enumerate(es)])
    y_ref[...] = y.astype(BF16)


def _dilated(a_views, slopes):
    t = a_views[0].shape[1]
    dh = HEAD_DIM
    nb = t // DIL_TOKENS
    hq, hk, hv = 0, HPB, 2 * HPB
    args, specs = [], []
    for g, (_, r) in enumerate(DIL_PATTERNS):
        rows = DIL_TOKENS // r
        prev_per_blk = rows // BAND
        width = r * dh

        def cur(base, rows=rows, width=width):
            return pl.BlockSpec((None, rows, width), lambda j, b: (base + j, b, 0))

        def prev(base, width=width, ppb=prev_per_blk):
            return pl.BlockSpec((None, BAND, width),
                                lambda j, b: (base + j, jnp.maximum(b * ppb - 1, 0), 0))

        args += [a_views[g]] * 5
        specs += [cur(hq), cur(hk), cur(hv), prev(hk), prev(hv)]

    return pl.pallas_call(
        _dilated_kernel,
        grid=(HEADS_PER_DIL_GROUP, nb),
        in_specs=[pl.BlockSpec(memory_space=pltpu.SMEM)] + specs,
        out_specs=pl.BlockSpec((DIL_TOKENS, dh), lambda j, b: (b, j)),
        out_shape=jax.ShapeDtypeStruct((t, HEADS_PER_DIL_GROUP * dh), BF16),
        scratch_shapes=[pltpu.VMEM((N_DIL, DIL_TOKENS, dh), F32),
                        pltpu.VMEM((N_DIL, DIL_TOKENS, LANES), F32)],
        compiler_params=_params("arbitrary", "arbitrary"),
        name="dilated",
    )(slopes, *args)


MOBA_UNROLL = 4


MOBA_HEADS_PER_STEP = 8


def _moba_kernel(slopes_ref, q_ref, k_ref, vt_ref, y_ref, kmean_scr, sel_scr, acc_scr, s_scr, *, n_blocks):
    hg = pl.program_id(0)
    qb = pl.program_id(1)
    blk = MOBA_BLOCK
    n_h, _, dh = q_ref.shape
    heads = range(n_h)
    slope = [slopes_ref[N_HEADS_A + hg * n_h + hh] for hh in heads]

    @pl.when(qb == 0)
    def _():
        def body(n, carry):
            for hh in heads:
                kn = k_ref[hh, pl.ds(pl.multiple_of(n * blk, blk), blk), :].astype(F32)
                kmean_scr[hh, pl.ds(n, 1), :] = jnp.mean(kn, axis=0, keepdims=True)
            return carry
        lax.fori_loop(0, n_blocks, body, 0)

    q = [q_ref[hh] for hh in heads]

    blk_id = lax.broadcasted_iota(jnp.int32, (n_blocks, blk), 0)
    blk_id_f = blk_id.astype(F32)
    past = blk_id < qb
    key_i = lax.broadcasted_iota(jnp.int32, (blk, blk), 0)
    qry_i = lax.broadcasted_iota(jnp.int32, (blk, blk), 1)
    own = pl.multiple_of(qb * blk, blk)

    in_block_max = [slope[hh] * (blk - 1) for hh in heads]

    width = MOBA_UNROLL
    n_chunks = (qb + width - 1) // width

    def chunk_scores(hh, ci, slot):
        start = pl.multiple_of(ci * (width * blk), width * blk)
        s_scr[slot, hh] = lax.dot_general(k_ref[hh, pl.ds(start, width * blk), :], q[hh], _NT,
                                          preferred_element_type=F32)

    gates = [lax.dot_general(kmean_scr[hh], q[hh].astype(F32), _NT,
                             precision=lax.Precision.HIGHEST, preferred_element_type=F32) for hh in heads]
    s_own = [lax.dot_general(k_ref[hh, pl.ds(own, blk), :], q[hh], _NT, preferred_element_type=F32)
             for hh in heads]
    for hh in heads:
        chunk_scores(hh, 0, 0)

    m0, p_own = [], []
    for hh in heads:
        work = jnp.where(past, gates[hh], NEG_INF)
        sel = jnp.zeros((n_blocks, blk), F32)
        for _ in range(MOBA_TOPK):
            mx = jnp.max(work, axis=0, keepdims=True)
            first = jnp.min(jnp.where(work == mx, blk_id_f, float(n_blocks)), axis=0, keepdims=True)
            hit = blk_id_f == first
            sel = jnp.where(hit, 1.0, sel)
            work = jnp.where(hit, -jnp.inf, work)
        sel_scr[hh] = jnp.where(past, sel, 0.0)

        s = jnp.where(key_i <= qry_i, s_own[hh], NEG_INF)
        m0.append(jnp.max(s, axis=0, keepdims=True) + in_block_max[hh])
        p_own.append(jnp.exp2(s - m0[hh]).astype(BF16))
    for hh in heads:
        acc_scr[hh] = jnp.dot(vt_ref[hh, qb], p_own[hh], preferred_element_type=F32)

    def past_chunk(ci, m_run, src, dst):
        n0 = ci * width
        nxt = jnp.minimum(ci + 1, n_chunks - 1)
        out = []
        for hh in heads:
            chunk_scores(hh, nxt, dst)
            c, chosen = [], []
            m_chunk = jnp.full((1, blk), NEG_INF, F32)
            for a in range(width):
                s = s_scr[src, hh, a * blk:(a + 1) * blk, :]
                c.append(slope[hh] * jnp.full((1, blk), (n0 + a - qb) * blk, jnp.int32).astype(F32))
                chosen.append(sel_scr[hh, pl.ds(n0 + a, 1), :] > 0.5)
                m_blk = jnp.max(s, axis=0, keepdims=True) + (c[a] + in_block_max[hh])
                m_chunk = jnp.maximum(m_chunk, jnp.where(chosen[a], m_blk, NEG_INF))
            pv = jnp.zeros(acc_scr.shape[1:], F32)
            for a in range(width):
                s = s_scr[src, hh, a * blk:(a + 1) * blk, :]
                p = jnp.exp2(s - jnp.where(chosen[a], m_chunk - c[a], -NEG_INF))
                pv = pv + jnp.dot(vt_ref[hh, n0 + a], p.astype(BF16), preferred_element_type=F32)
            m_new = jnp.maximum(m_run[hh], m_chunk)
            alpha = jnp.exp2(m_run[hh] - m_new)
            beta = jnp.exp2(m_chunk - m_new)
            acc_scr[hh] = alpha * acc_scr[hh] + beta * pv
            out.append(m_new)
        return tuple(out)

    def body(ci, m_run):
        return lax.cond(lax.rem(ci, 2) == 0,
                        lambda m: past_chunk(ci, m, 0, 1), lambda m: past_chunk(ci, m, 1, 0), m_run)

    lax.fori_loop(0, n_chunks, body, tuple(m0))
    for hh in heads:
        y = acc_scr[hh, 0:dh, :] / acc_scr[hh, dh:dh + 1, :]
        y_ref[:, hh * dh:(hh + 1) * dh] = y.T.astype(BF16)


def _moba(bqk, bvt, slopes):
    _, t, dh = bqk.shape
    n_blocks = t // MOBA_BLOCK
    assert n_blocks % MOBA_UNROLL == 0, "the padded last chunk must stay inside the key array"
    n_h = MOBA_HEADS_PER_STEP
    n_groups = N_HEADS_B // n_h
    return pl.pallas_call(
        functools.partial(_moba_kernel, n_blocks=n_blocks),
        grid=(n_groups, n_blocks),
        in_specs=[pl.BlockSpec(memory_space=pltpu.SMEM),
                  pl.BlockSpec((n_h, MOBA_BLOCK, dh), lambda h, i: (h, i, 0)),
                  pl.BlockSpec((n_h, t, dh), lambda h, i: (n_groups + h, 0, 0), pipeline_mode=pl.Buffered(1)),
                  pl.BlockSpec((n_h, n_blocks, MOBA_VT_ROWS, MOBA_BLOCK), lambda h, i: (h, 0, 0, 0),
                               pipeline_mode=pl.Buffered(1))],
        out_specs=pl.BlockSpec((MOBA_BLOCK, n_h * dh), lambda h, i: (i, h)),
        out_shape=jax.ShapeDtypeStruct((t, N_HEADS_B * dh), BF16),
        scratch_shapes=[pltpu.VMEM((n_h, n_blocks, dh), F32),
                        pltpu.VMEM((n_h, n_blocks, MOBA_BLOCK), F32),
                        pltpu.VMEM((n_h, MOBA_VT_ROWS, MOBA_BLOCK), F32),
                        pltpu.VMEM((2, n_h, MOBA_UNROLL * MOBA_BLOCK, MOBA_BLOCK), F32)],
        compiler_params=_params("arbitrary", "arbitrary"),
        name="moba",
    )(slopes, bqk, bqk, bvt)


def _merge_kernel(x_ref, ada_ref, ya_ref, yb_ref, gates_a_ref, gates_b_ref, wa_ref, wb_ref, wo_ref, o_ref):
    pa = jnp.dot(ya_ref[...], wa_ref[...], preferred_element_type=F32)
    pb = jnp.dot(yb_ref[...], wb_ref[...], preferred_element_type=F32)
    merged = gates_a_ref[...].astype(F32) * pa + gates_b_ref[...].astype(F32) * pb
    out = jnp.dot(merged.astype(BF16), wo_ref[...], preferred_element_type=F32)
    o_ref[...] = x_ref[...] + _ada_rows(ada_ref, 1)[2] * out


def _merge(x, ada, y_a, y_b, gates, w_a, w_b, w_o, *, tm=256):
    t, d = x.shape
    const = lambda i: (0, 0)
    return pl.pallas_call(
        _merge_kernel,
        grid=(t // tm,),
        in_specs=[pl.BlockSpec((tm, d), lambda i: (i, 0)),
                  pl.BlockSpec((N_ADA, d), const),
                  pl.BlockSpec((tm, y_a.shape[1]), lambda i: (i, 0)),
                  pl.BlockSpec((tm, y_b.shape[1]), lambda i: (i, 0)),
                  pl.BlockSpec((tm, d), lambda i: (i, 0)),
                  pl.BlockSpec((tm, d), lambda i: (i, 1)),
                  pl.BlockSpec(w_a.shape, const),
                  pl.BlockSpec(w_b.shape, const),
                  pl.BlockSpec(w_o.shape, const)],
        out_specs=pl.BlockSpec((tm, d), lambda i: (i, 0)),
        out_shape=jax.ShapeDtypeStruct((t, d), F32),
        compiler_params=_params("parallel"),
        name="merge",
    )(x, ada, y_a, y_b, gates, gates, w_a, w_b, w_o)


def _layer(x, c, w_ada, b_ada, g_ffn1, ffn1_w_gate, ffn1_w_up, ffn1_w_down,
           g_mix, w_in, q_norm, k_norm, w_gate, w_branch_a, w_branch_b, w_out,
           g_ffn2, ffn2_w_gate, ffn2_w_up, ffn2_w_down):
    t, d = x.shape
    ada = _ada(c, w_ada, b_ada).reshape(N_ADA, d)
    slopes = jnp.exp2(-8.0 * jnp.arange(1, N_HEADS + 1, dtype=F32) / N_HEADS) * LOG2_E

    x = _ffn(x, ada, g_ffn1, ffn1_w_gate, ffn1_w_up, ffn1_w_down, sub=0)

    scale = HEAD_DIM ** -0.5 * LOG2_E
    n_gate_heads = w_gate.shape[1] // HEAD_DIM
    gains = jnp.concatenate([q_norm[:N_HEADS_A] * scale, k_norm[:N_HEADS_A], jnp.ones((N_HEADS_A, HEAD_DIM), F32),
                             q_norm[N_HEADS_A:] * scale, k_norm[N_HEADS_A:],
                             jnp.ones((N_HEADS_B + n_gate_heads, HEAD_DIM), F32)], axis=0)
    norm_flags = jnp.concatenate([jnp.ones((2 * N_DIL,), F32), jnp.zeros((N_DIL,), F32),
                                  jnp.ones((BLK_BV0 - BLK_BQK0,), F32),
                                  jnp.zeros((BLK_GATE0 - BLK_BV0 + n_gate_heads // HPB,), F32)])
    offsets = jnp.arange(MOBA_BLOCK, dtype=F32)
    key_weights = jnp.exp2(slopes[N_HEADS_A:, None, None] * offsets[None, None, :])
    key_weights = jnp.pad(key_weights, ((0, 0), (0, MOBA_VT_PAD - 1), (0, 0)))
    a0, a1, a2, bqk, bvt, gates = _proj(_norm(x, ada, g_mix, sub=1), w_in, w_gate,
                                        gains, norm_flags, key_weights)

    y_a = _dilated((a0, a1, a2), slopes)
    y_b = _moba(bqk, bvt, slopes)

    x = _merge(x, ada, y_a, y_b, gates, w_branch_a.astype(BF16), w_branch_b.astype(BF16), w_out.astype(BF16))
    x = _ffn(x, ada, g_ffn2, ffn2_w_gate, ffn2_w_up, ffn2_w_down, sub=2)
    return x


def kernel(x, c, w_ada, b_ada, g_ffn1, ffn1_w_gate, ffn1_w_up, ffn1_w_down, g_mix, w_in, q_norm, k_norm,
           w_gate, w_branch_a, w_branch_b, w_out, g_ffn2, ffn2_w_gate, ffn2_w_up, ffn2_w_down):
    batch, depth = x.shape[0], w_ada.shape[0]
    outs = []
    for bi in range(batch):
        xb = x[bi]
        for l in range(depth):
            xb = _layer(xb, c[bi], w_ada[l], b_ada[l], g_ffn1[l], ffn1_w_gate[l], ffn1_w_up[l],
                        ffn1_w_down[l], g_mix[l], w_in[l], q_norm[l], k_norm[l], w_gate[l],
                        w_branch_a[l], w_branch_b[l], w_out[l], g_ffn2[l], ffn2_w_gate[l],
                        ffn2_w_up[l], ffn2_w_down[l])
        outs.append(xb)
    return jnp.stack(outs, axis=0)
```

```python
import functools

import jax
import jax.numpy as jnp
from jax import lax
from jax.experimental import pallas as pl
from jax.experimental.pallas import tpu as pltpu

HEAD_DIM = 128
DIL_PATTERNS = ((128, 1), (512, 4), (2048, 16))
N_DIL = len(DIL_PATTERNS)
HEADS_PER_DIL_GROUP = 4
N_HEADS_A = HEADS_PER_DIL_GROUP * N_DIL
N_HEADS_B = 8
N_HEADS = N_HEADS_A + N_HEADS_B
BAND = 128
MOBA_BLOCK = 256
MOBA_TOPK = 3
N_ADA = 9
EPS = 1e-6
NEG_INF = -1e30
LOG2_E = 1.4426950408889634

LANES = 128
VMEM_LIMIT = 60 * 1024 * 1024

BF16 = jnp.bfloat16
F32 = jnp.float32

_NT = (((1,), (1,)), ((), ()))


def _params(*sem):
    return pltpu.CompilerParams(dimension_semantics=sem, vmem_limit_bytes=VMEM_LIMIT)


def _sigmoid(x):
    return 1.0 / (1.0 + jnp.exp(-x))


def _ada_rows(ada_ref, sub):
    return tuple(ada_ref[3 * sub + k:3 * sub + k + 1, :] for k in range(3))


def _norm_modulate(x, g, shift, scale):
    ms = jnp.mean(x * x, axis=-1, keepdims=True)
    return ((x * lax.rsqrt(ms + EPS)) * (g * (1.0 + scale)) + shift).astype(BF16)


def _ada_kernel(c_ref, w_ref, b_ref, o_ref):
    c = c_ref[...]
    s = c * _sigmoid(c)
    o_ref[...] = jnp.sum(s * w_ref[...], axis=0, keepdims=True) + b_ref[...]


def _ada(c, w_ada, b_ada, *, tn=1536):
    d, n = w_ada.shape
    return pl.pallas_call(
        _ada_kernel,
        grid=(n // tn,),
        in_specs=[pl.BlockSpec((d, 1), lambda j: (0, 0)),
                  pl.BlockSpec((d, tn), lambda j: (0, j)),
                  pl.BlockSpec((1, tn), lambda j: (0, j))],
        out_specs=pl.BlockSpec((1, tn), lambda j: (0, j)),
        out_shape=jax.ShapeDtypeStruct((1, n), F32),
        compiler_params=_params("arbitrary"),
        name="ada",
    )(c.reshape(d, 1), w_ada, b_ada.reshape(1, n))


FFN_FIRST_ROWS = 512


def _ffn_kernel(x_ref, ada_ref, g_ref, wg_ref, wu_ref, wd_ref, o_ref, u_ref, *, sub):
    f = pl.program_id(1)

    def swiglu_tile(u, wg, wu, wd):
        hg = jnp.dot(u, wg, preferred_element_type=F32)
        hu = jnp.dot(u, wu, preferred_element_type=F32)
        h = (hg * _sigmoid(hg)) * hu
        return jnp.dot(h.astype(BF16), wd, preferred_element_type=F32)

    @pl.when(f == 0)
    def _():
        wg, wu, wd = wg_ref[...].astype(BF16), wu_ref[...].astype(BF16), wd_ref[...].astype(BF16)
        shift, scale, _ = _ada_rows(ada_ref, sub)
        for c in range(x_ref.shape[0] // FFN_FIRST_ROWS):
            rows = slice(c * FFN_FIRST_ROWS, (c + 1) * FFN_FIRST_ROWS)
            u = _norm_modulate(x_ref[rows, :], g_ref[...], shift, scale)
            u_ref[rows, :] = u
            o_ref[rows, :] = swiglu_tile(u, wg, wu, wd)

    @pl.when(f > 0)
    def _():
        o_ref[...] += swiglu_tile(u_ref[...], wg_ref[...].astype(BF16), wu_ref[...].astype(BF16),
                                  wd_ref[...].astype(BF16))

    @pl.when(f == pl.num_programs(1) - 1)
    def _():
        gate = _ada_rows(ada_ref, sub)[2]
        o_ref[...] = x_ref[...] + (0.5 * gate) * o_ref[...]


def _ffn(x, ada, g, w_gate, w_up, w_down, *, sub, tm=1024, tf=256):
    t, d = x.shape
    dff = w_gate.shape[1]
    return pl.pallas_call(
        functools.partial(_ffn_kernel, sub=sub),
        grid=(t // tm, dff // tf),
        in_specs=[pl.BlockSpec((tm, d), lambda i, f: (i, 0)),
                  pl.BlockSpec((N_ADA, d), lambda i, f: (0, 0)),
                  pl.BlockSpec((1, d), lambda i, f: (0, 0)),
                  pl.BlockSpec((d, tf), lambda i, f: (0, f)),
                  pl.BlockSpec((d, tf), lambda i, f: (0, f)),
                  pl.BlockSpec((tf, d), lambda i, f: (f, 0))],
        out_specs=pl.BlockSpec((tm, d), lambda i, f: (i, 0)),
        out_shape=jax.ShapeDtypeStruct((t, d), F32),
        scratch_shapes=[pltpu.VMEM((tm, d), BF16)],
        compiler_params=_params("parallel", "arbitrary"),
        name=f"ffn{sub}",
    )(x, ada, g.reshape(1, d), w_gate, w_up, w_down)


NORM_ROWS = 128


def _norm_kernel(x_ref, ada_ref, g_ref, u_ref, *, sub):
    shift, scale, _ = _ada_rows(ada_ref, sub)

    def body(c, carry):
        rows = pl.ds(pl.multiple_of(c * NORM_ROWS, NORM_ROWS), NORM_ROWS)
        u_ref[rows, :] = _norm_modulate(x_ref[rows, :], g_ref[...], shift, scale)
        return carry

    lax.fori_loop(0, x_ref.shape[0] // NORM_ROWS, body, 0)


def _norm(x, ada, g, *, sub, tm=1024):
    t, d = x.shape
    return pl.pallas_call(
        functools.partial(_norm_kernel, sub=sub),
        grid=(t // tm,),
        in_specs=[pl.BlockSpec((tm, d), lambda i: (i, 0)),
                  pl.BlockSpec((N_ADA, d), lambda i: (0, 0)),
                  pl.BlockSpec((1, d), lambda i: (0, 0))],
        out_specs=pl.BlockSpec((tm, d), lambda i: (i, 0)),
        out_shape=jax.ShapeDtypeStruct((t, d), BF16),
        compiler_params=_params("parallel"),
        name="norm",
    )(x, ada, g.reshape(1, d))


HPB = HEADS_PER_DIL_GROUP
PROJ_TN = HPB * HEAD_DIM
BLK_A_END = 3 * N_DIL
BLK_BQK0 = BLK_A_END
BLK_BV0 = BLK_BQK0 + 2 * (N_HEADS_B // HPB)
BLK_GATE0 = BLK_BV0 + N_HEADS_B // HPB


MOBA_VT_PAD = 16
MOBA_VT_ROWS = HEAD_DIM + MOBA_VT_PAD


PROJ_ROWS = MOBA_BLOCK


def _proj_kernel(nflag_ref, u_ref, win_ref, wgate_ref, gain_ref, kw_ref,
                 a0_ref, a1_ref, a2_ref, bqk_ref, bvt_ref, gate_ref, stg_ref, stg2_ref):
    j = pl.program_id(1)
    n_chunks = u_ref.shape[0] // PROJ_ROWS

    def chunks(w_ref):
        w = w_ref[...].astype(BF16)
        for c in range(n_chunks):
            rows = slice(c * PROJ_ROWS, (c + 1) * PROJ_ROWS)
            yield c, jnp.dot(u_ref[rows, :], w, preferred_element_type=F32)

    def head(res, h):
        a = res[:, h * HEAD_DIM:(h + 1) * HEAD_DIM]
        flag = nflag_ref[j]
        ms = jnp.mean(a * a, axis=-1, keepdims=True)
        return (a * (lax.rsqrt(ms + EPS) * flag + (1.0 - flag))) * gain_ref[j, h:h + 1, :]

    def dilated_block(o_ref, r):
        n = PROJ_ROWS // r
        for c, res in chunks(win_ref):
            for h in range(HPB):
                y = head(res, h)
                if r == 1:
                    o_ref[h, c * n:(c + 1) * n, :] = y.astype(BF16)
                    continue
                stg = stg_ref.at[h]
                stg[...] = y
                if r == 16:
                    stg2 = stg2_ref
                    m = PROJ_ROWS // 4
                    for a in range(4):
                        stg2[a * m:(a + 1) * m, :] = stg[pl.ds(a, m, stride=4), :]
                    parts = {a + 4 * b: stg2[pl.ds(a * m + b, n, stride=4), :] for a in range(4) for b in range(4)}
                else:
                    parts = {rho: stg[pl.ds(rho, n, stride=r), :] for rho in range(r)}
                for rho in range(r):
                    o_ref[h, c * n:(c + 1) * n, rho * HEAD_DIM:(rho + 1) * HEAD_DIM] = parts[rho].astype(BF16)

    a_refs = (a0_ref, a1_ref, a2_ref)
    for g, (_, r) in enumerate(DIL_PATTERNS):
        pl.when((j < BLK_A_END) & (lax.rem(j, N_DIL) == g))(functools.partial(dilated_block, a_refs[g], r))

    @pl.when((j >= BLK_BQK0) & (j < BLK_BV0))
    def _():
        for c, res in chunks(win_ref):
            for h in range(HPB):
                bqk_ref[h, c * PROJ_ROWS:(c + 1) * PROJ_ROWS, :] = head(res, h).astype(BF16)

    @pl.when((j >= BLK_BV0) & (j < BLK_GATE0))
    def _():
        for c, res in chunks(win_ref):
            for h in range(HPB):
                kw = kw_ref[j - BLK_BV0, h]
                yt = res[:, h * HEAD_DIM:(h + 1) * HEAD_DIM].T
                bvt_ref[h, c, 0:HEAD_DIM, :] = (yt * kw[0:1, :]).astype(BF16)
                bvt_ref[h, c, HEAD_DIM:, :] = kw.astype(BF16)

    @pl.when(j >= BLK_GATE0)
    def _():
        for c, res in chunks(wgate_ref):
            gate_ref[c * PROJ_ROWS:(c + 1) * PROJ_ROWS, :] = _sigmoid(res).astype(BF16)


def _proj(u, w_in, w_gate, gains, norm_flags, key_weights, *, tm=2048):
    t, d = u.shape
    n_gate = w_gate.shape[1] // PROJ_TN
    n_blk = BLK_GATE0 + n_gate
    assert w_in.shape[1] == BLK_GATE0 * PROJ_TN
    n_bv = BLK_GATE0 - BLK_BV0

    def a_spec(gi, r):
        return pl.BlockSpec((HPB, tm // r, r * HEAD_DIM),
                            lambda i, j: (jnp.clip((j - gi + N_DIL - 1) // N_DIL, 0, 2), i, 0))

    a_shapes = [jax.ShapeDtypeStruct((3 * HPB, t // r, r * HEAD_DIM), BF16) for _, r in DIL_PATTERNS]
    out_shape = a_shapes + [
        jax.ShapeDtypeStruct((2 * N_HEADS_B, t, HEAD_DIM), BF16),
        jax.ShapeDtypeStruct((N_HEADS_B, t // MOBA_BLOCK, MOBA_VT_ROWS, MOBA_BLOCK), BF16),
        jax.ShapeDtypeStruct((t, n_gate * PROJ_TN), BF16)]
    out_specs = [a_spec(gi, r) for gi, (_, r) in enumerate(DIL_PATTERNS)] + [
        pl.BlockSpec((HPB, tm, HEAD_DIM),
                     lambda i, j: (jnp.clip(j - BLK_BQK0, 0, BLK_BV0 - BLK_BQK0 - 1), i, 0)),
        pl.BlockSpec((HPB, tm // MOBA_BLOCK, MOBA_VT_ROWS, MOBA_BLOCK),
                     lambda i, j: (jnp.clip(j - BLK_BV0, 0, n_bv - 1), i, 0, 0)),
        pl.BlockSpec((tm, PROJ_TN), lambda i, j: (i, jnp.clip(j - BLK_GATE0, 0, n_gate - 1)))]
    return pl.pallas_call(
        _proj_kernel,
        grid=(t // tm, n_blk),
        in_specs=[pl.BlockSpec(memory_space=pltpu.SMEM),
                  pl.BlockSpec((tm, d), lambda i, j: (i, 0)),
                  pl.BlockSpec((d, PROJ_TN), lambda i, j: (0, jnp.minimum(j, BLK_GATE0 - 1))),
                  pl.BlockSpec((d, PROJ_TN), lambda i, j: (0, jnp.clip(j - BLK_GATE0, 0, n_gate - 1))),
                  pl.BlockSpec((n_blk, HPB, HEAD_DIM), lambda i, j: (0, 0, 0)),
                  pl.BlockSpec((n_bv, HPB, MOBA_VT_PAD, MOBA_BLOCK), lambda i, j: (0, 0, 0, 0))],
        out_specs=out_specs,
        out_shape=out_shape,
        scratch_shapes=[pltpu.VMEM((HPB, PROJ_ROWS, HEAD_DIM), F32), pltpu.VMEM((PROJ_ROWS, HEAD_DIM), F32)],
        compiler_params=_params("arbitrary", "arbitrary"),
        name="proj",
    )(norm_flags, u, w_in, w_gate, gains.reshape(n_blk, HPB, HEAD_DIM),
      key_weights.reshape(n_bv, HPB, MOBA_VT_PAD, MOBA_BLOCK))


DIL_TOKENS = BAND * max(r for _, r in DIL_PATTERNS)


def _band_blocks(blocks, store):
    scores = [lax.dot_general(q, k2, _NT, preferred_element_type=F32) + bias
              for q, k2, _, bias, _ in blocks]
    soft = []
    for s in scores:
        m = jnp.max(s, axis=-1, keepdims=True)
        p = jnp.exp2(s - m)
        denom = jnp.sum(p, axis=-1, keepdims=True)
        soft.append((p.astype(BF16), denom, jnp.broadcast_to(m + jnp.log2(denom), (BAND, LANES))))
    for (p, denom, lse), (_, _, v2, _, args) in zip(soft, blocks):
        o = jnp.dot(p, v2, preferred_element_type=F32) / denom
        store(*args, o, lse)


def _dilated_kernel(slopes_ref, *refs):
    groups = [refs[5 * g:5 * g + 5] for g in range(N_DIL)]
    y_ref, o_scr, l_scr = refs[5 * N_DIL:]
    j = pl.program_id(0)
    b = pl.program_id(1)

    qi = lax.broadcasted_iota(jnp.int32, (BAND, 2 * BAND), 0)
    ki = lax.broadcasted_iota(jnp.int32, (BAND, 2 * BAND), 1)
    dist = BAND + qi - ki
    in_band = (dist >= 0) & (dist <= BAND)
    dist_f = dist.astype(F32)
    first_ok = (ki + jnp.minimum(b, 1) * BAND) >= BAND

    for g, (_, r) in enumerate(DIL_PATTERNS):
        q_ref, k_ref, v_ref, kp_ref, vp_ref = groups[g]
        slope = slopes_ref[g * HEADS_PER_DIL_GROUP + j]
        bias = jnp.where(in_band, dist_f * (-slope * r), NEG_INF)
        bias_first = jnp.where(first_ok, bias, NEG_INF)
        n_blk = DIL_TOKENS // (r * BAND)

        def store(i, rho, o, lse, g=g, r=r):
            rows = pl.ds(i * (BAND * r) + rho, BAND, stride=r) if r > 1 else pl.ds(i * BAND, BAND)
            o_scr.at[g][rows, :] = o
            l_scr.at[g][rows, :] = lse

        blocks = []
        for rho in range(r):
            cols = slice(rho * HEAD_DIM, (rho + 1) * HEAD_DIM)
            kk = jnp.concatenate([kp_ref[:, cols], k_ref[0:BAND, cols]], axis=0)
            vv = jnp.concatenate([vp_ref[:, cols], v_ref[0:BAND, cols]], axis=0)
            blocks.append((q_ref[0:BAND, cols], kk, vv, bias_first, (0, rho)))
            for i in range(1, n_blk):
                blocks.append((q_ref[i * BAND:(i + 1) * BAND, cols],
                               k_ref[(i - 1) * BAND:(i + 1) * BAND, cols],
                               v_ref[(i - 1) * BAND:(i + 1) * BAND, cols], bias, (i, rho)))
        _band_blocks(blocks, store)

    lses = [l_scr[g] for g in range(N_DIL)]
    mx = functools.reduce(jnp.maximum, lses)
    es = [jnp.exp2(lse - mx) for lse in lses]
    tot = functools.reduce(jnp.add, es)
    y = functools.reduce(jnp.add, [(e / tot) * o_scr[g] for g, e in enumerate(es)])
    y_ref[...] = y.astype(BF16)


def _dilated(a_views, slopes):
    t = a_views[0].shape[1]
    dh = HEAD_DIM
    nb = t // DIL_TOKENS
    hq, hk, hv = 0, HPB, 2 * HPB
    args, specs = [], []
    for g, (_, r) in enumerate(DIL_PATTERNS):
        rows = DIL_TOKENS // r
        prev_per_blk = rows // BAND
        width = r * dh

        def cur(base, rows=rows, width=width):
            return pl.BlockSpec((None, rows, width), lambda j, b: (base + j, b, 0))

        def prev(base, width=width, ppb=prev_per_blk):
            return pl.BlockSpec((None, BAND, width),
                                lambda j, b: (base + j, jnp.maximum(b * ppb - 1, 0), 0))

        args += [a_views[g]] * 5
        specs += [cur(hq), cur(hk), cur(hv), prev(hk), prev(hv)]

    return pl.pallas_call(
        _dilated_kernel,
        grid=(HEADS_PER_DIL_GROUP, nb),
        in_specs=[pl.BlockSpec(memory_space=pltpu.SMEM)] + specs,
        out_specs=pl.BlockSpec((DIL_TOKENS, dh), lambda j, b: (b, j)),
        out_shape=jax.ShapeDtypeStruct((t, HEADS_PER_DIL_GROUP * dh), BF16),
        scratch_shapes=[pltpu.VMEM((N_DIL, DIL_TOKENS, dh), F32),
                        pltpu.VMEM((N_DIL, DIL_TOKENS, LANES), F32)],
        compiler_params=_params("arbitrary", "arbitrary"),
        name="dilated",
    )(slopes, *args)


MOBA_UNROLL = 4


MOBA_HEADS_PER_STEP = 8


def _moba_kernel(slopes_ref, q_ref, k_ref, vt_ref, y_ref, kmean_scr, sel_scr, acc_scr, s_scr, *, n_blocks):
    hg = pl.program_id(0)
    qb = pl.program_id(1)
    blk = MOBA_BLOCK
    n_h, _, dh = q_ref.shape
    heads = range(n_h)
    slope = [slopes_ref[N_HEADS_A + hg * n_h + hh] for hh in heads]

    @pl.when(qb == 0)
    def _():
        def body(n, carry):
            for hh in heads:
                kn = k_ref[hh, pl.ds(pl.multiple_of(n * blk, blk), blk), :].astype(F32)
                kmean_scr[hh, pl.ds(n, 1), :] = jnp.mean(kn, axis=0, keepdims=True)
            return carry
        lax.fori_loop(0, n_blocks, body, 0)

    q = [q_ref[hh] for hh in heads]

    blk_id = lax.broadcasted_iota(jnp.int32, (n_blocks, blk), 0)
    blk_id_f = blk_id.astype(F32)
    past = blk_id < qb
    key_i = lax.broadcasted_iota(jnp.int32, (blk, blk), 0)
    qry_i = lax.broadcasted_iota(jnp.int32, (blk, blk), 1)
    own = pl.multiple_of(qb * blk, blk)

    in_block_max = [slope[hh] * (blk - 1) for hh in heads]

    width = MOBA_UNROLL
    n_chunks = (qb + width - 1) // width

    def chunk_scores(hh, ci, slot):
        start = pl.multiple_of(ci * (width * blk), width * blk)
        s_scr[slot, hh] = lax.dot_general(k_ref[hh, pl.ds(start, width * blk), :], q[hh], _NT,
                                          preferred_element_type=F32)

    gates = [lax.dot_general(kmean_scr[hh], q[hh].astype(F32), _NT,
                             precision=lax.Precision.HIGHEST, preferred_element_type=F32) for hh in heads]
    s_own = [lax.dot_general(k_ref[hh, pl.ds(own, blk), :], q[hh], _NT, preferred_element_type=F32)
             for hh in heads]
    for hh in heads:
        chunk_scores(hh, 0, 0)

    m0, p_own = [], []
    for hh in heads:
        work = jnp.where(past, gates[hh], NEG_INF)
        sel = jnp.zeros((n_blocks, blk), F32)
        for _ in range(MOBA_TOPK):
            mx = jnp.max(work, axis=0, keepdims=True)
            first = jnp.min(jnp.where(work == mx, blk_id_f, float(n_blocks)), axis=0, keepdims=True)
            hit = blk_id_f == first
            sel = jnp.where(hit, 1.0, sel)
            work = jnp.where(hit, -jnp.inf, work)
        sel_scr[hh] = jnp.where(past, sel, 0.0)

        s = jnp.where(key_i <= qry_i, s_own[hh], NEG_INF)
        m0.append(jnp.max(s, axis=0, keepdims=True) + in_block_max[hh])
        p_own.append(jnp.exp2(s - m0[hh]).astype(BF16))
    for hh in heads:
        acc_scr[hh] = jnp.dot(vt_ref[hh, qb], p_own[hh], preferred_element_type=F32)

    def past_chunk(ci, m_run, src, dst):
        n0 = ci * width
        nxt = jnp.minimum(ci + 1, n_chunks - 1)
        out = []
        for hh in heads:
            chunk_scores(hh, nxt, dst)
            c, chosen = [], []
            m_chunk = jnp.full((1, blk), NEG_INF, F32)
            for a in range(width):
                s = s_scr[src, hh, a * blk:(a + 1) * blk, :]
                c.append(slope[hh] * jnp.full((1, blk), (n0 + a - qb) * blk, jnp.int32).astype(F32))
                chosen.append(sel_scr[hh, pl.ds(n0 + a, 1), :] > 0.5)
                m_blk = jnp.max(s, axis=0, keepdims=True) + (c[a] + in_block_max[hh])
                m_chunk = jnp.maximum(m_chunk, jnp.where(chosen[a], m_blk, NEG_INF))
            pv = jnp.zeros(acc_scr.shape[1:], F32)
            for a in range(width):
                s = s_scr[src, hh, a * blk:(a + 1) * blk, :]
                p = jnp.exp2(s - jnp.where(chosen[a], m_chunk - c[a], -NEG_INF))
                pv = pv + jnp.dot(vt_ref[hh, n0 + a], p.astype(BF16), preferred_element_type=F32)
            m_new = jnp.maximum(m_run[hh], m_chunk)
            alpha = jnp.exp2(m_run[hh] - m_new)
            beta = jnp.exp2(m_chunk - m_new)
            acc_scr[hh] = alpha * acc_scr[hh] + beta * pv
            out.append(m_new)
        return tuple(out)

    def body(ci, m_run):
        return lax.cond(lax.rem(ci, 2) == 0,
                        lambda m: past_chunk(ci, m, 0, 1), lambda m: past_chunk(ci, m, 1, 0), m_run)

    lax.fori_loop(0, n_chunks, body, tuple(m0))
    for hh in heads:
        y = acc_scr[hh, 0:dh, :] / acc_scr[hh, dh:dh + 1, :]
        y_ref[:, hh * dh:(hh + 1) * dh] = y.T.astype(BF16)


def _moba(bqk, bvt, slopes):
    _, t, dh = bqk.shape
    n_blocks = t // MOBA_BLOCK
    assert n_blocks % MOBA_UNROLL == 0, "the padded last chunk must stay inside the key array"
    n_h = MOBA_HEADS_PER_STEP
    n_groups = N_HEADS_B // n_h
    return pl.pallas_call(
        functools.partial(_moba_kernel, n_blocks=n_blocks),
        grid=(n_groups, n_blocks),
        in_specs=[pl.BlockSpec(memory_space=pltpu.SMEM),
                  pl.BlockSpec((n_h, MOBA_BLOCK, dh), lambda h, i: (h, i, 0)),
                  pl.BlockSpec((n_h, t, dh), lambda h, i: (n_groups + h, 0, 0), pipeline_mode=pl.Buffered(1)),
                  pl.BlockSpec((n_h, n_blocks, MOBA_VT_ROWS, MOBA_BLOCK), lambda h, i: (h, 0, 0, 0),
                               pipeline_mode=pl.Buffered(1))],
        out_specs=pl.BlockSpec((MOBA_BLOCK, n_h * dh), lambda h, i: (i, h)),
        out_shape=jax.ShapeDtypeStruct((t, N_HEADS_B * dh), BF16),
        scratch_shapes=[pltpu.VMEM((n_h, n_blocks, dh), F32),
                        pltpu.VMEM((n_h, n_blocks, MOBA_BLOCK), F32),
                        pltpu.VMEM((n_h, MOBA_VT_ROWS, MOBA_BLOCK), F32),
                        pltpu.VMEM((2, n_h, MOBA_UNROLL * MOBA_BLOCK, MOBA_BLOCK), F32)],
        compiler_params=_params("arbitrary", "arbitrary"),
        name="moba",
    )(slopes, bqk, bqk, bvt)


def _merge_kernel(x_ref, ada_ref, ya_ref, yb_ref, gates_a_ref, gates_b_ref, wa_ref, wb_ref, wo_ref, o_ref):
    pa = jnp.dot(ya_ref[...], wa_ref[...], preferred_element_type=F32)
    pb = jnp.dot(yb_ref[...], wb_ref[...], preferred_element_type=F32)
    merged = gates_a_ref[...].astype(F32) * pa + gates_b_ref[...].astype(F32) * pb
    out = jnp.dot(merged.astype(BF16), wo_ref[...], preferred_element_type=F32)
    o_ref[...] = x_ref[...] + _ada_rows(ada_ref, 1)[2] * out


def _merge(x, ada, y_a, y_b, gates, w_a, w_b, w_o, *, tm=512):
    t, d = x.shape
    const = lambda i: (0, 0)
    resident = dict(index_map=const, pipeline_mode=pl.Buffered(1))
    return pl.pallas_call(
        _merge_kernel,
        grid=(t // tm,),
        in_specs=[pl.BlockSpec((tm, d), lambda i: (i, 0)),
                  pl.BlockSpec((N_ADA, d), const),
                  pl.BlockSpec((tm, y_a.shape[1]), lambda i: (i, 0)),
                  pl.BlockSpec((tm, y_b.shape[1]), lambda i: (i, 0)),
                  pl.BlockSpec((tm, d), lambda i: (i, 0)),
                  pl.BlockSpec((tm, d), lambda i: (i, 1)),
                  pl.BlockSpec(w_a.shape, **resident),
                  pl.BlockSpec(w_b.shape, **resident),
                  pl.BlockSpec(w_o.shape, **resident)],
        out_specs=pl.BlockSpec((tm, d), lambda i: (i, 0)),
        out_shape=jax.ShapeDtypeStruct((t, d), F32),
        compiler_params=_params("parallel"),
        name="merge",
    )(x, ada, y_a, y_b, gates, gates, w_a, w_b, w_o)


def _layer(x, c, w_ada, b_ada, g_ffn1, ffn1_w_gate, ffn1_w_up, ffn1_w_down,
           g_mix, w_in, q_norm, k_norm, w_gate, w_branch_a, w_branch_b, w_out,
           g_ffn2, ffn2_w_gate, ffn2_w_up, ffn2_w_down):
    t, d = x.shape
    ada = _ada(c, w_ada, b_ada).reshape(N_ADA, d)
    slopes = jnp.exp2(-8.0 * jnp.arange(1, N_HEADS + 1, dtype=F32) / N_HEADS) * LOG2_E

    x = _ffn(x, ada, g_ffn1, ffn1_w_gate, ffn1_w_up, ffn1_w_down, sub=0)

    scale = HEAD_DIM ** -0.5 * LOG2_E
    n_gate_heads = w_gate.shape[1] // HEAD_DIM
    gains = jnp.concatenate([q_norm[:N_HEADS_A] * scale, k_norm[:N_HEADS_A], jnp.ones((N_HEADS_A, HEAD_DIM), F32),
                             q_norm[N_HEADS_A:] * scale, k_norm[N_HEADS_A:],
                             jnp.ones((N_HEADS_B + n_gate_heads, HEAD_DIM), F32)], axis=0)
    norm_flags = jnp.concatenate([jnp.ones((2 * N_DIL,), F32), jnp.zeros((N_DIL,), F32),
                                  jnp.ones((BLK_BV0 - BLK_BQK0,), F32),
                                  jnp.zeros((BLK_GATE0 - BLK_BV0 + n_gate_heads // HPB,), F32)])
    offsets = jnp.arange(MOBA_BLOCK, dtype=F32)
    key_weights = jnp.exp2(slopes[N_HEADS_A:, None, None] * offsets[None, None, :])
    key_weights = jnp.pad(key_weights, ((0, 0), (0, MOBA_VT_PAD - 1), (0, 0)))
    a0, a1, a2, bqk, bvt, gates = _proj(_norm(x, ada, g_mix, sub=1), w_in, w_gate,
                                        gains, norm_flags, key_weights)

    y_a = _dilated((a0, a1, a2), slopes)
    y_b = _moba(bqk, bvt, slopes)

    x = _merge(x, ada, y_a, y_b, gates, w_branch_a.astype(BF16), w_branch_b.astype(BF16), w_out.astype(BF16))
    x = _ffn(x, ada, g_ffn2, ffn2_w_gate, ffn2_w_up, ffn2_w_down, sub=2)
    return x


def kernel(x, c, w_ada, b_ada, g_ffn1, ffn1_w_gate, ffn1_w_up, ffn1_w_down, g_mix, w_in, q_norm, k_norm,
           w_gate, w_branch_a, w_branch_b, w_out, g_ffn2, ffn2_w_gate, ffn2_w_up, ffn2_w_down):
    batch, depth = x.shape[0], w_ada.shape[0]
    outs = []
    for bi in range(batch):
        xb = x[bi]
        for l in range(depth):
            xb = _layer(xb, c[bi], w_ada[l], b_ada[l], g_ffn1[l], ffn1_w_gate[l], ffn1_w_up[l],
                        ffn1_w_down[l], g_mix[l], w_in[l], q_norm[l], k_norm[l], w_gate[l],
                        w_branch_a[l], w_branch_b[l], w_out[l], g_ffn2[l], ffn2_w_gate[l],
                        ffn2_w_up[l], ffn2_w_down[l])
        outs.append(xb)
    return jnp.stack(outs, axis=0)
```

```python
import functools

import jax
import jax.numpy as jnp
from jax import lax
from jax.experimental import pallas as pl
from jax.experimental.pallas import tpu as pltpu

HEAD_DIM = 128
DIL_PATTERNS = ((128, 1), (512, 4), (2048, 16))
N_DIL = len(DIL_PATTERNS)
HEADS_PER_DIL_GROUP = 4
N_HEADS_A = HEADS_PER_DIL_GROUP * N_DIL
N_HEADS_B = 8
N_HEADS = N_HEADS_A + N_HEADS_B
BAND = 128
MOBA_BLOCK = 256
MOBA_TOPK = 3
N_ADA = 9
EPS = 1e-6
NEG_INF = -1e30
LOG2_E = 1.4426950408889634

LANES = 128
VMEM_LIMIT = 60 * 1024 * 1024

BF16 = jnp.bfloat16
F32 = jnp.float32

_NT = (((1,), (1,)), ((), ()))


def _params(*sem):
    return pltpu.CompilerParams(dimension_semantics=sem, vmem_limit_bytes=VMEM_LIMIT)


def _sigmoid(x):
    return 1.0 / (1.0 + jnp.exp(-x))


def _ada_rows(ada_ref, sub):
    return tuple(ada_ref[3 * sub + k:3 * sub + k + 1, :] for k in range(3))


def _norm_modulate(x, g, shift, scale):
    ms = jnp.mean(x * x, axis=-1, keepdims=True)
    return ((x * lax.rsqrt(ms + EPS)) * (g * (1.0 + scale)) + shift).astype(BF16)


def _ada_kernel(c_ref, w_ref, b_ref, o_ref):
    c = c_ref[...]
    s = c * _sigmoid(c)
    o_ref[...] = jnp.sum(s * w_ref[...], axis=0, keepdims=True) + b_ref[...]


def _ada(c, w_ada, b_ada, *, tn=1536):
    d, n = w_ada.shape
    return pl.pallas_call(
        _ada_kernel,
        grid=(n // tn,),
        in_specs=[pl.BlockSpec((d, 1), lambda j: (0, 0)),
                  pl.BlockSpec((d, tn), lambda j: (0, j)),
                  pl.BlockSpec((1, tn), lambda j: (0, j))],
        out_specs=pl.BlockSpec((1, tn), lambda j: (0, j)),
        out_shape=jax.ShapeDtypeStruct((1, n), F32),
        compiler_params=_params("arbitrary"),
        name="ada",
    )(c.reshape(d, 1), w_ada, b_ada.reshape(1, n))


FFN_FIRST_ROWS = 512


def _ffn_kernel(x_ref, ada_ref, g_ref, wg_ref, wu_ref, wd_ref, o_ref, u_ref, *, sub):
    f = pl.program_id(1)

    def swiglu_tile(u, wg, wu, wd):
        hg = jnp.dot(u, wg, preferred_element_type=F32)
        hu = jnp.dot(u, wu, preferred_element_type=F32)
        h = (hg * _sigmoid(hg)) * hu
        return jnp.dot(h.astype(BF16), wd, preferred_element_type=F32)

    @pl.when(f == 0)
    def _():
        wg, wu, wd = wg_ref[...].astype(BF16), wu_ref[...].astype(BF16), wd_ref[...].astype(BF16)
        shift, scale, _ = _ada_rows(ada_ref, sub)
        for c in range(x_ref.shape[0] // FFN_FIRST_ROWS):
            rows = slice(c * FFN_FIRST_ROWS, (c + 1) * FFN_FIRST_ROWS)
            u = _norm_modulate(x_ref[rows, :], g_ref[...], shift, scale)
            u_ref[rows, :] = u
            o_ref[rows, :] = swiglu_tile(u, wg, wu, wd)

    @pl.when(f > 0)
    def _():
        o_ref[...] += swiglu_tile(u_ref[...], wg_ref[...].astype(BF16), wu_ref[...].astype(BF16),
                                  wd_ref[...].astype(BF16))

    @pl.when(f == pl.num_programs(1) - 1)
    def _():
        gate = _ada_rows(ada_ref, sub)[2]
        o_ref[...] = x_ref[...] + (0.5 * gate) * o_ref[...]


def _ffn(x, ada, g, w_gate, w_up, w_down, *, sub, tm=1024, tf=256):
    t, d = x.shape
    dff = w_gate.shape[1]
    return pl.pallas_call(
        functools.partial(_ffn_kernel, sub=sub),
        grid=(t // tm, dff // tf),
        in_specs=[pl.BlockSpec((tm, d), lambda i, f: (i, 0)),
                  pl.BlockSpec((N_ADA, d), lambda i, f: (0, 0)),
                  pl.BlockSpec((1, d), lambda i, f: (0, 0)),
                  pl.BlockSpec((d, tf), lambda i, f: (0, f)),
                  pl.BlockSpec((d, tf), lambda i, f: (0, f)),
                  pl.BlockSpec((tf, d), lambda i, f: (f, 0))],
        out_specs=pl.BlockSpec((tm, d), lambda i, f: (i, 0)),
        out_shape=jax.ShapeDtypeStruct((t, d), F32),
        scratch_shapes=[pltpu.VMEM((tm, d), BF16)],
        compiler_params=_params("parallel", "arbitrary"),
        name=f"ffn{sub}",
    )(x, ada, g.reshape(1, d), w_gate, w_up, w_down)


NORM_ROWS = 128


def _norm_kernel(x_ref, ada_ref, g_ref, u_ref, *, sub):
    shift, scale, _ = _ada_rows(ada_ref, sub)

    def body(c, carry):
        rows = pl.ds(pl.multiple_of(c * NORM_ROWS, NORM_ROWS), NORM_ROWS)
        u_ref[rows, :] = _norm_modulate(x_ref[rows, :], g_ref[...], shift, scale)
        return carry

    lax.fori_loop(0, x_ref.shape[0] // NORM_ROWS, body, 0)


def _norm(x, ada, g, *, sub, tm=1024):
    t, d = x.shape
    return pl.pallas_call(
        functools.partial(_norm_kernel, sub=sub),
        grid=(t // tm,),
        in_specs=[pl.BlockSpec((tm, d), lambda i: (i, 0)),
                  pl.BlockSpec((N_ADA, d), lambda i: (0, 0)),
                  pl.BlockSpec((1, d), lambda i: (0, 0))],
        out_specs=pl.BlockSpec((tm, d), lambda i: (i, 0)),
        out_shape=jax.ShapeDtypeStruct((t, d), BF16),
        compiler_params=_params("parallel"),
        name="norm",
    )(x, ada, g.reshape(1, d))


HPB = HEADS_PER_DIL_GROUP
PROJ_TN = HPB * HEAD_DIM
BLK_A_END = 3 * N_DIL
BLK_BQK0 = BLK_A_END
BLK_BV0 = BLK_BQK0 + 2 * (N_HEADS_B // HPB)
BLK_GATE0 = BLK_BV0 + N_HEADS_B // HPB


MOBA_VT_PAD = 16
MOBA_VT_ROWS = HEAD_DIM + MOBA_VT_PAD


PROJ_ROWS = MOBA_BLOCK


def _proj_kernel(nflag_ref, u_ref, win_ref, wgate_ref, gain_ref, kw_ref,
                 a0_ref, a1_ref, a2_ref, bqk_ref, bvt_ref, gate_ref, stg_ref, stg2_ref):
    j = pl.program_id(1)
    n_chunks = u_ref.shape[0] // PROJ_ROWS

    def chunks(w_ref):
        w = w_ref[...].astype(BF16)
        for c in range(n_chunks):
            rows = slice(c * PROJ_ROWS, (c + 1) * PROJ_ROWS)
            yield c, jnp.dot(u_ref[rows, :], w, preferred_element_type=F32)

    def head(res, h):
        a = res[:, h * HEAD_DIM:(h + 1) * HEAD_DIM]
        flag = nflag_ref[j]
        ms = jnp.mean(a * a, axis=-1, keepdims=True)
        return (a * (lax.rsqrt(ms + EPS) * flag + (1.0 - flag))) * gain_ref[j, h:h + 1, :]

    def dilated_block(o_ref, r):
        n = PROJ_ROWS // r
        for c, res in chunks(win_ref):
            for h in range(HPB):
                y = head(res, h)
                if r == 1:
                    o_ref[h, c * n:(c + 1) * n, :] = y.astype(BF16)
                    continue
                stg = stg_ref.at[h]
                stg[...] = y
                if r == 16:
                    stg2 = stg2_ref
                    m = PROJ_ROWS // 4
                    for a in range(4):
                        stg2[a * m:(a + 1) * m, :] = stg[pl.ds(a, m, stride=4), :]
                    parts = {a + 4 * b: stg2[pl.ds(a * m + b, n, stride=4), :] for a in range(4) for b in range(4)}
                else:
                    parts = {rho: stg[pl.ds(rho, n, stride=r), :] for rho in range(r)}
                for rho in range(r):
                    o_ref[h, c * n:(c + 1) * n, rho * HEAD_DIM:(rho + 1) * HEAD_DIM] = parts[rho].astype(BF16)

    a_refs = (a0_ref, a1_ref, a2_ref)
    for g, (_, r) in enumerate(DIL_PATTERNS):
        pl.when((j < BLK_A_END) & (lax.rem(j, N_DIL) == g))(functools.partial(dilated_block, a_refs[g], r))

    @pl.when((j >= BLK_BQK0) & (j < BLK_BV0))
    def _():
        for c, res in chunks(win_ref):
            for h in range(HPB):
                bqk_ref[h, c * PROJ_ROWS:(c + 1) * PROJ_ROWS, :] = head(res, h).astype(BF16)

    @pl.when((j >= BLK_BV0) & (j < BLK_GATE0))
    def _():
        for c, res in chunks(win_ref):
            for h in range(HPB):
                kw = kw_ref[j - BLK_BV0, h]
                yt = res[:, h * HEAD_DIM:(h + 1) * HEAD_DIM].T
                bvt_ref[h, c, 0:HEAD_DIM, :] = (yt * kw[0:1, :]).astype(BF16)
                bvt_ref[h, c, HEAD_DIM:, :] = kw.astype(BF16)

    @pl.when(j >= BLK_GATE0)
    def _():
        for c, res in chunks(wgate_ref):
            gate_ref[c * PROJ_ROWS:(c + 1) * PROJ_ROWS, :] = _sigmoid(res).astype(BF16)


def _proj(u, w_in, w_gate, gains, norm_flags, key_weights, *, tm=2048):
    t, d = u.shape
    n_gate = w_gate.shape[1] // PROJ_TN
    n_blk = BLK_GATE0 + n_gate
    assert w_in.shape[1] == BLK_GATE0 * PROJ_TN
    n_bv = BLK_GATE0 - BLK_BV0

    def a_spec(gi, r):
        return pl.BlockSpec((HPB, tm // r, r * HEAD_DIM),
                            lambda i, j: (jnp.clip((j - gi + N_DIL - 1) // N_DIL, 0, 2), i, 0))

    a_shapes = [jax.ShapeDtypeStruct((3 * HPB, t // r, r * HEAD_DIM), BF16) for _, r in DIL_PATTERNS]
    out_shape = a_shapes + [
        jax.ShapeDtypeStruct((2 * N_HEADS_B, t, HEAD_DIM), BF16),
        jax.ShapeDtypeStruct((N_HEADS_B, t // MOBA_BLOCK, MOBA_VT_ROWS, MOBA_BLOCK), BF16),
        jax.ShapeDtypeStruct((t, n_gate * PROJ_TN), BF16)]
    out_specs = [a_spec(gi, r) for gi, (_, r) in enumerate(DIL_PATTERNS)] + [
        pl.BlockSpec((HPB, tm, HEAD_DIM),
                     lambda i, j: (jnp.clip(j - BLK_BQK0, 0, BLK_BV0 - BLK_BQK0 - 1), i, 0)),
        pl.BlockSpec((HPB, tm // MOBA_BLOCK, MOBA_VT_ROWS, MOBA_BLOCK),
                     lambda i, j: (jnp.clip(j - BLK_BV0, 0, n_bv - 1), i, 0, 0)),
        pl.BlockSpec((tm, PROJ_TN), lambda i, j: (i, jnp.clip(j - BLK_GATE0, 0, n_gate - 1)))]
    return pl.pallas_call(
        _proj_kernel,
        grid=(t // tm, n_blk),
        in_specs=[pl.BlockSpec(memory_space=pltpu.SMEM),
                  pl.BlockSpec((tm, d), lambda i, j: (i, 0)),
                  pl.BlockSpec((d, PROJ_TN), lambda i, j: (0, jnp.minimum(j, BLK_GATE0 - 1))),
                  pl.BlockSpec((d, PROJ_TN), lambda i, j: (0, jnp.clip(j - BLK_GATE0, 0, n_gate - 1))),
                  pl.BlockSpec((n_blk, HPB, HEAD_DIM), lambda i, j: (0, 0, 0)),
                  pl.BlockSpec((n_bv, HPB, MOBA_VT_PAD, MOBA_BLOCK), lambda i, j: (0, 0, 0, 0))],
        out_specs=out_specs,
        out_shape=out_shape,
        scratch_shapes=[pltpu.VMEM((HPB, PROJ_ROWS, HEAD_DIM), F32), pltpu.VMEM((PROJ_ROWS, HEAD_DIM), F32)],
        compiler_params=_params("arbitrary", "arbitrary"),
        name="proj",
    )(norm_flags, u, w_in, w_gate, gains.reshape(n_blk, HPB, HEAD_DIM),
      key_weights.reshape(n_bv, HPB, MOBA_VT_PAD, MOBA_BLOCK))


DIL_TOKENS = BAND * max(r for _, r in DIL_PATTERNS)


def _band_blocks(blocks, store):
    scores = [lax.dot_general(q, k2, _NT, preferred_element_type=F32) + bias
              for q, k2, _, bias, _ in blocks]
    soft = []
    for s in scores:
        m = jnp.max(s, axis=-1, keepdims=True)
        p = jnp.exp2(s - m)
        denom = jnp.sum(p, axis=-1, keepdims=True)
        soft.append((p.astype(BF16), denom, jnp.broadcast_to(m + jnp.log2(denom), (BAND, LANES))))
    for (p, denom, lse), (_, _, v2, _, args) in zip(soft, blocks):
        o = jnp.dot(p, v2, preferred_element_type=F32) / denom
        store(*args, o, lse)


def _dilated_kernel(slopes_ref, *refs):
    groups = [refs[5 * g:5 * g + 5] for g in range(N_DIL)]
    y_ref, o_scr, l_scr = refs[5 * N_DIL:]
    j = pl.program_id(0)
    b = pl.program_id(1)

    qi = lax.broadcasted_iota(jnp.int32, (BAND, 2 * BAND), 0)
    ki = lax.broadcasted_iota(jnp.int32, (BAND, 2 * BAND), 1)
    dist = BAND + qi - ki
    in_band = (dist >= 0) & (dist <= BAND)
    dist_f = dist.astype(F32)
    first_ok = (ki + jnp.minimum(b, 1) * BAND) >= BAND

    for g, (_, r) in enumerate(DIL_PATTERNS):
        q_ref, k_ref, v_ref, kp_ref, vp_ref = groups[g]
        slope = slopes_ref[g * HEADS_PER_DIL_GROUP + j]
        bias = jnp.where(in_band, dist_f * (-slope * r), NEG_INF)
        bias_first = jnp.where(first_ok, bias, NEG_INF)
        n_blk = DIL_TOKENS // (r * BAND)

        def store(i, rho, o, lse, g=g, r=r):
            rows = pl.ds(i * (BAND * r) + rho, BAND, stride=r) if r > 1 else pl.ds(i * BAND, BAND)
            o_scr.at[g][rows, :] = o
            l_scr.at[g][rows, :] = lse

        blocks = []
        for rho in range(r):
            cols = slice(rho * HEAD_DIM, (rho + 1) * HEAD_DIM)
            kk = jnp.concatenate([kp_ref[:, cols], k_ref[0:BAND, cols]], axis=0)
            vv = jnp.concatenate([vp_ref[:, cols], v_ref[0:BAND, cols]], axis=0)
            blocks.append((q_ref[0:BAND, cols], kk, vv, bias_first, (0, rho)))
            for i in range(1, n_blk):
                blocks.append((q_ref[i * BAND:(i + 1) * BAND, cols],
                               k_ref[(i - 1) * BAND:(i + 1) * BAND, cols],
                               v_ref[(i - 1) * BAND:(i + 1) * BAND, cols], bias, (i, rho)))
        _band_blocks(blocks, store)

    lses = [l_scr[g] for g in range(N_DIL)]
    mx = functools.reduce(jnp.maximum, lses)
    es = [jnp.exp2(lse - mx) for lse in lses]
    tot = functools.reduce(jnp.add, es)
    y = functools.reduce(jnp.add, [(e / tot) * o_scr[g] for g, e in enumerate(es)])
    y_ref[...] = y.astype(BF16)


def _dilated(a_views, slopes):
    t = a_views[0].shape[1]
    dh = HEAD_DIM
    nb = t // DIL_TOKENS
    hq, hk, hv = 0, HPB, 2 * HPB
    args, specs = [], []
    for g, (_, r) in enumerate(DIL_PATTERNS):
        rows = DIL_TOKENS // r
        prev_per_blk = rows // BAND
        width = r * dh

        def cur(base, rows=rows, width=width):
            return pl.BlockSpec((None, rows, width), lambda j, b: (base + j, b, 0))

        def prev(base, width=width, ppb=prev_per_blk):
            return pl.BlockSpec((None, BAND, width),
                                lambda j, b: (base + j, jnp.maximum(b * ppb - 1, 0), 0))

        args += [a_views[g]] * 5
        specs += [cur(hq), cur(hk), cur(hv), prev(hk), prev(hv)]

    return pl.pallas_call(
        _dilated_kernel,
        grid=(HEADS_PER_DIL_GROUP, nb),
        in_specs=[pl.BlockSpec(memory_space=pltpu.SMEM)] + specs,
        out_specs=pl.BlockSpec((DIL_TOKENS, dh), lambda j, b: (b, j)),
        out_shape=jax.ShapeDtypeStruct((t, HEADS_PER_DIL_GROUP * dh), BF16),
        scratch_shapes=[pltpu.VMEM((N_DIL, DIL_TOKENS, dh), F32),
                        pltpu.VMEM((N_DIL, DIL_TOKENS, LANES), F32)],
        compiler_params=_params("arbitrary", "arbitrary"),
        name="dilated",
    )(slopes, *args)


MOBA_UNROLL = 4


MOBA_HEADS_PER_STEP = 8


def _moba_kernel(slopes_ref, q_ref, qn_ref, k_ref, vt_ref, y_ref,
                 kmean_scr, sel_scr, acc_scr, s_scr, qq_scr, slot_scr, *, n_blocks):
    hg = pl.program_id(0)
    qb = pl.program_id(1)
    blk = MOBA_BLOCK
    n_h, _, dh = q_ref.shape
    heads = range(n_h)
    slope = [slopes_ref[N_HEADS_A + hg * n_h + hh] for hh in heads]
    width = MOBA_UNROLL
    n_chunks = (qb + width - 1) // width

    qq_scr[0] = q_ref[...]
    qq_scr[1] = qn_ref[...]

    def chunk_scores(hh, ci, which_q, slot):
        start = pl.multiple_of(ci * (width * blk), width * blk)
        s_scr[slot, hh] = lax.dot_general(k_ref[hh, pl.ds(start, width * blk), :], qq_scr[which_q, hh], _NT,
                                          preferred_element_type=F32)

    @pl.when(qb == 0)
    def _():
        def body(n, carry):
            for hh in heads:
                kn = k_ref[hh, pl.ds(pl.multiple_of(n * blk, blk), blk), :].astype(F32)
                kmean_scr[hh, pl.ds(n, 1), :] = jnp.mean(kn, axis=0, keepdims=True)
            return carry
        lax.fori_loop(0, n_blocks, body, 0)
        slot_scr[0] = 0
        for hh in heads:
            chunk_scores(hh, 0, 1, 0)

    slot0 = slot_scr[0]
    q = [q_ref[hh] for hh in heads]

    blk_id = lax.broadcasted_iota(jnp.int32, (n_blocks, blk), 0)
    blk_id_f = blk_id.astype(F32)
    past = blk_id < qb
    key_i = lax.broadcasted_iota(jnp.int32, (blk, blk), 0)
    qry_i = lax.broadcasted_iota(jnp.int32, (blk, blk), 1)
    own = pl.multiple_of(qb * blk, blk)

    in_block_max = [slope[hh] * (blk - 1) for hh in heads]

    gates = [lax.dot_general(kmean_scr[hh], q[hh].astype(F32), _NT,
                             precision=lax.Precision.HIGHEST, preferred_element_type=F32) for hh in heads]
    s_own = [lax.dot_general(k_ref[hh, pl.ds(own, blk), :], q[hh], _NT, preferred_element_type=F32)
             for hh in heads]

    m0, p_own = [], []
    for hh in heads:
        work = jnp.where(past, gates[hh], NEG_INF)
        sel = jnp.zeros((n_blocks, blk), F32)
        for _ in range(MOBA_TOPK):
            mx = jnp.max(work, axis=0, keepdims=True)
            first = jnp.min(jnp.where(work == mx, blk_id_f, float(n_blocks)), axis=0, keepdims=True)
            hit = blk_id_f == first
            sel = jnp.where(hit, 1.0, sel)
            work = jnp.where(hit, -jnp.inf, work)
        sel_scr[hh] = jnp.where(past, sel, 0.0)

        s = jnp.where(key_i <= qry_i, s_own[hh], NEG_INF)
        m0.append(jnp.max(s, axis=0, keepdims=True) + in_block_max[hh])
        p_own.append(jnp.exp2(s - m0[hh]).astype(BF16))
    for hh in heads:
        acc_scr[hh] = jnp.dot(vt_ref[hh, qb], p_own[hh], preferred_element_type=F32)

    def past_chunk(ci, m_run, src, dst):
        n0 = ci * width
        last = ci == n_chunks - 1
        nxt = jnp.where(last, 0, ci + 1)
        which_q = jnp.where(last, 1, 0)
        out = []
        for hh in heads:
            chunk_scores(hh, nxt, which_q, dst)
            c, chosen = [], []
            m_chunk = jnp.full((1, blk), NEG_INF, F32)
            for a in range(width):
                s = s_scr[src, hh, a * blk:(a + 1) * blk, :]
                c.append(slope[hh] * jnp.full((1, blk), (n0 + a - qb) * blk, jnp.int32).astype(F32))
                chosen.append(sel_scr[hh, pl.ds(n0 + a, 1), :] > 0.5)
                m_blk = jnp.max(s, axis=0, keepdims=True) + (c[a] + in_block_max[hh])
                m_chunk = jnp.maximum(m_chunk, jnp.where(chosen[a], m_blk, NEG_INF))
            pv = jnp.zeros(acc_scr.shape[1:], F32)
            for a in range(width):
                s = s_scr[src, hh, a * blk:(a + 1) * blk, :]
                p = jnp.exp2(s - jnp.where(chosen[a], m_chunk - c[a], -NEG_INF))
                pv = pv + jnp.dot(vt_ref[hh, n0 + a], p.astype(BF16), preferred_element_type=F32)
            m_new = jnp.maximum(m_run[hh], m_chunk)
            alpha = jnp.exp2(m_run[hh] - m_new)
            beta = jnp.exp2(m_chunk - m_new)
            acc_scr[hh] = alpha * acc_scr[hh] + beta * pv
            out.append(m_new)
        return tuple(out)

    def body(ci, m_run):
        return lax.cond(lax.rem(slot0 + ci, 2) == 0,
                        lambda m: past_chunk(ci, m, 0, 1), lambda m: past_chunk(ci, m, 1, 0), m_run)

    lax.fori_loop(0, n_chunks, body, tuple(m0))
    slot_scr[0] = lax.rem(slot0 + n_chunks, 2)
    for hh in heads:
        y = acc_scr[hh, 0:dh, :] / acc_scr[hh, dh:dh + 1, :]
        y_ref[:, hh * dh:(hh + 1) * dh] = y.T.astype(BF16)


def _moba(bqk, bvt, slopes):
    _, t, dh = bqk.shape
    n_blocks = t // MOBA_BLOCK
    assert n_blocks % MOBA_UNROLL == 0, "the padded last chunk must stay inside the key array"
    n_h = MOBA_HEADS_PER_STEP
    n_groups = N_HEADS_B // n_h
    return pl.pallas_call(
        functools.partial(_moba_kernel, n_blocks=n_blocks),
        grid=(n_groups, n_blocks),
        in_specs=[pl.BlockSpec(memory_space=pltpu.SMEM),
                  pl.BlockSpec((n_h, MOBA_BLOCK, dh), lambda h, i: (h, i, 0)),
                  pl.BlockSpec((n_h, MOBA_BLOCK, dh), lambda h, i: (h, jnp.minimum(i + 1, n_blocks - 1), 0)),
                  pl.BlockSpec((n_h, t, dh), lambda h, i: (n_groups + h, 0, 0), pipeline_mode=pl.Buffered(1)),
                  pl.BlockSpec((n_h, n_blocks, MOBA_VT_ROWS, MOBA_BLOCK), lambda h, i: (h, 0, 0, 0),
                               pipeline_mode=pl.Buffered(1))],
        out_specs=pl.BlockSpec((MOBA_BLOCK, n_h * dh), lambda h, i: (i, h)),
        out_shape=jax.ShapeDtypeStruct((t, N_HEADS_B * dh), BF16),
        scratch_shapes=[pltpu.VMEM((n_h, n_blocks, dh), F32),
                        pltpu.VMEM((n_h, n_blocks, MOBA_BLOCK), F32),
                        pltpu.VMEM((n_h, MOBA_VT_ROWS, MOBA_BLOCK), F32),
                        pltpu.VMEM((2, n_h, MOBA_UNROLL * MOBA_BLOCK, MOBA_BLOCK), F32),
                        pltpu.VMEM((2, n_h, MOBA_BLOCK, dh), BF16),
                        pltpu.SMEM((1,), jnp.int32)],
        compiler_params=_params("arbitrary", "arbitrary"),
        name="moba",
    )(slopes, bqk, bqk, bqk, bvt)


def _merge_kernel(x_ref, ada_ref, ya_ref, yb_ref, gates_a_ref, gates_b_ref, wa_ref, wb_ref, wo_ref, o_ref):
    pa = jnp.dot(ya_ref[...], wa_ref[...], preferred_element_type=F32)
    pb = jnp.dot(yb_ref[...], wb_ref[...], preferred_element_type=F32)
    merged = gates_a_ref[...].astype(F32) * pa + gates_b_ref[...].astype(F32) * pb
    out = jnp.dot(merged.astype(BF16), wo_ref[...], preferred_element_type=F32)
    o_ref[...] = x_ref[...] + _ada_rows(ada_ref, 1)[2] * out


def _merge(x, ada, y_a, y_b, gates, w_a, w_b, w_o, *, tm=512):
    t, d = x.shape
    const = lambda i: (0, 0)
    resident = dict(index_map=const, pipeline_mode=pl.Buffered(1))
    return pl.pallas_call(
        _merge_kernel,
        grid=(t // tm,),
        in_specs=[pl.BlockSpec((tm, d), lambda i: (i, 0)),
                  pl.BlockSpec((N_ADA, d), const),
                  pl.BlockSpec((tm, y_a.shape[1]), lambda i: (i, 0)),
                  pl.BlockSpec((tm, y_b.shape[1]), lambda i: (i, 0)),
                  pl.BlockSpec((tm, d), lambda i: (i, 0)),
                  pl.BlockSpec((tm, d), lambda i: (i, 1)),
                  pl.BlockSpec(w_a.shape, **resident),
                  pl.BlockSpec(w_b.shape, **resident),
                  pl.BlockSpec(w_o.shape, **resident)],
        out_specs=pl.BlockSpec((tm, d), lambda i: (i, 0)),
        out_shape=jax.ShapeDtypeStruct((t, d), F32),
        compiler_params=_params("parallel"),
        name="merge",
    )(x, ada, y_a, y_b, gates, gates, w_a, w_b, w_o)


def _layer(x, c, w_ada, b_ada, g_ffn1, ffn1_w_gate, ffn1_w_up, ffn1_w_down,
           g_mix, w_in, q_norm, k_norm, w_gate, w_branch_a, w_branch_b, w_out,
           g_ffn2, ffn2_w_gate, ffn2_w_up, ffn2_w_down):
    t, d = x.shape
    ada = _ada(c, w_ada, b_ada).reshape(N_ADA, d)
    slopes = jnp.exp2(-8.0 * jnp.arange(1, N_HEADS + 1, dtype=F32) / N_HEADS) * LOG2_E

    x = _ffn(x, ada, g_ffn1, ffn1_w_gate, ffn1_w_up, ffn1_w_down, sub=0)

    scale = HEAD_DIM ** -0.5 * LOG2_E
    n_gate_heads = w_gate.shape[1] // HEAD_DIM
    gains = jnp.concatenate([q_norm[:N_HEADS_A] * scale, k_norm[:N_HEADS_A], jnp.ones((N_HEADS_A, HEAD_DIM), F32),
                             q_norm[N_HEADS_A:] * scale, k_norm[N_HEADS_A:],
                             jnp.ones((N_HEADS_B + n_gate_heads, HEAD_DIM), F32)], axis=0)
    norm_flags = jnp.concatenate([jnp.ones((2 * N_DIL,), F32), jnp.zeros((N_DIL,), F32),
                                  jnp.ones((BLK_BV0 - BLK_BQK0,), F32),
                                  jnp.zeros((BLK_GATE0 - BLK_BV0 + n_gate_heads // HPB,), F32)])
    offsets = jnp.arange(MOBA_BLOCK, dtype=F32)
    key_weights = jnp.exp2(slopes[N_HEADS_A:, None, None] * offsets[None, None, :])
    key_weights = jnp.pad(key_weights, ((0, 0), (0, MOBA_VT_PAD - 1), (0, 0)))
    a0, a1, a2, bqk, bvt, gates = _proj(_norm(x, ada, g_mix, sub=1), w_in, w_gate,
                                        gains, norm_flags, key_weights)

    y_a = _dilated((a0, a1, a2), slopes)
    y_b = _moba(bqk, bvt, slopes)

    x = _merge(x, ada, y_a, y_b, gates, w_branch_a.astype(BF16), w_branch_b.astype(BF16), w_out.astype(BF16))
    x = _ffn(x, ada, g_ffn2, ffn2_w_gate, ffn2_w_up, ffn2_w_down, sub=2)
    return x


def kernel(x, c, w_ada, b_ada, g_ffn1, ffn1_w_gate, ffn1_w_up, ffn1_w_down, g_mix, w_in, q_norm, k_norm,
           w_gate, w_branch_a, w_branch_b, w_out, g_ffn2, ffn2_w_gate, ffn2_w_up, ffn2_w_down):
    batch, depth = x.shape[0], w_ada.shape[0]
    outs = []
    for bi in range(batch):
        xb = x[bi]
        for l in range(depth):
            xb = _layer(xb, c[bi], w_ada[l], b_ada[l], g_ffn1[l], ffn1_w_gate[l], ffn1_w_up[l],
                        ffn1_w_down[l], g_mix[l], w_in[l], q_norm[l], k_norm[l], w_gate[l],
                        w_branch_a[l], w_branch_b[l], w_out[l], g_ffn2[l], ffn2_w_gate[l],
                        ffn2_w_up[l], ffn2_w_down[l])
        outs.append(xb)
    return jnp.stack(outs, axis=0)
```

```python
import functools

import jax
import jax.numpy as jnp
from jax import lax
from jax.experimental import pallas as pl
from jax.experimental.pallas import tpu as pltpu

HEAD_DIM = 128
DIL_PATTERNS = ((128, 1), (512, 4), (2048, 16))
N_DIL = len(DIL_PATTERNS)
HEADS_PER_DIL_GROUP = 4
N_HEADS_A = HEADS_PER_DIL_GROUP * N_DIL
N_HEADS_B = 8
N_HEADS = N_HEADS_A + N_HEADS_B
BAND = 128
MOBA_BLOCK = 256
MOBA_TOPK = 3
N_ADA = 9
EPS = 1e-6
NEG_INF = -1e30
LOG2_E = 1.4426950408889634

LANES = 128
VMEM_LIMIT = 60 * 1024 * 1024

BF16 = jnp.bfloat16
F32 = jnp.float32

_NT = (((1,), (1,)), ((), ()))


def _params(*sem):
    return pltpu.CompilerParams(dimension_semantics=sem, vmem_limit_bytes=VMEM_LIMIT)


def _sigmoid(x):
    return 1.0 / (1.0 + jnp.exp(-x))


def _ada_rows(ada_ref, sub):
    return tuple(ada_ref[3 * sub + k:3 * sub + k + 1, :] for k in range(3))


def _norm_modulate(x, g, shift, scale):
    ms = jnp.mean(x * x, axis=-1, keepdims=True)
    return ((x * lax.rsqrt(ms + EPS)) * (g * (1.0 + scale)) + shift).astype(BF16)


def _ada_kernel(c_ref, w_ref, b_ref, o_ref):
    c = c_ref[...]
    s = c * _sigmoid(c)
    o_ref[...] = jnp.sum(s * w_ref[...], axis=0, keepdims=True) + b_ref[...]


def _ada(c, w_ada, b_ada, *, tn=1536):
    d, n = w_ada.shape
    return pl.pallas_call(
        _ada_kernel,
        grid=(n // tn,),
        in_specs=[pl.BlockSpec((d, 1), lambda j: (0, 0)),
                  pl.BlockSpec((d, tn), lambda j: (0, j)),
                  pl.BlockSpec((1, tn), lambda j: (0, j))],
        out_specs=pl.BlockSpec((1, tn), lambda j: (0, j)),
        out_shape=jax.ShapeDtypeStruct((1, n), F32),
        compiler_params=_params("arbitrary"),
        name="ada",
    )(c.reshape(d, 1), w_ada, b_ada.reshape(1, n))


FFN_FIRST_ROWS = 512


def _ffn_kernel(x_ref, ada_ref, g_ref, wg_ref, wu_ref, wd_ref, o_ref, u_ref, *, sub):
    f = pl.program_id(1)

    def swiglu_tile(u, wg, wu, wd):
        hg = jnp.dot(u, wg, preferred_element_type=F32)
        hu = jnp.dot(u, wu, preferred_element_type=F32)
        h = (hg * _sigmoid(hg)) * hu
        return jnp.dot(h.astype(BF16), wd, preferred_element_type=F32)

    @pl.when(f == 0)
    def _():
        wg, wu, wd = wg_ref[...].astype(BF16), wu_ref[...].astype(BF16), wd_ref[...].astype(BF16)
        shift, scale, _ = _ada_rows(ada_ref, sub)
        for c in range(x_ref.shape[0] // FFN_FIRST_ROWS):
            rows = slice(c * FFN_FIRST_ROWS, (c + 1) * FFN_FIRST_ROWS)
            u = _norm_modulate(x_ref[rows, :], g_ref[...], shift, scale)
            u_ref[rows, :] = u
            o_ref[rows, :] = swiglu_tile(u, wg, wu, wd)

    @pl.when(f > 0)
    def _():
        o_ref[...] += swiglu_tile(u_ref[...], wg_ref[...].astype(BF16), wu_ref[...].astype(BF16),
                                  wd_ref[...].astype(BF16))

    @pl.when(f == pl.num_programs(1) - 1)
    def _():
        gate = _ada_rows(ada_ref, sub)[2]
        o_ref[...] = x_ref[...] + (0.5 * gate) * o_ref[...]


def _ffn(x, ada, g, w_gate, w_up, w_down, *, sub, tm=1024, tf=256):
    t, d = x.shape
    dff = w_gate.shape[1]
    return pl.pallas_call(
        functools.partial(_ffn_kernel, sub=sub),
        grid=(t // tm, dff // tf),
        in_specs=[pl.BlockSpec((tm, d), lambda i, f: (i, 0)),
                  pl.BlockSpec((N_ADA, d), lambda i, f: (0, 0)),
                  pl.BlockSpec((1, d), lambda i, f: (0, 0)),
                  pl.BlockSpec((d, tf), lambda i, f: (0, f)),
                  pl.BlockSpec((d, tf), lambda i, f: (0, f)),
                  pl.BlockSpec((tf, d), lambda i, f: (f, 0))],
        out_specs=pl.BlockSpec((tm, d), lambda i, f: (i, 0)),
        out_shape=jax.ShapeDtypeStruct((t, d), F32),
        scratch_shapes=[pltpu.VMEM((tm, d), BF16)],
        compiler_params=_params("parallel", "arbitrary"),
        name=f"ffn{sub}",
    )(x, ada, g.reshape(1, d), w_gate, w_up, w_down)


NORM_ROWS = 128


def _norm_kernel(x_ref, ada_ref, g_ref, u_ref, *, sub):
    shift, scale, _ = _ada_rows(ada_ref, sub)

    def body(c, carry):
        rows = pl.ds(pl.multiple_of(c * NORM_ROWS, NORM_ROWS), NORM_ROWS)
        u_ref[rows, :] = _norm_modulate(x_ref[rows, :], g_ref[...], shift, scale)
        return carry

    lax.fori_loop(0, x_ref.shape[0] // NORM_ROWS, body, 0)


def _norm(x, ada, g, *, sub, tm=1024):
    t, d = x.shape
    return pl.pallas_call(
        functools.partial(_norm_kernel, sub=sub),
        grid=(t // tm,),
        in_specs=[pl.BlockSpec((tm, d), lambda i: (i, 0)),
                  pl.BlockSpec((N_ADA, d), lambda i: (0, 0)),
                  pl.BlockSpec((1, d), lambda i: (0, 0))],
        out_specs=pl.BlockSpec((tm, d), lambda i: (i, 0)),
        out_shape=jax.ShapeDtypeStruct((t, d), BF16),
        compiler_params=_params("parallel"),
        name="norm",
    )(x, ada, g.reshape(1, d))


HPB = HEADS_PER_DIL_GROUP
PROJ_TN = HPB * HEAD_DIM
BLK_A_END = 3 * N_DIL
BLK_BQK0 = BLK_A_END
BLK_BV0 = BLK_BQK0 + 2 * (N_HEADS_B // HPB)
BLK_GATE0 = BLK_BV0 + N_HEADS_B // HPB


MOBA_VT_PAD = 16
MOBA_VT_ROWS = HEAD_DIM + MOBA_VT_PAD


PROJ_ROWS = MOBA_BLOCK


def _proj_kernel(nflag_ref, u_ref, win_ref, wgate_ref, gain_ref, kw_ref,
                 a0_ref, a1_ref, a2_ref, bqk_ref, bvt_ref, gate_ref, stg_ref, stg2_ref):
    j = pl.program_id(1)
    n_chunks = u_ref.shape[0] // PROJ_ROWS

    def chunks(w_ref):
        w = w_ref[...].astype(BF16)
        for c in range(n_chunks):
            rows = slice(c * PROJ_ROWS, (c + 1) * PROJ_ROWS)
            yield c, jnp.dot(u_ref[rows, :], w, preferred_element_type=F32)

    def head(res, h):
        a = res[:, h * HEAD_DIM:(h + 1) * HEAD_DIM]
        flag = nflag_ref[j]
        ms = jnp.mean(a * a, axis=-1, keepdims=True)
        return (a * (lax.rsqrt(ms + EPS) * flag + (1.0 - flag))) * gain_ref[j, h:h + 1, :]

    def dilated_block(o_ref, r):
        n = PROJ_ROWS // r
        for c, res in chunks(win_ref):
            for h in range(HPB):
                y = head(res, h)
                if r == 1:
                    o_ref[h, c * n:(c + 1) * n, :] = y.astype(BF16)
                    continue
                stg = stg_ref.at[h]
                stg[...] = y
                if r == 16:
                    stg2 = stg2_ref
                    m = PROJ_ROWS // 4
                    for a in range(4):
                        stg2[a * m:(a + 1) * m, :] = stg[pl.ds(a, m, stride=4), :]
                    parts = {a + 4 * b: stg2[pl.ds(a * m + b, n, stride=4), :] for a in range(4) for b in range(4)}
                else:
                    parts = {rho: stg[pl.ds(rho, n, stride=r), :] for rho in range(r)}
                for rho in range(r):
                    o_ref[h, c * n:(c + 1) * n, rho * HEAD_DIM:(rho + 1) * HEAD_DIM] = parts[rho].astype(BF16)

    a_refs = (a0_ref, a1_ref, a2_ref)
    for g, (_, r) in enumerate(DIL_PATTERNS):
        pl.when((j < BLK_A_END) & (lax.rem(j, N_DIL) == g))(functools.partial(dilated_block, a_refs[g], r))

    @pl.when((j >= BLK_BQK0) & (j < BLK_BV0))
    def _():
        for c, res in chunks(win_ref):
            for h in range(HPB):
                bqk_ref[h, c * PROJ_ROWS:(c + 1) * PROJ_ROWS, :] = head(res, h).astype(BF16)

    @pl.when((j >= BLK_BV0) & (j < BLK_GATE0))
    def _():
        for c, res in chunks(win_ref):
            for h in range(HPB):
                kw = kw_ref[j - BLK_BV0, h]
                yt = res[:, h * HEAD_DIM:(h + 1) * HEAD_DIM].T
                bvt_ref[h, c, 0:HEAD_DIM, :] = (yt * kw[0:1, :]).astype(BF16)
                bvt_ref[h, c, HEAD_DIM:, :] = kw.astype(BF16)

    @pl.when(j >= BLK_GATE0)
    def _():
        for c, res in chunks(wgate_ref):
            gate_ref[c * PROJ_ROWS:(c + 1) * PROJ_ROWS, :] = _sigmoid(res).astype(BF16)


def _proj(u, w_in, w_gate, gains, norm_flags, key_weights, *, tm=2048):
    t, d = u.shape
    n_gate = w_gate.shape[1] // PROJ_TN
    n_blk = BLK_GATE0 + n_gate
    assert w_in.shape[1] == BLK_GATE0 * PROJ_TN
    n_bv = BLK_GATE0 - BLK_BV0

    def a_spec(gi, r):
        return pl.BlockSpec((HPB, tm // r, r * HEAD_DIM),
                            lambda i, j: (jnp.clip((j - gi + N_DIL - 1) // N_DIL, 0, 2), i, 0))

    a_shapes = [jax.ShapeDtypeStruct((3 * HPB, t // r, r * HEAD_DIM), BF16) for _, r in DIL_PATTERNS]
    out_shape = a_shapes + [
        jax.ShapeDtypeStruct((2 * N_HEADS_B, t, HEAD_DIM), BF16),
        jax.ShapeDtypeStruct((N_HEADS_B, t // MOBA_BLOCK, MOBA_VT_ROWS, MOBA_BLOCK), BF16),
        jax.ShapeDtypeStruct((t, n_gate * PROJ_TN), BF16)]
    out_specs = [a_spec(gi, r) for gi, (_, r) in enumerate(DIL_PATTERNS)] + [
        pl.BlockSpec((HPB, tm, HEAD_DIM),
                     lambda i, j: (jnp.clip(j - BLK_BQK0, 0, BLK_BV0 - BLK_BQK0 - 1), i, 0)),
        pl.BlockSpec((HPB, tm // MOBA_BLOCK, MOBA_VT_ROWS, MOBA_BLOCK),
                     lambda i, j: (jnp.clip(j - BLK_BV0, 0, n_bv - 1), i, 0, 0)),
        pl.BlockSpec((tm, PROJ_TN), lambda i, j: (i, jnp.clip(j - BLK_GATE0, 0, n_gate - 1)))]
    return pl.pallas_call(
        _proj_kernel,
        grid=(t // tm, n_blk),
        in_specs=[pl.BlockSpec(memory_space=pltpu.SMEM),
                  pl.BlockSpec((tm, d), lambda i, j: (i, 0)),
                  pl.BlockSpec((d, PROJ_TN), lambda i, j: (0, jnp.minimum(j, BLK_GATE0 - 1))),
                  pl.BlockSpec((d, PROJ_TN), lambda i, j: (0, jnp.clip(j - BLK_GATE0, 0, n_gate - 1))),
                  pl.BlockSpec((n_blk, HPB, HEAD_DIM), lambda i, j: (0, 0, 0)),
                  pl.BlockSpec((n_bv, HPB, MOBA_VT_PAD, MOBA_BLOCK), lambda i, j: (0, 0, 0, 0))],
        out_specs=out_specs,
        out_shape=out_shape,
        scratch_shapes=[pltpu.VMEM((HPB, PROJ_ROWS, HEAD_DIM), F32), pltpu.VMEM((PROJ_ROWS, HEAD_DIM), F32)],
        compiler_params=_params("arbitrary", "arbitrary"),
        name="proj",
    )(norm_flags, u, w_in, w_gate, gains.reshape(n_blk, HPB, HEAD_DIM),
      key_weights.reshape(n_bv, HPB, MOBA_VT_PAD, MOBA_BLOCK))


DIL_TOKENS = BAND * max(r for _, r in DIL_PATTERNS)


def _band_blocks(blocks, store):
    scores = [lax.dot_general(q, k2, _NT, preferred_element_type=F32) + bias
              for q, k2, _, bias, _ in blocks]
    soft = []
    for s in scores:
        m = jnp.max(s, axis=-1, keepdims=True)
        p = jnp.exp2(s - m)
        denom = jnp.sum(p, axis=-1, keepdims=True)
        soft.append((p.astype(BF16), denom, jnp.broadcast_to(m + jnp.log2(denom), (BAND, LANES))))
    for (p, denom, lse), (_, _, v2, _, args) in zip(soft, blocks):
        o = jnp.dot(p, v2, preferred_element_type=F32) / denom
        store(*args, o, lse)


def _dilated_kernel(slopes_ref, *refs):
    groups = [refs[5 * g:5 * g + 5] for g in range(N_DIL)]
    y_ref, o_scr, l_scr = refs[5 * N_DIL:]
    j = pl.program_id(0)
    b = pl.program_id(1)

    qi = lax.broadcasted_iota(jnp.int32, (BAND, 2 * BAND), 0)
    ki = lax.broadcasted_iota(jnp.int32, (BAND, 2 * BAND), 1)
    dist = BAND + qi - ki
    in_band = (dist >= 0) & (dist <= BAND)
    dist_f = dist.astype(F32)
    first_ok = (ki + jnp.minimum(b, 1) * BAND) >= BAND

    for g, (_, r) in enumerate(DIL_PATTERNS):
        q_ref, k_ref, v_ref, kp_ref, vp_ref = groups[g]
        slope = slopes_ref[g * HEADS_PER_DIL_GROUP + j]
        bias = jnp.where(in_band, dist_f * (-slope * r), NEG_INF)
        bias_first = jnp.where(first_ok, bias, NEG_INF)
        n_blk = DIL_TOKENS // (r * BAND)

        def store(i, rho, o, lse, g=g, r=r):
            rows = pl.ds(i * (BAND * r) + rho, BAND, stride=r) if r > 1 else pl.ds(i * BAND, BAND)
            o_scr.at[g][rows, :] = o
            l_scr.at[g][rows, :] = lse

        blocks = []
        for rho in range(r):
            cols = slice(rho * HEAD_DIM, (rho + 1) * HEAD_DIM)
            kk = jnp.concatenate([kp_ref[:, cols], k_ref[0:BAND, cols]], axis=0)
            vv = jnp.concatenate([vp_ref[:, cols], v_ref[0:BAND, cols]], axis=0)
            blocks.append((q_ref[0:BAND, cols], kk, vv, bias_first, (0, rho)))
            for i in range(1, n_blk):
                blocks.append((q_ref[i * BAND:(i + 1) * BAND, cols],
                               k_ref[(i - 1) * BAND:(i + 1) * BAND, cols],
                               v_ref[(i - 1) * BAND:(i + 1) * BAND, cols], bias, (i, rho)))
        _band_blocks(blocks, store)

    lses = [l_scr[g] for g in range(N_DIL)]
    mx = functools.reduce(jnp.maximum, lses)
    es = [jnp.exp2(lse - mx) for lse in lses]
    tot = functools.reduce(jnp.add, es)
    y = functools.reduce(jnp.add, [(e / tot) * o_scr[g] for g, e in enumerate(es)])
    y_ref[...] = y.astype(BF16)


def _dilated(a_views, slopes):
    t = a_views[0].shape[1]
    dh = HEAD_DIM
    nb = t // DIL_TOKENS
    hq, hk, hv = 0, HPB, 2 * HPB
    args, specs = [], []
    for g, (_, r) in enumerate(DIL_PATTERNS):
        rows = DIL_TOKENS // r
        prev_per_blk = rows // BAND
        width = r * dh

        def cur(base, rows=rows, width=width):
            return pl.BlockSpec((None, rows, width), lambda j, b: (base + j, b, 0))

        def prev(base, width=width, ppb=prev_per_blk):
            return pl.BlockSpec((None, BAND, width),
                                lambda j, b: (base + j, jnp.maximum(b * ppb - 1, 0), 0))

        args += [a_views[g]] * 5
        specs += [cur(hq), cur(hk), cur(hv), prev(hk), prev(hv)]

    return pl.pallas_call(
        _dilated_kernel,
        grid=(HEADS_PER_DIL_GROUP, nb),
        in_specs=[pl.BlockSpec(memory_space=pltpu.SMEM)] + specs,
        out_specs=pl.BlockSpec((DIL_TOKENS, dh), lambda j, b: (b, j)),
        out_shape=jax.ShapeDtypeStruct((t, HEADS_PER_DIL_GROUP * dh), BF16),
        scratch_shapes=[pltpu.VMEM((N_DIL, DIL_TOKENS, dh), F32),
                        pltpu.VMEM((N_DIL, DIL_TOKENS, LANES), F32)],
        compiler_params=_params("arbitrary", "arbitrary"),
        name="dilated",
    )(slopes, *args)


MOBA_UNROLL = 4


KMEAN_TERMS = 3


MOBA_HEADS_PER_STEP = 8


def _moba_kernel(slopes_ref, q_ref, qn_ref, k_ref, vt_ref, y_ref,
                 kmean_scr, kparts_scr, sel_scr, acc_scr, s_scr, qq_scr, slot_scr, *, n_blocks):
    hg = pl.program_id(0)
    qb = pl.program_id(1)
    blk = MOBA_BLOCK
    n_h, _, dh = q_ref.shape
    heads = range(n_h)
    slope = [slopes_ref[N_HEADS_A + hg * n_h + hh] for hh in heads]
    width = MOBA_UNROLL
    n_chunks = (qb + width - 1) // width

    qq_scr[0] = q_ref[...]
    qq_scr[1] = qn_ref[...]

    def chunk_scores(hh, ci, which_q, slot):
        start = pl.multiple_of(ci * (width * blk), width * blk)
        s_scr[slot, hh] = lax.dot_general(k_ref[hh, pl.ds(start, width * blk), :], qq_scr[which_q, hh], _NT,
                                          preferred_element_type=F32)

    @pl.when(qb == 0)
    def _():
        def body(n, carry):
            for hh in heads:
                kn = k_ref[hh, pl.ds(pl.multiple_of(n * blk, blk), blk), :].astype(F32)
                kmean_scr[hh, pl.ds(n, 1), :] = jnp.mean(kn, axis=0, keepdims=True)
            return carry
        lax.fori_loop(0, n_blocks, body, 0)
        for hh in heads:
            rest = kmean_scr[hh]
            for term in range(KMEAN_TERMS):
                part = rest.astype(BF16)
                kparts_scr[hh, term] = part
                rest = rest - part.astype(F32)
        slot_scr[0] = 0
        for hh in heads:
            chunk_scores(hh, 0, 1, 0)

    slot0 = slot_scr[0]
    q = [q_ref[hh] for hh in heads]

    blk_id = lax.broadcasted_iota(jnp.int32, (n_blocks, blk), 0)
    blk_id_f = blk_id.astype(F32)
    past = blk_id < qb
    key_i = lax.broadcasted_iota(jnp.int32, (blk, blk), 0)
    qry_i = lax.broadcasted_iota(jnp.int32, (blk, blk), 1)
    own = pl.multiple_of(qb * blk, blk)

    in_block_max = [slope[hh] * (blk - 1) for hh in heads]

    gates = [functools.reduce(jnp.add, [lax.dot_general(kparts_scr[hh, term], q[hh], _NT,
                                                        preferred_element_type=F32)
                                        for term in range(KMEAN_TERMS)]) for hh in heads]
    s_own = [lax.dot_general(k_ref[hh, pl.ds(own, blk), :], q[hh], _NT, preferred_element_type=F32)
             for hh in heads]

    m0, p_own = [], []
    for hh in heads:
        work = jnp.where(past, gates[hh], NEG_INF)
        sel = jnp.zeros((n_blocks, blk), F32)
        for _ in range(MOBA_TOPK):
            mx = jnp.max(work, axis=0, keepdims=True)
            first = jnp.min(jnp.where(work == mx, blk_id_f, float(n_blocks)), axis=0, keepdims=True)
            hit = blk_id_f == first
            sel = jnp.where(hit, 1.0, sel)
            work = jnp.where(hit, -jnp.inf, work)
        sel_scr[hh] = jnp.where(past, sel, 0.0)

        s = jnp.where(key_i <= qry_i, s_own[hh], NEG_INF)
        m0.append(jnp.max(s, axis=0, keepdims=True) + in_block_max[hh])
        p_own.append(jnp.exp2(s - m0[hh]).astype(BF16))
    for hh in heads:
        acc_scr[hh] = jnp.dot(vt_ref[hh, qb], p_own[hh], preferred_element_type=F32)

    def past_chunk(ci, m_run, src, dst):
        n0 = ci * width
        last = ci == n_chunks - 1
        nxt = jnp.where(last, 0, ci + 1)
        which_q = jnp.where(last, 1, 0)
        out = []
        for hh in heads:
            chunk_scores(hh, nxt, which_q, dst)
            c, chosen = [], []
            m_chunk = jnp.full((1, blk), NEG_INF, F32)
            for a in range(width):
                s = s_scr[src, hh, a * blk:(a + 1) * blk, :]
                c.append(slope[hh] * jnp.full((1, blk), (n0 + a - qb) * blk, jnp.int32).astype(F32))
                chosen.append(sel_scr[hh, pl.ds(n0 + a, 1), :] > 0.5)
                m_blk = jnp.max(s, axis=0, keepdims=True) + (c[a] + in_block_max[hh])
                m_chunk = jnp.maximum(m_chunk, jnp.where(chosen[a], m_blk, NEG_INF))
            pv = jnp.zeros(acc_scr.shape[1:], F32)
            for a in range(width):
                s = s_scr[src, hh, a * blk:(a + 1) * blk, :]
                p = jnp.exp2(s - jnp.where(chosen[a], m_chunk - c[a], -NEG_INF))
                pv = pv + jnp.dot(vt_ref[hh, n0 + a], p.astype(BF16), preferred_element_type=F32)
            m_new = jnp.maximum(m_run[hh], m_chunk)
            alpha = jnp.exp2(m_run[hh] - m_new)
            beta = jnp.exp2(m_chunk - m_new)
            acc_scr[hh] = alpha * acc_scr[hh] + beta * pv
            out.append(m_new)
        return tuple(out)

    def body(ci, m_run):
        return lax.cond(lax.rem(slot0 + ci, 2) == 0,
                        lambda m: past_chunk(ci, m, 0, 1), lambda m: past_chunk(ci, m, 1, 0), m_run)

    lax.fori_loop(0, n_chunks, body, tuple(m0))
    slot_scr[0] = lax.rem(slot0 + n_chunks, 2)
    for hh in heads:
        y = acc_scr[hh, 0:dh, :] / acc_scr[hh, dh:dh + 1, :]
        y_ref[:, hh * dh:(hh + 1) * dh] = y.T.astype(BF16)


def _moba(bqk, bvt, slopes):
    _, t, dh = bqk.shape
    n_blocks = t // MOBA_BLOCK
    assert n_blocks % MOBA_UNROLL == 0, "the padded last chunk must stay inside the key array"
    n_h = MOBA_HEADS_PER_STEP
    n_groups = N_HEADS_B // n_h
    return pl.pallas_call(
        functools.partial(_moba_kernel, n_blocks=n_blocks),
        grid=(n_groups, n_blocks),
        in_specs=[pl.BlockSpec(memory_space=pltpu.SMEM),
                  pl.BlockSpec((n_h, MOBA_BLOCK, dh), lambda h, i: (h, i, 0)),
                  pl.BlockSpec((n_h, MOBA_BLOCK, dh), lambda h, i: (h, jnp.minimum(i + 1, n_blocks - 1), 0)),
                  pl.BlockSpec((n_h, t, dh), lambda h, i: (n_groups + h, 0, 0), pipeline_mode=pl.Buffered(1)),
                  pl.BlockSpec((n_h, n_blocks, MOBA_VT_ROWS, MOBA_BLOCK), lambda h, i: (h, 0, 0, 0),
                               pipeline_mode=pl.Buffered(1))],
        out_specs=pl.BlockSpec((MOBA_BLOCK, n_h * dh), lambda h, i: (i, h)),
        out_shape=jax.ShapeDtypeStruct((t, N_HEADS_B * dh), BF16),
        scratch_shapes=[pltpu.VMEM((n_h, n_blocks, dh), F32),
                        pltpu.VMEM((n_h, KMEAN_TERMS, n_blocks, dh), BF16),
                        pltpu.VMEM((n_h, n_blocks, MOBA_BLOCK), F32),
                        pltpu.VMEM((n_h, MOBA_VT_ROWS, MOBA_BLOCK), F32),
                        pltpu.VMEM((2, n_h, MOBA_UNROLL * MOBA_BLOCK, MOBA_BLOCK), F32),
                        pltpu.VMEM((2, n_h, MOBA_BLOCK, dh), BF16),
                        pltpu.SMEM((1,), jnp.int32)],
        compiler_params=_params("arbitrary", "arbitrary"),
        name="moba",
    )(slopes, bqk, bqk, bqk, bvt)


def _merge_kernel(x_ref, ada_ref, ya_ref, yb_ref, gates_a_ref, gates_b_ref, wa_ref, wb_ref, wo_ref, o_ref):
    pa = jnp.dot(ya_ref[...], wa_ref[...], preferred_element_type=F32)
    pb = jnp.dot(yb_ref[...], wb_ref[...], preferred_element_type=F32)
    merged = gates_a_ref[...].astype(F32) * pa + gates_b_ref[...].astype(F32) * pb
    out = jnp.dot(merged.astype(BF16), wo_ref[...], preferred_element_type=F32)
    o_ref[...] = x_ref[...] + _ada_rows(ada_ref, 1)[2] * out


def _merge(x, ada, y_a, y_b, gates, w_a, w_b, w_o, *, tm=512):
    t, d = x.shape
    const = lambda i: (0, 0)
    resident = dict(index_map=const, pipeline_mode=pl.Buffered(1))
    return pl.pallas_call(
        _merge_kernel,
        grid=(t // tm,),
        in_specs=[pl.BlockSpec((tm, d), lambda i: (i, 0)),
                  pl.BlockSpec((N_ADA, d), const),
                  pl.BlockSpec((tm, y_a.shape[1]), lambda i: (i, 0)),
                  pl.BlockSpec((tm, y_b.shape[1]), lambda i: (i, 0)),
                  pl.BlockSpec((tm, d), lambda i: (i, 0)),
                  pl.BlockSpec((tm, d), lambda i: (i, 1)),
                  pl.BlockSpec(w_a.shape, **resident),
                  pl.BlockSpec(w_b.shape, **resident),
                  pl.BlockSpec(w_o.shape, **resident)],
        out_specs=pl.BlockSpec((tm, d), lambda i: (i, 0)),
        out_shape=jax.ShapeDtypeStruct((t, d), F32),
        compiler_params=_params("parallel"),
        name="merge",
    )(x, ada, y_a, y_b, gates, gates, w_a, w_b, w_o)


def _layer(x, c, w_ada, b_ada, g_ffn1, ffn1_w_gate, ffn1_w_up, ffn1_w_down,
           g_mix, w_in, q_norm, k_norm, w_gate, w_branch_a, w_branch_b, w_out,
           g_ffn2, ffn2_w_gate, ffn2_w_up, ffn2_w_down):
    t, d = x.shape
    ada = _ada(c, w_ada, b_ada).reshape(N_ADA, d)
    slopes = jnp.exp2(-8.0 * jnp.arange(1, N_HEADS + 1, dtype=F32) / N_HEADS) * LOG2_E

    x = _ffn(x, ada, g_ffn1, ffn1_w_gate, ffn1_w_up, ffn1_w_down, sub=0)

    scale = HEAD_DIM ** -0.5 * LOG2_E
    n_gate_heads = w_gate.shape[1] // HEAD_DIM
    gains = jnp.concatenate([q_norm[:N_HEADS_A] * scale, k_norm[:N_HEADS_A], jnp.ones((N_HEADS_A, HEAD_DIM), F32),
                             q_norm[N_HEADS_A:] * scale, k_norm[N_HEADS_A:],
                             jnp.ones((N_HEADS_B + n_gate_heads, HEAD_DIM), F32)], axis=0)
    norm_flags = jnp.concatenate([jnp.ones((2 * N_DIL,), F32), jnp.zeros((N_DIL,), F32),
                                  jnp.ones((BLK_BV0 - BLK_BQK0,), F32),
                                  jnp.zeros((BLK_GATE0 - BLK_BV0 + n_gate_heads // HPB,), F32)])
    offsets = jnp.arange(MOBA_BLOCK, dtype=F32)
    key_weights = jnp.exp2(slopes[N_HEADS_A:, None, None] * offsets[None, None, :])
    key_weights = jnp.pad(key_weights, ((0, 0), (0, MOBA_VT_PAD - 1), (0, 0)))
    a0, a1, a2, bqk, bvt, gates = _proj(_norm(x, ada, g_mix, sub=1), w_in, w_gate,
                                        gains, norm_flags, key_weights)

    y_a = _dilated((a0, a1, a2), slopes)
    y_b = _moba(bqk, bvt, slopes)

    x = _merge(x, ada, y_a, y_b, gates, w_branch_a.astype(BF16), w_branch_b.astype(BF16), w_out.astype(BF16))
    x = _ffn(x, ada, g_ffn2, ffn2_w_gate, ffn2_w_up, ffn2_w_down, sub=2)
    return x


def kernel(x, c, w_ada, b_ada, g_ffn1, ffn1_w_gate, ffn1_w_up, ffn1_w_down, g_mix, w_in, q_norm, k_norm,
           w_gate, w_branch_a, w_branch_b, w_out, g_ffn2, ffn2_w_gate, ffn2_w_up, ffn2_w_down):
    batch, depth = x.shape[0], w_ada.shape[0]
    outs = []
    for bi in range(batch):
        xb = x[bi]
        for l in range(depth):
            xb = _layer(xb, c[bi], w_ada[l], b_ada[l], g_ffn1[l], ffn1_w_gate[l], ffn1_w_up[l],
                        ffn1_w_down[l], g_mix[l], w_in[l], q_norm[l], k_norm[l], w_gate[l],
                        w_branch_a[l], w_branch_b[l], w_out[l], g_ffn2[l], ffn2_w_gate[l],
                        ffn2_w_up[l], ffn2_w_down[l])
        outs.append(xb)
    return jnp.stack(outs, axis=0)
```
